```python
import math
import jax, jax.numpy as jnp
from jax import lax
import numpy as np

D_MODEL = 1024
BATCH = 16
SEQ = 4096
DEPTH = 4
DEC_BATCH = 8
DEC_SEQ = 64
PAST_LEN = 4096

CHUNK = 64
Q_BLOCK = 128
HEAD_DIM = 64
H_A = D_MODEL // 256
H_IDX = 4
D_IDX = 64
TOPK_MAX = 256
H_B = D_MODEL // 256
H_C = D_MODEL // 128
C_BAND_CHUNKS = 8
C_BAND = C_BAND_CHUNKS * CHUNK
REL_CLIP = 256
N_BUCKETS = 32
T5_MAX_DIST = 1024
N_GROUPS = 4
EXP_PER_GROUP = 4
N_EXPERTS = N_GROUPS * EXP_PER_GROUP
D_EXPERT = D_MODEL // 2
TOP_K_EXPERTS = 2
RMS_EPS = 1e-6
MIX_A = H_A * HEAD_DIM
MIX_B = H_B * HEAD_DIM
MIX_C = H_C * HEAD_DIM
MIX_W = MIX_A + MIX_B + MIX_C
IN_SIZES = (MIX_A, HEAD_DIM, HEAD_DIM, H_IDX * D_IDX, D_IDX, H_IDX,
            MIX_B, MIX_B, MIX_B, MIX_C, MIX_C, MIX_C)
D_IN = sum(IN_SIZES)

kernel_name = 'hybrid_stream_encoder_step'


def rmsnorm(x, g):
    x32 = x.astype(jnp.float32)
    y = x32 * lax.rsqrt(jnp.mean(x32 * x32, axis=-1, keepdims=True) + RMS_EPS)
    return (y * g.astype(jnp.float32)).astype(x.dtype)


def modulate(h, shift, scale):
    return h * (1 + scale[:, None, :]) + shift[:, None, :]


def ada_mods(c, w, b):
    m = jnp.einsum('bd,de->be', jax.nn.silu(c), w) + b
    return jnp.split(m, 6, axis=-1)


def t5_bucket(rel):
    nb = N_BUCKETS // 2
    max_exact = nb // 2
    ret = jnp.where(rel > 0, nb, 0).astype(jnp.int32)
    n = jnp.abs(rel)
    n_f = jnp.maximum(n, 1).astype(jnp.float32)
    large = max_exact + (jnp.log(n_f / max_exact) / math.log(T5_MAX_DIST / max_exact)
                         * (nb - max_exact)).astype(jnp.int32)
    large = jnp.minimum(large, nb - 1)
    return ret + jnp.where(n < max_exact, n, large).astype(jnp.int32)


def dyn(a, start, size, axis=1):
    return lax.dynamic_slice_in_dim(a, start, size, axis=axis)


def unblock(o):
    nb, b, qb = o.shape[:3]
    return jnp.moveaxis(o, 0, 1).reshape((b, nb * qb) + o.shape[3:])


def split_cols(p):
    out, start = [], 0
    for n in IN_SIZES:
        out.append(p[..., start:start + n])
        start += n
    return out


def project(h, w_in):
    B, T, _ = h.shape
    qa, ka, va, qi, ki, wi, qb, kb, vb, qc, kc, vc = split_cols(jnp.einsum('btd,de->bte', h, w_in))
    heads = lambda a, n: a.reshape(B, T, n, HEAD_DIM)
    return (heads(qa, H_A), ka, va, qi.reshape(B, T, H_IDX, D_IDX), ki, wi * H_IDX ** -0.5,
            heads(qb, H_B), heads(kb, H_B), heads(vb, H_B),
            heads(qc, H_C), heads(kc, H_C), heads(vc, H_C))


def dsa_attend(q, qi, wi, k, v, ki, q_pos, k_pos, t5_table, topk):
    f32 = jnp.float32
    s_idx = jnp.einsum('bthi,bsi->bths', qi.astype(f32), ki.astype(f32)) * D_IDX ** -0.5
    score = jnp.einsum('bth,bths->bts', wi.astype(f32), jax.nn.relu(s_idx))
    q_chunk = q_pos // CHUNK
    admissible = (k_pos[None, :] // CHUNK) <= q_chunk[:, None]
    score = jnp.where(admissible[None], score, -jnp.inf)
    _, idx = lax.top_k(score, topk)
    sel_pos = k_pos[idx]
    valid = (sel_pos // CHUNK) <= q_chunk[None, :, None]
    gather = jax.vmap(lambda rows, ii: rows[ii])
    k_sel = gather(k, idx).astype(f32)
    v_sel = gather(v, idx).astype(f32)
    logits = jnp.einsum('bthd,btkd->bthk', q.astype(f32), k_sel) * HEAD_DIM ** -0.5
    bias = t5_table.astype(f32)[t5_bucket(sel_pos - q_pos[None, :, None])]
    logits = jnp.where(valid[:, :, None, :], logits + jnp.moveaxis(bias, -1, 2), -jnp.inf)
    p = jax.nn.softmax(logits, axis=-1)
    return jnp.einsum('bthk,btkd->bthd', p, v_sel).astype(q.dtype)


def sb_attend(q, k, v, q_pos, k_pos):
    f32 = jnp.float32
    z = jnp.einsum('bthd,bshd->bhts', q.astype(f32), k.astype(f32)) * HEAD_DIM ** -0.5
    causal = (k_pos[None, :] < q_pos[:, None])[None, None]
    log_1m = jnp.where(causal, jax.nn.log_sigmoid(-z), 0.0)
    after = lax.cumsum(log_1m, axis=3, reverse=True) - log_1m
    a = jnp.where(causal, jnp.exp(jax.nn.log_sigmoid(z) + after), 0.0)
    return jnp.einsum('bhts,bshd->bthd', a, v.astype(f32)).astype(q.dtype)


def band_attend(q, k, v, q_pos, k_pos, rel_table):
    f32 = jnp.float32
    logits = jnp.einsum('bthd,bshd->bhts', q.astype(f32), k.astype(f32)) * HEAD_DIM ** -0.5
    rel = jnp.clip(k_pos[None, :] - q_pos[:, None], -REL_CLIP, REL_CLIP) + REL_CLIP
    logits = logits + jnp.transpose(rel_table.astype(f32)[rel], (2, 0, 1))[None]
    qc = q_pos[:, None] // CHUNK
    kc = k_pos[None, :] // CHUNK
    mask = (kc <= qc) & (kc >= qc - C_BAND_CHUNKS) & (k_pos[None, :] >= 0)
    p = jax.nn.softmax(jnp.where(mask[None, None], logits, -jnp.inf), axis=-1)
    return jnp.einsum('bhts,bshd->bthd', p, v.astype(f32)).astype(q.dtype)


def prompt_mixers(parts, t5_table, rel_table):
    qa, ka, va, qi, ki, wi, qb, kb, vb, qc, kc, vc = parts
    B, T = qa.shape[:2]
    pos = jnp.arange(T)
    topk = min(TOPK_MAX, T // 4)

    def dsa_blk(i):
        s = i * Q_BLOCK
        return dsa_attend(dyn(qa, s, Q_BLOCK), dyn(qi, s, Q_BLOCK), dyn(wi, s, Q_BLOCK),
                          ka, va, ki, dyn(pos, s, Q_BLOCK, 0), pos, t5_table, topk)

    def sb_blk(i):
        s = i * Q_BLOCK
        return sb_attend(dyn(qb, s, Q_BLOCK), kb, vb, dyn(pos, s, Q_BLOCK, 0), pos)

    pad = jnp.zeros((B, C_BAND) + kc.shape[2:], kc.dtype)
    kc_ext = jnp.concatenate([pad, kc], axis=1)
    vc_ext = jnp.concatenate([pad, vc], axis=1)
    kpos_ext = jnp.arange(-C_BAND, T)

    def band_blk(i):
        s = i * CHUNK
        return band_attend(dyn(qc, s, CHUNK), dyn(kc_ext, s, C_BAND + CHUNK),
                           dyn(vc_ext, s, C_BAND + CHUNK), dyn(pos, s, CHUNK, 0),
                           dyn(kpos_ext, s, C_BAND + CHUNK, 0), rel_table)

    oa = unblock(lax.map(dsa_blk, jnp.arange(T // Q_BLOCK)))
    ob = unblock(lax.map(sb_blk, jnp.arange(T // Q_BLOCK)))
    oc = unblock(lax.map(band_blk, jnp.arange(T // CHUNK)))
    nbuf = min(C_BAND, T)
    state = (ka, va, ki, kb, vb, kc[:, T - nbuf:], vc[:, T - nbuf:])
    return oa, ob, oc, state


def sample_mixers(parts, ca_k, ca_v, ca_ki, cb_k, cb_v, cc_k, cc_v, t5_table, rel_table):
    qa, ka, va, qi, ki, wi, qb, kb, vb, qc, kc, vc = parts
    T = qa.shape[1]
    P = ca_k.shape[1]
    cbuf = cc_k.shape[1]
    q_pos = P + jnp.arange(T)
    k_pos = jnp.arange(P + T)
    topk = min(TOPK_MAX, (P + T) // 4)
    oa = dsa_attend(qa, qi, wi, jnp.concatenate([ca_k, ka], 1), jnp.concatenate([ca_v, va], 1),
                    jnp.concatenate([ca_ki, ki], 1), q_pos, k_pos, t5_table, topk)
    ob = sb_attend(qb, jnp.concatenate([cb_k, kb], 1), jnp.concatenate([cb_v, vb], 1), q_pos, k_pos)
    kc_f = jnp.concatenate([cc_k, kc], 1)
    vc_f = jnp.concatenate([cc_v, vc], 1)
    oc = band_attend(qc, kc_f, vc_f, q_pos, jnp.arange(P - cbuf, P + T), rel_table)
    state = (ka, va, ki, kb, vb, kc_f[:, T:], vc_f[:, T:])
    return oa, ob, oc, state


def merge_branches(h, oa, ob, oc, w_gate, b_gate, w_branch, w_out):
    B, T, _ = h.shape
    ya = jnp.einsum('btm,md->btd', oa.reshape(B, T, MIX_A), w_branch[:MIX_A])
    yb = jnp.einsum('btm,md->btd', ob.reshape(B, T, MIX_B), w_branch[MIX_A:MIX_A + MIX_B])
    yc = jnp.einsum('btm,md->btd', oc.reshape(B, T, MIX_C), w_branch[MIX_A + MIX_B:])
    g = jax.nn.sigmoid(jnp.einsum('btd,de->bte', h, w_gate) + b_gate)
    ga, gb, gc = jnp.split(g, 3, axis=-1)
    return jnp.einsum('btd,de->bte', ga * ya + gb * yb + gc * yc, w_out)


def hier_moe(h, w_rg, b_rg, w_re, b_re, w1, w3, w2):
    f32 = jnp.float32
    B, T, _ = h.shape
    lg = jnp.einsum('btd,dg->btg', h, w_rg).astype(f32) + b_rg.astype(f32)
    pg_top, g_top = lax.top_k(jax.nn.softmax(lg, axis=-1), 1)
    le = (jnp.einsum('btd,de->bte', h, w_re).astype(f32) + b_re.astype(f32)).reshape(
        B, T, N_GROUPS, EXP_PER_GROUP)
    le_grp = jnp.take_along_axis(le, g_top[..., None], axis=2)[:, :, 0, :]
    le_top, e_top = lax.top_k(le_grp, TOP_K_EXPERTS)
    combine = pg_top * jax.nn.softmax(le_top, axis=-1)
    expert_id = g_top * EXP_PER_GROUP + e_top
    gates = jnp.einsum('btk,btke->bte', combine, jax.nn.one_hot(expert_id, N_EXPERTS, dtype=f32))
    y = jnp.zeros(h.shape, f32)
    for e in range(N_EXPERTS):
        a = jnp.einsum('btd,df->btf', h, w1[e])
        b = jnp.einsum('btd,df->btf', h, w3[e])
        out = jnp.einsum('btf,fd->btd', jax.nn.silu(a) * b, w2[e]).astype(f32)
        y = y + gates[..., e:e + 1] * out
    return y.astype(h.dtype)


def block(x, mods, lp, mixer_fn, *mixer_args):
    n1, n2, w_in, w_gate, b_gate, w_branch, w_out, w_rg, b_rg, w_re, b_re, w1, w3, w2 = lp
    sh1, sc1, g1, sh2, sc2, g2 = mods
    h = modulate(rmsnorm(x, n1), sh1, sc1)
    oa, ob, oc, state = mixer_fn(project(h, w_in), *mixer_args)
    x = x + g1[:, None, :] * merge_branches(h, oa, ob, oc, w_gate, b_gate, w_branch, w_out)
    h2 = modulate(rmsnorm(x, n2), sh2, sc2)
    x = x + g2[:, None, :] * hier_moe(h2, w_rg, b_rg, w_re, b_re, w1, w3, w2)
    return x, state


def stack_layers(states, i):
    return jnp.stack([s[i] for s in states], axis=0)


def setup_inputs(seed: int = 0) -> dict:
    key = jax.random.key(seed)
    ks = iter(jax.random.split(key, 40))
    nrm = lambda shape, scale=1.0: jax.random.normal(next(ks), shape, jnp.float32) * scale
    D = D_MODEL
    c_buf = min(C_BAND, PAST_LEN)
    return {
        'x_prompt': nrm((BATCH, SEQ, D)),
        'x_sample': nrm((DEC_BATCH, DEC_SEQ, D)),
        'c_prompt': nrm((BATCH, D)),
        'c_sample': nrm((DEC_BATCH, D)),
        'cache_a_k': nrm((DEPTH, DEC_BATCH, PAST_LEN, HEAD_DIM)),
        'cache_a_v': nrm((DEPTH, DEC_BATCH, PAST_LEN, HEAD_DIM)),
        'cache_a_kidx': nrm((DEPTH, DEC_BATCH, PAST_LEN, D_IDX)),
        'cache_b_k': nrm((DEPTH, DEC_BATCH, PAST_LEN, H_B, HEAD_DIM)),
        'cache_b_v': nrm((DEPTH, DEC_BATCH, PAST_LEN, H_B, HEAD_DIM)),
        'cache_c_k': nrm((DEPTH, DEC_BATCH, c_buf, H_C, HEAD_DIM)),
        'cache_c_v': nrm((DEPTH, DEC_BATCH, c_buf, H_C, HEAD_DIM)),
        'norm1': 1.0 + nrm((DEPTH, D), 0.05),
        'norm2': 1.0 + nrm((DEPTH, D), 0.05),
        'final_norm': 1.0 + nrm((D,), 0.05),
        'w_ada': nrm((DEPTH, D, 6 * D), 0.5 * D ** -0.5),
        'b_ada': nrm((DEPTH, 6 * D), 0.02),
        'w_in': nrm((DEPTH, D, D_IN), D ** -0.5),
        't5_table': nrm((N_BUCKETS, H_A), 0.5),
        'rel_c': nrm((DEPTH, 2 * REL_CLIP + 1, H_C), 0.5),
        'w_gate': nrm((DEPTH, D, 3 * D), D ** -0.5),
        'b_gate': nrm((DEPTH, 3 * D), 0.02),
        'w_branch': nrm((DEPTH, MIX_W, D), (MIX_W // 3) ** -0.5),
        'w_out': nrm((DEPTH, D, D), D ** -0.5),
        'w_rg': nrm((DEPTH, D, N_GROUPS), D ** -0.5),
        'b_rg': nrm((DEPTH, N_GROUPS), 0.01),
        'w_re': nrm((DEPTH, D, N_EXPERTS), D ** -0.5),
        'b_re': nrm((DEPTH, N_EXPERTS), 0.01),
        'w1': nrm((DEPTH, N_EXPERTS, D, D_EXPERT), D ** -0.5),
        'w3': nrm((DEPTH, N_EXPERTS, D, D_EXPERT), D ** -0.5),
        'w2': nrm((DEPTH, N_EXPERTS, D_EXPERT, D), D_EXPERT ** -0.5),
    }


def reference(x_prompt, x_sample, c_prompt, c_sample, cache_a_k, cache_a_v, cache_a_kidx,
              cache_b_k, cache_b_v, cache_c_k, cache_c_v, norm1, norm2, final_norm, w_ada, b_ada,
              w_in, t5_table, rel_c, w_gate, b_gate, w_branch, w_out, w_rg, b_rg, w_re, b_re,
              w1, w3, w2):
    xp, xs = x_prompt, x_sample
    st_p, st_s = [], []
    for l in range(DEPTH):
        lp = (norm1[l], norm2[l], w_in[l], w_gate[l], b_gate[l], w_branch[l], w_out[l],
              w_rg[l], b_rg[l], w_re[l], b_re[l], w1[l], w3[l], w2[l])
        xp, sp = block(xp, ada_mods(c_prompt, w_ada[l], b_ada[l]), lp,
                       prompt_mixers, t5_table, rel_c[l])
        xs, ss = block(xs, ada_mods(c_sample, w_ada[l], b_ada[l]), lp,
                       sample_mixers, cache_a_k[l], cache_a_v[l], cache_a_kidx[l],
                       cache_b_k[l], cache_b_v[l], cache_c_k[l], cache_c_v[l], t5_table, rel_c[l])
        st_p.append(sp)
        st_s.append(ss)
    y_prompt = rmsnorm(xp, final_norm)
    y_sample = rmsnorm(xs, final_norm)
    a_k_p, a_v_p, a_kidx_p = stack_layers(st_p, 0), stack_layers(st_p, 1), stack_layers(st_p, 2)
    b_k_p, b_v_p = stack_layers(st_p, 3), stack_layers(st_p, 4)
    c_k_p, c_v_p = stack_layers(st_p, 5), stack_layers(st_p, 6)
    a_k_s, a_v_s, a_kidx_s = stack_layers(st_s, 0), stack_layers(st_s, 1), stack_layers(st_s, 2)
    b_k_s, b_v_s = stack_layers(st_s, 3), stack_layers(st_s, 4)
    c_k_s, c_v_s = stack_layers(st_s, 5), stack_layers(st_s, 6)
    return (y_prompt, y_sample, a_k_p, a_v_p, a_kidx_p, b_k_p, b_v_p, c_k_p, c_v_p,
            a_k_s, a_v_s, a_kidx_s, b_k_s, b_v_s, c_k_s, c_v_s)
```

```python
import functools
import math

import jax
import jax.numpy as jnp
from jax import lax
from jax.experimental import pallas as pl
from jax.experimental.pallas import tpu as pltpu

F32 = jnp.float32
BF16 = jnp.bfloat16
HIGHEST = lax.Precision.HIGHEST

CHUNK = 64
HEAD_DIM = 64
D_IDX = 64
H_A = 4
H_IDX = 4
H_B = 4
H_C = 8
C_BAND_CHUNKS = 8
C_BAND = C_BAND_CHUNKS * CHUNK
REL_CLIP = 256
N_BUCKETS = 32
T5_MAX_DIST = 1024
TOPK_MAX = 256
N_GROUPS = 4
EXP_PER_GROUP = 4
N_EXPERTS = N_GROUPS * EXP_PER_GROUP
RMS_EPS = 1e-6

NEG = -1e30
INT_MIN = -(2 ** 31)
LANES = 128
VMEM_LIMIT = 56 * 1024 * 1024

DSA_TQ = 128
DSA_LC1 = 512
DSA_LC3 = 256
SB_T = 128
DSA_NEAR = -(-(DSA_LC3 - 1 + T5_MAX_DIST) // DSA_TQ)

_C_QA, _C_QI, _C_SM, _C_QB, _C_KB, _C_VB, _C_QC, _C_KC, _C_VC, _C_END = (
    0, 256, 512, 768, 1024, 1280, 1536, 2048, 2560, 3072)


def _cparams(sem):
    return pltpu.CompilerParams(dimension_semantics=sem, vmem_limit_bytes=VMEM_LIMIT)


def _nt_dot(a, b):
    return lax.dot_general(a, b, (((1,), (1,)), ((), ())), preferred_element_type=F32)


def _dot(a, b):
    return jnp.dot(a, b, preferred_element_type=F32)


def _rms_mod(x, g, sc, sh):
    ms = jnp.mean(x * x, axis=-1, keepdims=True)
    return (x * lax.rsqrt(ms + RMS_EPS) * g) * (1.0 + sc) + sh


def _ada_kernel(c_ref, w_ref, b_ref, o_ref):
    c = c_ref[...]
    s = c * jax.nn.sigmoid(c)
    o_ref[0] = jnp.dot(s, w_ref[0], preferred_element_type=F32, precision=HIGHEST) + b_ref[0]


def _ada_mods(c_all, w_ada, b_ada):
    depth, d, e = w_ada.shape
    r = c_all.shape[0]
    tn = 1024
    return pl.pallas_call(
        _ada_kernel,
        grid=(depth, e // tn),
        in_specs=[pl.BlockSpec((r, d), lambda l, j: (0, 0)),
                  pl.BlockSpec((1, d, tn), lambda l, j: (l, 0, j)),
                  pl.BlockSpec((1, 1, tn), lambda l, j: (l, 0, j))],
        out_specs=pl.BlockSpec((1, r, tn), lambda l, j: (l, 0, j)),
        out_shape=jax.ShapeDtypeStruct((depth, r, e), F32),
        compiler_params=_cparams(("arbitrary", "arbitrary")),
        name="ada_mods",
    )(c_all, w_ada, b_ada.reshape(depth, 1, e))


def _proj_kernel(x_ref, sc_ref, sh_ref, g_ref, w_ref,
                 h_ref, qa_ref, qi_ref, sm_ref, qb_ref, kbh_ref, vbh_ref, kb_ref, vb_ref,
                 qc_ref, kch_ref, vch_ref, kcs_ref, vcs_ref, *, n_tiles, n_state_tiles):
    i = pl.program_id(1)
    h = _rms_mod(x_ref[0], g_ref[...], sc_ref[0], sh_ref[0])
    hb = h.astype(BF16)
    h_ref[0] = hb

    def mm(lo, hi):
        return _dot(hb, w_ref[:, lo:hi])

    def heads(ref, y, n, scale=None):
        for hh in range(n):
            part = y[:, hh * HEAD_DIM:(hh + 1) * HEAD_DIM]
            if scale is not None:
                part = part * scale
            ref[0, hh] = part.astype(BF16)

    qscale = HEAD_DIM ** -0.5
    heads(qa_ref, mm(_C_QA, _C_QI), H_A, qscale)
    heads(qi_ref, mm(_C_QI, _C_SM), H_IDX, D_IDX ** -0.5)
    sm_ref[0] = mm(_C_SM, _C_QB)
    heads(qb_ref, mm(_C_QB, _C_KB), H_B, qscale)
    kb = mm(_C_KB, _C_VB)
    kb_ref[0] = kb
    heads(kbh_ref, kb, H_B)
    vb = mm(_C_VB, _C_QC)
    vb_ref[0] = vb
    heads(vbh_ref, vb, H_B)
    heads(qc_ref, mm(_C_QC, _C_KC), H_C, qscale)
    kc = mm(_C_KC, _C_VC)
    heads(kch_ref, kc, H_C)
    vc = mm(_C_VC, _C_END)
    heads(vch_ref, vc, H_C)

    @pl.when(i >= n_tiles - n_state_tiles)
    def _():
        kcs_ref[0] = kc
        vcs_ref[0] = vc


def _proj(x, sc, sh, g, w_in_p, tm):
    b, t, d = x.shape
    n_tiles = t // tm
    n_state_tiles = min(C_BAND, t) // tm
    nbuf = n_state_tiles * tm

    def hm(n):
        return (pl.BlockSpec((1, n, tm, HEAD_DIM), lambda bi, i: (bi, 0, i, 0)),
                jax.ShapeDtypeStruct((b, n, t, HEAD_DIM), BF16))

    def tok(width, dtype):
        return (pl.BlockSpec((1, tm, width), lambda bi, i: (bi, i, 0)),
                jax.ShapeDtypeStruct((b, t, width), dtype))

    def state(width):
        return (pl.BlockSpec((1, tm, width),
                             lambda bi, i: (bi, jnp.maximum(i - (n_tiles - n_state_tiles), 0), 0)),
                jax.ShapeDtypeStruct((b, nbuf, width), F32))

    outs = [tok(d, BF16), hm(H_A), hm(H_IDX), tok(256, F32), hm(H_B), hm(H_B), hm(H_B),
            tok(256, F32), tok(256, F32), hm(H_C), hm(H_C), hm(H_C), state(512), state(512)]
    vec = pl.BlockSpec((1, 1, d), lambda bi, i: (bi, 0, 0))
    return pl.pallas_call(
        functools.partial(_proj_kernel, n_tiles=n_tiles, n_state_tiles=n_state_tiles),
        grid=(b, n_tiles),
        in_specs=[pl.BlockSpec((1, tm, d), lambda bi, i: (bi, i, 0)), vec, vec,
                  pl.BlockSpec((1, d), lambda bi, i: (0, 0)),
                  pl.BlockSpec((d, _C_END), lambda bi, i: (0, 0))],
        out_specs=[o[0] for o in outs],
        out_shape=[o[1] for o in outs],
        compiler_params=_cparams(("arbitrary", "arbitrary")),
        name="proj",
    )(x, sc.reshape(b, 1, d), sh.reshape(b, 1, d), g.reshape(1, d), w_in_p)


def _dsa_kernel(qi_ref, qa_ref, wi_ref, ki_ref, ka_ref, vt_ref, bias_ref, o_ref, key_ref,
                *, qoff, l_valid, topk):
    tq, lc1, lc3 = DSA_TQ, DSA_LC1, DSA_LC3
    i = pl.program_id(1)
    q0 = qoff + i * tq
    lane = lax.broadcasted_iota(jnp.int32, (1, tq), 1)
    qpos = q0 + lane
    lim = jnp.minimum((qpos // CHUNK + 1) * CHUNK, l_valid)
    n_adm = jnp.minimum(((q0 + tq - 1) // CHUNK + 1) * CHUNK, l_valid)
    nch1 = (n_adm + lc1 - 1) // lc1
    nch3 = (n_adm + lc3 - 1) // lc3
    qi = qi_ref[0].reshape(H_IDX * tq, D_IDX)
    qa = qa_ref[0].reshape(H_A * tq, HEAD_DIM)
    wi = wi_ref[0]

    def score_chunk(c, carry):
        s0 = pl.multiple_of(c * lc1, lc1)
        s_all = _nt_dot(ki_ref[0, pl.ds(s0, lc1), :], qi)
        sc = None
        for h in range(H_IDX):
            term = wi[h:h + 1, :] * jnp.maximum(s_all[:, h * tq:(h + 1) * tq], 0.0)
            sc = term if sc is None else sc + term
        bits = lax.bitcast_convert_type(sc, jnp.int32)
        key = bits ^ (lax.shift_right_arithmetic(bits, 31) & 0x7FFFFFFF)
        sidx = s0 + lax.broadcasted_iota(jnp.int32, (lc1, tq), 0)
        key_ref[pl.ds(s0, lc1), :] = jnp.where(sidx < lim, key, INT_MIN)
        return carry

    lax.fori_loop(0, nch1, score_chunk, 0)

    def count(pred):
        def body(c, acc):
            s0 = pl.multiple_of(c * lc1, lc1)
            kk = key_ref[pl.ds(s0, lc1), :]
            sidx = s0 + lax.broadcasted_iota(jnp.int32, (lc1, tq), 0)
            return acc + jnp.sum(pred(kk, sidx).reshape(lc1 // 8, 8, tq), axis=0)
        acc = lax.fori_loop(0, nch1, body, jnp.zeros((8, tq), F32))
        return jnp.sum(acc, axis=0, keepdims=True)

    kf = float(topk)

    def bit_step(it, thr):
        cand = thr + lax.shift_left(jnp.int32(1), 31 - it)
        n_ge = count(lambda kk, sidx: jnp.where(kk >= cand, 1.0, 0.0))
        return jnp.where(n_ge >= kf, cand, thr)

    thr = lax.fori_loop(0, 32, bit_step, jnp.full((1, tq), INT_MIN, jnp.int32))
    thr = jnp.maximum(thr, INT_MIN + 1)

    n_gt = count(lambda kk, sidx: jnp.where(kk > thr, 1.0, 0.0))
    n_eq = count(lambda kk, sidx: jnp.where(kk == thr, 1.0, 0.0))
    room = kf - n_gt
    idx_bits = 14
    assert l_valid < (1 << idx_bits)

    def tie_search():
        def step(it, end):
            cand = end + lax.shift_left(jnp.int32(1), idx_bits - 1 - it)
            n = count(lambda kk, sidx: jnp.where(kk == thr, jnp.where(sidx < cand, 1.0, 0.0), 0.0))
            return jnp.where(n <= room, cand, end)
        return lax.fori_loop(0, idx_bits, step, jnp.zeros((1, tq), jnp.int32))

    tie_end = lax.cond(jnp.max(n_eq - room) > 0.0, tie_search,
                       lambda: jnp.full((1, tq), 1 << idx_bits, jnp.int32))

    def attend_chunk(c, carry):
        m, l, acc = carry
        s0 = pl.multiple_of(c * lc3, lc3)
        lg = _nt_dot(ka_ref[0, pl.ds(s0, lc3), :], qa)
        kk = key_ref[pl.ds(s0, lc3), :]
        sidx = s0 + lax.broadcasted_iota(jnp.int32, (lc3, tq), 0)
        madd = jnp.where(kk > thr, 0.0,
                         jnp.where(kk == thr, jnp.where(sidx < tie_end, 0.0, NEG), NEG))
        d = jnp.clip((q0 - s0) // tq, 0, DSA_NEAR)
        lgb = jnp.concatenate(
            [lg[:, h * tq:(h + 1) * tq] + bias_ref[d, h] + madd for h in range(H_A)], axis=1)
        m_new = jnp.maximum(m, jnp.max(lgb, axis=0, keepdims=True))
        alpha = jnp.exp(m - m_new)
        p = jnp.exp(lgb - m_new)
        l = l * alpha + jnp.sum(p, axis=0, keepdims=True)
        acc = acc * alpha + _dot(vt_ref[0, c], p.astype(BF16))
        return m_new, l, acc

    m0 = jnp.full((1, H_A * tq), NEG, F32)
    l0 = jnp.zeros((1, H_A * tq), F32)
    a0 = jnp.zeros((HEAD_DIM, H_A * tq), F32)
    _, l, acc = lax.fori_loop(0, nch3, attend_chunk, (m0, l0, a0))
    out = acc / l
    for h in range(H_A):
        o_ref[0, h] = out[:, h * tq:(h + 1) * tq].astype(BF16)


def _t5_bucket(rel):
    nb = N_BUCKETS // 2
    max_exact = nb // 2
    ret = jnp.where(rel > 0, nb, 0).astype(jnp.int32)
    n = jnp.abs(rel)
    n_f = jnp.maximum(n, 1).astype(F32)
    large = max_exact + (jnp.log(n_f / max_exact) / math.log(T5_MAX_DIST / max_exact)
                         * (nb - max_exact)).astype(jnp.int32)
    large = jnp.minimum(large, nb - 1)
    return ret + jnp.where(n < max_exact, n, large).astype(jnp.int32)


def _dsa_bias_tiles(t5_table):
    d = jnp.arange(DSA_NEAR + 1)[:, None, None]
    s = jnp.arange(DSA_LC3)[None, :, None]
    t = jnp.arange(DSA_TQ)[None, None, :]
    rel = s - t - DSA_TQ * d
    rel = jnp.where(d == DSA_NEAR, -T5_MAX_DIST, rel)
    return jnp.moveaxis(t5_table.astype(F32)[_t5_bucket(rel)], -1, 1)


def _dsa(qi_hm, qa_hm, wi_t, ki, ka, va, bias_tiles, qoff, l_valid, topk):
    b, _, tq_all, _ = qi_hm.shape
    lp = ki.shape[1]
    nq = tq_all // DSA_TQ
    nc3 = lp // DSA_LC3
    vt = jnp.swapaxes(va.reshape(b, nc3, DSA_LC3, HEAD_DIM), 2, 3)
    qspec = pl.BlockSpec((1, H_A, DSA_TQ, HEAD_DIM), lambda bi, i: (bi, 0, i, 0))
    kspec = pl.BlockSpec((1, lp, HEAD_DIM), lambda bi, i: (bi, 0, 0))
    return pl.pallas_call(
        functools.partial(_dsa_kernel, qoff=qoff, l_valid=l_valid, topk=topk),
        grid=(b, nq),
        in_specs=[qspec, qspec,
                  pl.BlockSpec((1, H_IDX, DSA_TQ), lambda bi, i: (bi, 0, i)),
                  kspec, kspec,
                  pl.BlockSpec((1, nc3, HEAD_DIM, DSA_LC3), lambda bi, i: (bi, 0, 0, 0)),
                  pl.BlockSpec(bias_tiles.shape, lambda bi, i: (0, 0, 0, 0))],
        out_specs=pl.BlockSpec((1, H_A, HEAD_DIM, DSA_TQ), lambda bi, i: (bi, 0, 0, i)),
        out_shape=jax.ShapeDtypeStruct((b, H_A, HEAD_DIM, tq_all), BF16),
        scratch_shapes=[pltpu.VMEM((lp, DSA_TQ), jnp.int32)],
        compiler_params=_cparams(("arbitrary", "arbitrary")),
        name="dsa",
    )(qi_hm, qa_hm, wi_t, ki, ka, vt, bias_tiles)


def _sb_kernel(q_ref, k_ref, v_ref, u_ref, o_ref, *, qoff):
    t = SB_T
    i = pl.program_id(2)
    q = q_ref[0, 0]
    u = u_ref[...]
    kb_diag = qoff // t + i
    row = lax.broadcasted_iota(jnp.int32, (t, t), 0)
    col = lax.broadcasted_iota(jnp.int32, (t, t), 1)
    causal = col < row

    def block(kb, carry, acc, diag):
        s0 = pl.multiple_of(kb * t, t)
        k = k_ref[0, 0, pl.ds(s0, t), :]
        v = v_ref[0, 0, pl.ds(s0, t), :]
        z = _nt_dot(q, k)
        sp = jnp.maximum(z, 0.0) + jnp.log1p(jnp.exp(-jnp.abs(z)))
        lm = -sp
        if diag:
            lm = jnp.where(causal, lm, 0.0)
        hi = lm.astype(BF16)
        lo = (lm - hi.astype(F32)).astype(BF16)
        after = _dot(hi, u) + _dot(lo, u)
        a = jnp.exp(z - sp + after + carry)
        if diag:
            a = jnp.where(causal, a, 0.0)
        acc = acc + _dot(a.astype(BF16), v)
        carry = carry + jnp.sum(lm, axis=1, keepdims=True)
        return carry, acc

    carry, acc = block(kb_diag, jnp.zeros((t, 1), F32), jnp.zeros((t, HEAD_DIM), F32), True)

    def body(j, ca):
        return block(kb_diag - 1 - j, ca[0], ca[1], False)

    carry, acc = lax.fori_loop(0, kb_diag, body, (carry, acc))
    o_ref[0, 0] = acc.astype(BF16)


def _sb(q_hm, k_hm, v_hm, qoff):
    b, h, tq_all, _ = q_hm.shape
    lp = k_hm.shape[2]
    t = SB_T
    jj = lax.broadcasted_iota(jnp.int32, (t, t), 0)
    ss = lax.broadcasted_iota(jnp.int32, (t, t), 1)
    u = (jj > ss).astype(BF16)
    kspec = pl.BlockSpec((1, 1, lp, HEAD_DIM), lambda bi, hi, i: (bi, hi, 0, 0))
    qspec = pl.BlockSpec((1, 1, t, HEAD_DIM), lambda bi, hi, i: (bi, hi, i, 0))
    return pl.pallas_call(
        functools.partial(_sb_kernel, qoff=qoff),
        grid=(b, h, tq_all // t),
        in_specs=[qspec, kspec, kspec, pl.BlockSpec((t, t), lambda bi, hi, i: (0, 0))],
        out_specs=qspec,
        out_shape=jax.ShapeDtypeStruct((b, h, tq_all, HEAD_DIM), BF16),
        compiler_params=_cparams(("arbitrary", "arbitrary", "arbitrary")),
        name="sb",
    )(q_hm, k_hm, v_hm, u)


def _band_kernel(q_ref, k_ref, v_ref, bm_ref, o_ref, *, tq, w, n_invalid):
    i = pl.program_id(2)
    r0 = pl.multiple_of(i * tq, tq)
    q = q_ref[0, 0]
    k = k_ref[0, 0, pl.ds(r0, w), :]
    v = v_ref[0, 0, pl.ds(r0, w), :]
    lg = _nt_dot(q, k) + bm_ref[0]
    if n_invalid:
        col = lax.broadcasted_iota(jnp.int32, (1, w), 1)
        lg = lg + jnp.where(r0 + col >= n_invalid, 0.0, NEG)
    m = jnp.max(lg, axis=1, keepdims=True)
    p = jnp.exp(lg - m)
    l = jnp.sum(p, axis=1, keepdims=True)
    o_ref[0, 0] = (_dot(p.astype(BF16), v) / l).astype(BF16)


def _band_bias(rel_table, tq):
    w = C_BAND + tq
    t = jnp.arange(tq)[:, None]
    c = jnp.arange(w)[None, :] - C_BAND
    rel = jnp.clip(c - t, -REL_CLIP, REL_CLIP) + REL_CLIP
    bias = jnp.transpose(rel_table.astype(F32)[rel], (2, 0, 1))
    qc = t // CHUNK
    kc = jnp.floor_divide(c, CHUNK)
    mask = (kc <= qc) & (kc >= qc - C_BAND_CHUNKS)
    return jnp.where(mask[None], bias, NEG)


def _band(q_hm, k_ext, v_ext, bm, tq, n_invalid):
    b, h, t, _ = q_hm.shape
    w = C_BAND + tq
    kspec = pl.BlockSpec((1, 1, C_BAND + t, HEAD_DIM), lambda bi, hi, i: (bi, hi, 0, 0))
    qspec = pl.BlockSpec((1, 1, tq, HEAD_DIM), lambda bi, hi, i: (bi, hi, i, 0))
    return pl.pallas_call(
        functools.partial(_band_kernel, tq=tq, w=w, n_invalid=n_invalid),
        grid=(b, h, t // tq),
        in_specs=[qspec, kspec, kspec, pl.BlockSpec((1, tq, w), lambda bi, hi, i: (hi, 0, 0))],
        out_specs=qspec,
        out_shape=jax.ShapeDtypeStruct((b, h, t, HEAD_DIM), BF16),
        compiler_params=_cparams(("arbitrary", "arbitrary", "arbitrary")),
        name="band",
    )(q_hm, k_ext, v_ext, bm)


def _merge_kernel(x_ref, h_ref, oa_ref, ob_ref, oc_ref, g1_ref, sc2_ref, sh2_ref, n2_ref,
                  wg_ref, bg_ref, wb_ref, wo_ref, wr_ref, br_ref,
                  xo_ref, h2_ref, gates_ref):
    d = x_ref.shape[-1]
    hb = h_ref[0]
    mix = None
    off = 0
    for j, o_ref in enumerate((oa_ref, ob_ref, oc_ref)):
        width = o_ref.shape[-1]
        y = _dot(o_ref[0], wb_ref[off:off + width, :])
        off += width
        g = jax.nn.sigmoid(_dot(hb, wg_ref[:, j * d:(j + 1) * d]) + bg_ref[:, j * d:(j + 1) * d])
        mix = g * y if mix is None else mix + g * y
    x = x_ref[0] + g1_ref[0] * _dot(mix.astype(BF16), wo_ref[...])
    xo_ref[0] = x
    h2 = _rms_mod(x, n2_ref[...], sc2_ref[0], sh2_ref[0])
    h2_ref[0] = h2.astype(BF16)

    lr = jnp.dot(h2, wr_ref[...], preferred_element_type=F32, precision=HIGHEST) + br_ref[...]
    tm = lr.shape[0]
    lane = lax.broadcasted_iota(jnp.int32, (tm, LANES), 1)
    lanef = lane.astype(F32)
    ninf = -jnp.inf
    lg = jnp.where(lane < N_GROUPS, lr[:, :LANES], ninf)
    eg = jnp.exp(lg - jnp.max(lg, axis=1, keepdims=True))
    pg = eg / jnp.sum(eg, axis=1, keepdims=True)
    pg_top = jnp.max(pg, axis=1, keepdims=True)
    g_top = jnp.min(jnp.where(pg == pg_top, lanef, float(LANES)), axis=1, keepdims=True)
    in_group = (lane // EXP_PER_GROUP).astype(F32) == g_top
    le = jnp.where(lane < N_EXPERTS, jnp.where(in_group, lr[:, LANES:], ninf), ninf)
    m1 = jnp.max(le, axis=1, keepdims=True)
    i1 = jnp.min(jnp.where(le == m1, lanef, float(LANES)), axis=1, keepdims=True)
    le2 = jnp.where(lanef == i1, ninf, le)
    m2 = jnp.max(le2, axis=1, keepdims=True)
    i2 = jnp.min(jnp.where(le2 == m2, lanef, float(LANES)), axis=1, keepdims=True)
    e2 = jnp.exp(m2 - m1)
    den = 1.0 + e2
    gates = (jnp.where(lanef == i1, pg_top / den, 0.0)
             + jnp.where(lanef == i2, pg_top * e2 / den, 0.0))
    gates_ref[0] = gates[:, :N_EXPERTS]


def _merge(x, h, oa, ob, oc, g1, sc2, sh2, n2, wg, bg, wb, wo, wr, br, tm):
    b, t, d = x.shape

    def tok(width):
        return pl.BlockSpec((1, tm, width), lambda bi, i: (bi, i, 0))

    vec = pl.BlockSpec((1, 1, d), lambda bi, i: (bi, 0, 0))

    def full(a):
        return pl.BlockSpec(a.shape, lambda bi, i: (0,) * a.ndim)

    n2r, bgr = n2.reshape(1, d), bg.reshape(1, 3 * d)
    return pl.pallas_call(
        _merge_kernel,
        grid=(b, t // tm),
        in_specs=[tok(d), tok(d), tok(oa.shape[-1]), tok(ob.shape[-1]), tok(oc.shape[-1]),
                  vec, vec, vec, full(n2r), full(wg), full(bgr), full(wb), full(wo), full(wr), full(br)],
        out_specs=[tok(d), tok(d), tok(N_EXPERTS)],
        out_shape=[jax.ShapeDtypeStruct((b, t, d), F32), jax.ShapeDtypeStruct((b, t, d), BF16),
                   jax.ShapeDtypeStruct((b, t, N_EXPERTS), F32)],
        compiler_params=_cparams(("arbitrary", "arbitrary")),
        name="merge",
    )(x, h, oa, ob, oc, g1.reshape(b, 1, d), sc2.reshape(b, 1, d), sh2.reshape(b, 1, d),
      n2r, wg, bgr, wb, wo, wr, br)


def _moe_kernel(x_ref, h2_ref, gates_ref, g2_ref, w1_ref, w3_ref, w2_ref, o_ref, acc_ref):
    e = pl.program_id(2)

    @pl.when(e == 0)
    def _():
        acc_ref[...] = jnp.zeros_like(acc_ref)

    hb = h2_ref[0]
    a = _dot(hb, w1_ref[0])
    bb = _dot(hb, w3_ref[0])
    u = (a * jax.nn.sigmoid(a)) * bb
    out = _dot(u.astype(BF16), w2_ref[0])
    gates = gates_ref[0]
    lane = lax.broadcasted_iota(jnp.int32, gates.shape, 1)
    ge = jnp.sum(jnp.where(lane == e, gates, 0.0), axis=1, keepdims=True)
    acc_ref[...] += ge * out

    @pl.when(e == pl.num_programs(2) - 1)
    def _():
        o_ref[0] = x_ref[0] + g2_ref[0] * acc_ref[...]


def _moe(x, h2, gates, g2, w1, w3, w2, tm):
    b, t, d = x.shape
    ne, _, f = w1.shape
    tok = lambda width: pl.BlockSpec((1, tm, width), lambda bi, i, e: (bi, i, 0))
    return pl.pallas_call(
        _moe_kernel,
        grid=(b, t // tm, ne),
        in_specs=[tok(d), tok(d), tok(N_EXPERTS),
                  pl.BlockSpec((1, 1, d), lambda bi, i, e: (bi, 0, 0)),
                  pl.BlockSpec((1, d, f), lambda bi, i, e: (e, 0, 0)),
                  pl.BlockSpec((1, d, f), lambda bi, i, e: (e, 0, 0)),
                  pl.BlockSpec((1, f, d), lambda bi, i, e: (e, 0, 0))],
        out_specs=tok(d),
        out_shape=jax.ShapeDtypeStruct((b, t, d), F32),
        scratch_shapes=[pltpu.VMEM((tm, d), F32)],
        compiler_params=_cparams(("arbitrary", "arbitrary", "arbitrary")),
        name="moe",
    )(x, h2, gates, g2.reshape(b, 1, d), w1, w3, w2)


def _final_norm_kernel(x_ref, g_ref, o_ref):
    x = x_ref[0]
    ms = jnp.mean(x * x, axis=-1, keepdims=True)
    o_ref[0] = x * lax.rsqrt(ms + RMS_EPS) * g_ref[...]


def _final_norm(x, g, tm):
    b, t, d = x.shape
    spec = pl.BlockSpec((1, tm, d), lambda bi, i: (bi, i, 0))
    return pl.pallas_call(
        _final_norm_kernel,
        grid=(b, t // tm),
        in_specs=[spec, pl.BlockSpec((1, d), lambda bi, i: (0, 0))],
        out_specs=spec,
        out_shape=jax.ShapeDtypeStruct((b, t, d), F32),
        compiler_params=_cparams(("arbitrary", "arbitrary")),
        name="final_norm",
    )(x, g.reshape(1, d))


def _prep_layer_weights(w_in, w_gate, w_branch, w_out, w_rg, b_rg, w_re, b_re, w1, w3, w2):
    d = w_in.shape[0]
    qa, ka, va, qi, ki, wi, qb, kb, vb, qc, kc, vc = _split_in(w_in)
    pad = jnp.zeros((d, 256 - (64 * 3 + H_IDX)), w_in.dtype)
    w_in_p = jnp.concatenate([qa, qi, ka, va, ki, wi, pad, qb, kb, vb, qc, kc, vc], axis=1).astype(BF16)
    wr = jnp.zeros((d, 2 * LANES), F32).at[:, :N_GROUPS].set(w_rg).at[:, LANES:LANES + N_EXPERTS].set(w_re)
    br = jnp.zeros((1, 2 * LANES), F32).at[0, :N_GROUPS].set(b_rg).at[0, LANES:LANES + N_EXPERTS].set(b_re)
    return (w_in_p, w_gate.astype(BF16), w_branch.astype(BF16), w_out.astype(BF16), wr, br,
            w1.astype(BF16), w3.astype(BF16), w2.astype(BF16))


def _split_in(w_in):
    sizes = (256, 64, 64, 256, 64, H_IDX, 256, 256, 256, 512, 512, 512)
    out, start = [], 0
    for n in sizes:
        out.append(w_in[:, start:start + n])
        start += n
    return out


def _from_hm(o_hm):
    b, h, t, dh = o_hm.shape
    return jnp.swapaxes(o_hm, 1, 2).reshape(b, t, h * dh)


def _to_hm(a, dtype):
    return jnp.swapaxes(a, 1, 2).astype(dtype)


def _pad_axis(a, axis, size):
    if a.shape[axis] == size:
        return a
    widths = [(0, 0)] * a.ndim
    widths[axis] = (0, size - a.shape[axis])
    return jnp.pad(a, widths)


def _round_up(n, m):
    return -(-n // m) * m


def _layer(x, mods, norms, lw, consts, cache):
    n1, n2 = norms
    sh1, sc1, g1, sh2, sc2, g2 = mods
    w_in_p, wg, bg, wb, wo, wr, br, w1, w3, w2 = lw
    dsa_bias, band_bias = consts
    b, t, d = x.shape
    tm = min(256, t)

    (h, qa_hm, qi_hm, sm, qb_hm, kb_hm, vb_hm, kb, vb, qc_hm, kc_hm, vc_hm, kcs, vcs) = _proj(
        x, sc1, sh1, n1, w_in_p, tm)
    ka, va, ki = sm[..., 0:64], sm[..., 64:128], sm[..., 128:192]
    wi_t = jnp.swapaxes(sm[..., 192:192 + H_IDX], 1, 2) * (H_IDX ** -0.5)

    if cache is None:
        qoff, tq_pad = 0, t
        ka_f, va_f, ki_f = ka, va, ki
        kb_f, vb_f = kb_hm, vb_hm
        kc_f = jnp.pad(kc_hm, ((0, 0), (0, 0), (C_BAND, 0), (0, 0)))
        vc_f = jnp.pad(vc_hm, ((0, 0), (0, 0), (C_BAND, 0), (0, 0)))
        band_tq, n_invalid = 2 * CHUNK, C_BAND
        state = (ka, va, ki, kb.reshape(b, t, H_B, HEAD_DIM), vb.reshape(b, t, H_B, HEAD_DIM),
                 kcs.reshape(b, -1, H_C, HEAD_DIM), vcs.reshape(b, -1, H_C, HEAD_DIM))
    else:
        ca_k, ca_v, ca_ki, cb_k, cb_v, cc_k, cc_v = cache
        qoff = ca_k.shape[1]
        tq_pad = _round_up(t, DSA_TQ)
        ka_f = jnp.concatenate([ca_k, ka], axis=1)
        va_f = jnp.concatenate([ca_v, va], axis=1)
        ki_f = jnp.concatenate([ca_ki, ki], axis=1)
        kb_f = jnp.concatenate([_to_hm(cb_k, BF16), kb_hm], axis=2)
        vb_f = jnp.concatenate([_to_hm(cb_v, BF16), vb_hm], axis=2)
        kc_f = jnp.concatenate([_to_hm(cc_k, BF16), kc_hm], axis=2)
        vc_f = jnp.concatenate([_to_hm(cc_v, BF16), vc_hm], axis=2)
        band_tq, n_invalid = CHUNK, 0
        kc_new = kcs.reshape(b, t, H_C, HEAD_DIM)
        vc_new = vcs.reshape(b, t, H_C, HEAD_DIM)
        state = (ka, va, ki, kb.reshape(b, t, H_B, HEAD_DIM), vb.reshape(b, t, H_B, HEAD_DIM),
                 jnp.concatenate([cc_k, kc_new], axis=1)[:, t:],
                 jnp.concatenate([cc_v, vc_new], axis=1)[:, t:])

    l_valid = ka_f.shape[1]
    topk = min(TOPK_MAX, l_valid // 4)
    lp = _round_up(l_valid, DSA_LC1)
    oa_t = _dsa(_pad_axis(qi_hm, 2, tq_pad), _pad_axis(qa_hm, 2, tq_pad), _pad_axis(wi_t, 2, tq_pad),
                _pad_axis(ki_f.astype(BF16), 1, lp), _pad_axis(ka_f.astype(BF16), 1, lp),
                _pad_axis(va_f.astype(BF16), 1, lp), dsa_bias, qoff, l_valid, topk)
    oa = jnp.transpose(oa_t[..., :t], (0, 3, 1, 2)).reshape(b, t, H_A * HEAD_DIM)

    lp_b = _round_up(qoff + tq_pad, SB_T)
    ob_hm = _sb(_pad_axis(qb_hm, 2, tq_pad), _pad_axis(kb_f, 2, lp_b), _pad_axis(vb_f, 2, lp_b), qoff)
    ob = _from_hm(ob_hm[:, :, :t])

    oc = _from_hm(_band(qc_hm, kc_f, vc_f, band_bias[band_tq], band_tq, n_invalid))

    x, h2, gates = _merge(x, h, oa, ob, oc, g1, sc2, sh2, n2, wg, bg, wb, wo, wr, br, tm)
    x = _moe(x, h2, gates, g2, w1, w3, w2, min(512, t))
    return x, state


def kernel(x_prompt, x_sample, c_prompt, c_sample, cache_a_k, cache_a_v, cache_a_kidx, cache_b_k, cache_b_v, cache_c_k, cache_c_v, norm1, norm2, final_norm, w_ada, b_ada, w_in, t5_table, rel_c, w_gate, b_gate, w_branch, w_out, w_rg, b_rg, w_re, b_re, w1, w3, w2):
    depth = norm1.shape[0]
    bp = x_prompt.shape[0]
    mods_all = _ada_mods(jnp.concatenate([c_prompt, c_sample], axis=0), w_ada, b_ada)
    dsa_bias = _dsa_bias_tiles(t5_table)
    xp, xs = x_prompt, x_sample
    st_p, st_s = [], []
    for l in range(depth):
        lw = _prep_layer_weights(w_in[l], w_gate[l], w_branch[l], w_out[l], w_rg[l], b_rg[l],
                                 w_re[l], b_re[l], w1[l], w3[l], w2[l])
        lw = lw[:2] + (b_gate[l],) + lw[2:]
        band_bias = {tq: _band_bias(rel_c[l], tq) for tq in (CHUNK, 2 * CHUNK)}
        consts = (dsa_bias, band_bias)
        norms = (norm1[l], norm2[l])
        mods = jnp.split(mods_all[l], 6, axis=-1)
        xp, sp = _layer(xp, [m[:bp] for m in mods], norms, lw, consts, None)
        cache = (cache_a_k[l], cache_a_v[l], cache_a_kidx[l], cache_b_k[l], cache_b_v[l],
                 cache_c_k[l], cache_c_v[l])
        xs, ss = _layer(xs, [m[bp:] for m in mods], norms, lw, consts, cache)
        st_p.append(sp)
        st_s.append(ss)
    y_prompt = _final_norm(xp, final_norm, min(512, xp.shape[1]))
    y_sample = _final_norm(xs, final_norm, min(512, xs.shape[1]))
    stack = lambda states, i: jnp.stack([s[i] for s in states], axis=0)
    return ((y_prompt, y_sample) + tuple(stack(st_p, i) for i in range(7))
            + tuple(stack(st_s, i) for i in range(7)))
```

```python
import functools
import math

import jax
import jax.numpy as jnp
from jax import lax
from jax.experimental import pallas as pl
from jax.experimental.pallas import tpu as pltpu

F32 = jnp.float32
BF16 = jnp.bfloat16
HIGHEST = lax.Precision.HIGHEST

CHUNK = 64
HEAD_DIM = 64
D_IDX = 64
H_A = 4
H_IDX = 4
H_B = 4
H_C = 8
C_BAND_CHUNKS = 8
C_BAND = C_BAND_CHUNKS * CHUNK
REL_CLIP = 256
N_BUCKETS = 32
T5_MAX_DIST = 1024
TOPK_MAX = 256
N_GROUPS = 4
EXP_PER_GROUP = 4
N_EXPERTS = N_GROUPS * EXP_PER_GROUP
RMS_EPS = 1e-6

NEG = -1e30
INT_MIN = -(2 ** 31)
LANES = 128
VMEM_LIMIT = 56 * 1024 * 1024

DSA_TQ = 128
DSA_LC1 = 512
DSA_LC3 = 256
SB_T = 256
SB_DEAD = -104.0
BAND_TQ = 2 * CHUNK
DSA_NEAR = -(-(DSA_LC3 - 1 + T5_MAX_DIST) // DSA_TQ)

_C_QA, _C_QI, _C_SM, _C_QB, _C_KB, _C_VB, _C_QC, _C_KC, _C_VC, _C_END = (
    0, 256, 512, 768, 1024, 1280, 1536, 2048, 2560, 3072)


def _cparams(sem):
    return pltpu.CompilerParams(dimension_semantics=sem, vmem_limit_bytes=VMEM_LIMIT)


def _nt_dot(a, b):
    return lax.dot_general(a, b, (((1,), (1,)), ((), ())), preferred_element_type=F32)


def _dot(a, b):
    return jnp.dot(a, b, preferred_element_type=F32)


def _col_reduce(x, op):
    r, c = x.shape
    if r > 64:
        x = op(x.reshape(r // 64, 64, c), axis=0)
    return op(x, axis=0, keepdims=True)


def _rms_mod(x, g, sc, sh):
    ms = jnp.mean(x * x, axis=-1, keepdims=True)
    return (x * lax.rsqrt(ms + RMS_EPS) * g) * (1.0 + sc) + sh


def _ada_kernel(c_ref, w_ref, b_ref, o_ref):
    c = c_ref[...]
    s = c * jax.nn.sigmoid(c)
    o_ref[0] = jnp.dot(s, w_ref[0], preferred_element_type=F32, precision=HIGHEST) + b_ref[0]


def _ada_mods(c_all, w_ada, b_ada):
    depth, d, e = w_ada.shape
    r = c_all.shape[0]
    tn = 1024
    return pl.pallas_call(
        _ada_kernel,
        grid=(depth, e // tn),
        in_specs=[pl.BlockSpec((r, d), lambda l, j: (0, 0)),
                  pl.BlockSpec((1, d, tn), lambda l, j: (l, 0, j)),
                  pl.BlockSpec((1, 1, tn), lambda l, j: (l, 0, j))],
        out_specs=pl.BlockSpec((1, r, tn), lambda l, j: (l, 0, j)),
        out_shape=jax.ShapeDtypeStruct((depth, r, e), F32),
        compiler_params=_cparams(("arbitrary", "arbitrary")),
        name="ada_mods",
    )(c_all, w_ada, b_ada.reshape(depth, 1, e))


def _proj_kernel(x_ref, sc_ref, sh_ref, g_ref, w_ref,
                 h_ref, qa_ref, qi_ref, sm_ref, qb_ref, kbh_ref, vbh_ref, kb_ref, vb_ref,
                 qc_ref, kch_ref, vch_ref, kcs_ref, vcs_ref, *, n_tiles, n_state_tiles):
    i = pl.program_id(1)
    h = _rms_mod(x_ref[0], g_ref[...], sc_ref[0], sh_ref[0])
    hb = h.astype(BF16)
    h_ref[0] = hb

    def mm(lo, hi):
        return _dot(hb, w_ref[:, lo:hi])

    def heads(ref, y, n, scale=None):
        for hh in range(n):
            part = y[:, hh * HEAD_DIM:(hh + 1) * HEAD_DIM]
            if scale is not None:
                part = part * scale
            ref[0, hh] = part.astype(BF16)

    qscale = HEAD_DIM ** -0.5
    heads(qa_ref, mm(_C_QA, _C_QI), H_A, qscale)
    heads(qi_ref, mm(_C_QI, _C_SM), H_IDX, D_IDX ** -0.5)
    sm_ref[0] = mm(_C_SM, _C_QB)
    heads(qb_ref, mm(_C_QB, _C_KB), H_B, qscale)
    kb = mm(_C_KB, _C_VB)
    kb_ref[0] = kb
    heads(kbh_ref, kb, H_B)
    vb = mm(_C_VB, _C_QC)
    vb_ref[0] = vb
    heads(vbh_ref, vb, H_B)
    heads(qc_ref, mm(_C_QC, _C_KC), H_C, qscale)
    kc = mm(_C_KC, _C_VC)
    heads(kch_ref, kc, H_C)
    vc = mm(_C_VC, _C_END)
    heads(vch_ref, vc, H_C)

    @pl.when(i >= n_tiles - n_state_tiles)
    def _():
        kcs_ref[0] = kc
        vcs_ref[0] = vc


def _proj(x, sc, sh, g, w_in_p, tm):
    b, t, d = x.shape
    n_tiles = t // tm
    n_state_tiles = min(C_BAND, t) // tm
    nbuf = n_state_tiles * tm

    def hm(n):
        return (pl.BlockSpec((1, n, tm, HEAD_DIM), lambda bi, i: (bi, 0, i, 0)),
                jax.ShapeDtypeStruct((b, n, t, HEAD_DIM), BF16))

    def tok(width, dtype):
        return (pl.BlockSpec((1, tm, width), lambda bi, i: (bi, i, 0)),
                jax.ShapeDtypeStruct((b, t, width), dtype))

    def state(width):
        return (pl.BlockSpec((1, tm, width),
                             lambda bi, i: (bi, jnp.maximum(i - (n_tiles - n_state_tiles), 0), 0)),
                jax.ShapeDtypeStruct((b, nbuf, width), F32))

    outs = [tok(d, BF16), hm(H_A), hm(H_IDX), tok(256, F32), hm(H_B), hm(H_B), hm(H_B),
            tok(256, F32), tok(256, F32), hm(H_C), hm(H_C), hm(H_C), state(512), state(512)]
    vec = pl.BlockSpec((1, 1, d), lambda bi, i: (bi, 0, 0))
    return pl.pallas_call(
        functools.partial(_proj_kernel, n_tiles=n_tiles, n_state_tiles=n_state_tiles),
        grid=(b, n_tiles),
        in_specs=[pl.BlockSpec((1, tm, d), lambda bi, i: (bi, i, 0)), vec, vec,
                  pl.BlockSpec((1, d), lambda bi, i: (0, 0)),
                  pl.BlockSpec((d, _C_END), lambda bi, i: (0, 0))],
        out_specs=[o[0] for o in outs],
        out_shape=[o[1] for o in outs],
        compiler_params=_cparams(("arbitrary", "arbitrary")),
        name="proj",
    )(x, sc.reshape(b, 1, d), sh.reshape(b, 1, d), g.reshape(1, d), w_in_p)


def _dsa_kernel(qi_ref, qa_ref, wi_ref, ki_ref, ka_ref, vt_ref, bias_ref, o_ref, key_ref,
                *, qoff, l_valid, topk):
    tq, lc1, lc3 = DSA_TQ, DSA_LC1, DSA_LC3
    i = pl.program_id(1)
    q0 = qoff + i * tq
    lane = lax.broadcasted_iota(jnp.int32, (1, tq), 1)
    qpos = q0 + lane
    lim = jnp.minimum((qpos // CHUNK + 1) * CHUNK, l_valid)
    n_adm = jnp.minimum(((q0 + tq - 1) // CHUNK + 1) * CHUNK, l_valid)
    nch1 = (n_adm + lc1 - 1) // lc1
    qi = qi_ref[0].reshape(H_IDX * tq, D_IDX)
    qa = qa_ref[0].reshape(H_A * tq, HEAD_DIM)
    wi = wi_ref[0]

    def score_chunk(c, carry):
        s0 = pl.multiple_of(c * lc1, lc1)
        s_all = _nt_dot(ki_ref[0, pl.ds(s0, lc1), :], qi)
        sc = None
        for h in range(H_IDX):
            term = wi[h:h + 1, :] * jnp.maximum(s_all[:, h * tq:(h + 1) * tq], 0.0)
            sc = term if sc is None else sc + term
        bits = lax.bitcast_convert_type(sc, jnp.int32)
        key = bits ^ (lax.shift_right_arithmetic(bits, 31) & 0x7FFFFFFF)
        sidx = s0 + lax.broadcasted_iota(jnp.int32, (lc1, tq), 0)
        key_ref[pl.ds(s0, lc1), :] = jnp.where(sidx < lim, key, INT_MIN)
        return carry

    lax.fori_loop(0, nch1, score_chunk, 0)

    def count(*preds):
        def body(c, accs):
            s0 = pl.multiple_of(c * lc1, lc1)
            kk = key_ref[pl.ds(s0, lc1), :]
            sidx = s0 + lax.broadcasted_iota(jnp.int32, (lc1, tq), 0)
            return tuple(acc + jnp.sum(pred(kk, sidx).reshape(lc1 // 64, 64, tq), axis=0)
                         for acc, pred in zip(accs, preds))
        accs = lax.fori_loop(0, nch1, body, tuple(jnp.zeros((64, tq), F32) for _ in preds))
        return tuple(jnp.sum(acc, axis=0, keepdims=True) for acc in accs)

    kf = float(topk)

    def bit_step(it, thr):
        cand = thr + lax.shift_left(jnp.int32(1), 31 - it)
        n_ge, = count(lambda kk, sidx: jnp.where(kk >= cand, 1.0, 0.0))
        return jnp.where(n_ge >= kf, cand, thr)

    thr = lax.fori_loop(0, 32, bit_step, jnp.full((1, tq), INT_MIN, jnp.int32))
    thr = jnp.maximum(thr, INT_MIN + 1)

    n_gt, n_eq = count(lambda kk, sidx: jnp.where(kk > thr, 1.0, 0.0),
                       lambda kk, sidx: jnp.where(kk == thr, 1.0, 0.0))
    room = kf - n_gt
    idx_bits = 14
    assert l_valid < (1 << idx_bits)

    def tie_search():
        def step(it, end):
            cand = end + lax.shift_left(jnp.int32(1), idx_bits - 1 - it)
            n, = count(lambda kk, sidx: jnp.where(kk == thr, jnp.where(sidx < cand, 1.0, 0.0), 0.0))
            return jnp.where(n <= room, cand, end)
        return lax.fori_loop(0, idx_bits, step, jnp.zeros((1, tq), jnp.int32))

    tie_end = lax.cond(jnp.max(n_eq - room) > 0.0, tie_search,
                       lambda: jnp.full((1, tq), 1 << idx_bits, jnp.int32))

    def attend_chunk(c, carry):
        m, l, acc = carry
        s0 = pl.multiple_of(c * lc3, lc3)
        lg = _nt_dot(ka_ref[0, pl.ds(s0, lc3), :], qa)
        kk = key_ref[pl.ds(s0, lc3), :]
        sidx = s0 + lax.broadcasted_iota(jnp.int32, (lc3, tq), 0)
        madd = jnp.where(kk > thr, 0.0,
                         jnp.where(kk == thr, jnp.where(sidx < tie_end, 0.0, NEG), NEG))
        d = jnp.clip((q0 - s0) // tq, 0, DSA_NEAR)
        lgb = jnp.concatenate(
            [lg[:, h * tq:(h + 1) * tq] + bias_ref[d, h] + madd for h in range(H_A)], axis=1)
        m_new = jnp.maximum(m, _col_reduce(lgb, jnp.max))
        alpha = jnp.exp(m - m_new)
        p = jnp.exp(lgb - m_new)
        l = l * alpha + _col_reduce(p, jnp.sum)
        acc = acc * alpha + _dot(vt_ref[0, c], p.astype(BF16))
        return m_new, l, acc

    m0 = jnp.full((1, H_A * tq), NEG, F32)
    l0 = jnp.zeros((1, H_A * tq), F32)
    a0 = jnp.zeros((HEAD_DIM, H_A * tq), F32)
    def attend_pair(c1, carry):
        for sub in range(lc1 // lc3):
            carry = attend_chunk(c1 * (lc1 // lc3) + sub, carry)
        return carry

    _, l, acc = lax.fori_loop(0, nch1, attend_pair, (m0, l0, a0))
    out = acc / l
    for h in range(H_A):
        o_ref[0, h] = out[:, h * tq:(h + 1) * tq].astype(BF16)


def _t5_bucket(rel):
    nb = N_BUCKETS // 2
    max_exact = nb // 2
    ret = jnp.where(rel > 0, nb, 0).astype(jnp.int32)
    n = jnp.abs(rel)
    n_f = jnp.maximum(n, 1).astype(F32)
    large = max_exact + (jnp.log(n_f / max_exact) / math.log(T5_MAX_DIST / max_exact)
                         * (nb - max_exact)).astype(jnp.int32)
    large = jnp.minimum(large, nb - 1)
    return ret + jnp.where(n < max_exact, n, large).astype(jnp.int32)


def _toeplitz(vec, n, m):
    length = n + m - 1
    lead = vec.shape[:-1]
    flat = jnp.tile(vec, (1,) * len(lead) + (n + 1,))[..., :n * (length + 1)]
    hankel = flat.reshape(lead + (n, length + 1))[..., :m]
    return hankel[..., ::-1]


def _dsa_bias_tiles(t5_table):
    d = jnp.arange(DSA_NEAR + 1)[:, None]
    k = jnp.arange(DSA_LC3 + DSA_TQ - 1)[None, :]
    rel = k - (DSA_TQ - 1) - DSA_TQ * d
    rel = jnp.where(d == DSA_NEAR, -T5_MAX_DIST, rel)
    vec = jnp.moveaxis(t5_table.astype(F32)[_t5_bucket(rel)], -1, 1)
    return _toeplitz(vec, DSA_LC3, DSA_TQ)


def _dsa(qi_hm, qa_hm, wi_t, ki, ka, va, bias_tiles, qoff, l_valid, topk):
    b, _, tq_all, _ = qi_hm.shape
    lp = ki.shape[1]
    nq = tq_all // DSA_TQ
    nc3 = lp // DSA_LC3
    vt = jnp.swapaxes(va.reshape(b, nc3, DSA_LC3, HEAD_DIM), 2, 3)
    qspec = pl.BlockSpec((1, H_A, DSA_TQ, HEAD_DIM), lambda bi, i: (bi, 0, i, 0))
    kspec = pl.BlockSpec((1, lp, HEAD_DIM), lambda bi, i: (bi, 0, 0))
    return pl.pallas_call(
        functools.partial(_dsa_kernel, qoff=qoff, l_valid=l_valid, topk=topk),
        grid=(b, nq),
        in_specs=[qspec, qspec,
                  pl.BlockSpec((1, H_IDX, DSA_TQ), lambda bi, i: (bi, 0, i)),
                  kspec, kspec,
                  pl.BlockSpec((1, nc3, HEAD_DIM, DSA_LC3), lambda bi, i: (bi, 0, 0, 0)),
                  pl.BlockSpec(bias_tiles.shape, lambda bi, i: (0, 0, 0, 0))],
        out_specs=pl.BlockSpec((1, H_A, HEAD_DIM, DSA_TQ), lambda bi, i: (bi, 0, 0, i)),
        out_shape=jax.ShapeDtypeStruct((b, H_A, HEAD_DIM, tq_all), BF16),
        scratch_shapes=[pltpu.VMEM((lp, DSA_TQ), jnp.int32)],
        compiler_params=_cparams(("arbitrary", "arbitrary")),
        name="dsa",
    )(qi_hm, qa_hm, wi_t, ki, ka, vt, bias_tiles)


def _sb_kernel(q_ref, k_ref, v_ref, u_ref, o_ref, *, qoff):
    t = SB_T
    n_heads = q_ref.shape[1]
    i = pl.program_id(1)
    u = u_ref[...]
    kb_diag = qoff // t + i
    row = lax.broadcasted_iota(jnp.int32, (t, t), 0)
    col = lax.broadcasted_iota(jnp.int32, (t, t), 1)
    causal = col < row

    def block(kb, carries, accs, diag):
        s0 = pl.multiple_of(kb * t, t)
        new_c, new_a = [], []
        for h in range(n_heads):
            k = k_ref[0, h, pl.ds(s0, t), :]
            v = v_ref[0, h, pl.ds(s0, t), :]
            z = _nt_dot(q_ref[0, h], k)
            sp = jnp.maximum(z, 0.0) + jnp.log1p(jnp.exp(-jnp.abs(z)))
            lm = -sp
            if diag:
                lm = jnp.where(causal, lm, 0.0)
            hi = lm.astype(BF16)
            lo = (lm - hi.astype(F32)).astype(BF16)
            ext = _dot(hi, u) + _dot(lo, u)
            e = z - sp + ext[:, :t]
            a = jnp.exp(jnp.concatenate(
                [e[:, j * LANES:(j + 1) * LANES] + carries[h] for j in range(t // LANES)], axis=1))
            if diag:
                a = jnp.where(causal, a, 0.0)
            new_a.append(accs[h] + _dot(a.astype(BF16), v))
            new_c.append(carries[h] + ext[:, t:])
        return tuple(new_c), tuple(new_a)

    zeros_c = tuple(jnp.zeros((t, LANES), F32) for _ in range(n_heads))
    zeros_a = tuple(jnp.zeros((t, HEAD_DIM), F32) for _ in range(n_heads))
    carries, accs = block(kb_diag, zeros_c, zeros_a, True)

    def worst(cs):
        m = cs[0]
        for c in cs[1:]:
            m = jnp.maximum(m, c)
        return jnp.max(m)

    def cond(st):
        return jnp.logical_and(st[0] < kb_diag, st[1] > SB_DEAD)

    def body(st):
        j, _, cs, acs = st
        cs, acs = block(kb_diag - 1 - j, cs, acs, False)
        return j + 1, worst(cs), cs, acs

    _, _, _, accs = lax.while_loop(cond, body, (jnp.int32(0), worst(carries), carries, accs))
    for h in range(n_heads):
        o_ref[0, h] = accs[h].astype(BF16)


def _sb(q_hm, k_hm, v_hm, qoff):
    b, h, tq_all, _ = q_hm.shape
    lp = k_hm.shape[2]
    t = SB_T
    assert qoff % t == 0 and tq_all % t == 0 and lp >= qoff + tq_all
    jj = lax.broadcasted_iota(jnp.int32, (t, t + LANES), 0)
    ss = lax.broadcasted_iota(jnp.int32, (t, t + LANES), 1)
    u = jnp.logical_or(jj > ss, ss >= t).astype(BF16)
    kspec = pl.BlockSpec((1, h, lp, HEAD_DIM), lambda bi, i: (bi, 0, 0, 0))
    qspec = pl.BlockSpec((1, h, t, HEAD_DIM), lambda bi, i: (bi, 0, i, 0))
    return pl.pallas_call(
        functools.partial(_sb_kernel, qoff=qoff),
        grid=(b, tq_all // t),
        in_specs=[qspec, kspec, kspec, pl.BlockSpec((t, t + LANES), lambda bi, i: (0, 0))],
        out_specs=qspec,
        out_shape=jax.ShapeDtypeStruct((b, h, tq_all, HEAD_DIM), BF16),
        compiler_params=_cparams(("arbitrary", "arbitrary")),
        name="sb",
    )(q_hm, k_hm, v_hm, u)


def _band_kernel(q_ref, k_ref, v_ref, bm_ref, o_ref, *, tq, w, n_invalid):
    i = pl.program_id(1)
    if n_invalid:
        col = lax.broadcasted_iota(jnp.int32, (1, w), 1)
        valid = jnp.where(i * tq + col >= n_invalid, 0.0, NEG)
    for h in range(q_ref.shape[1]):
        lg = _nt_dot(q_ref[0, h], k_ref[0, h]) + bm_ref[h]
        if n_invalid:
            lg = lg + valid
        m = jnp.max(lg, axis=1, keepdims=True)
        p = jnp.exp(lg - m)
        l = jnp.sum(p, axis=1, keepdims=True)
        o_ref[0, h] = (_dot(p.astype(BF16), v_ref[0, h]) / l).astype(BF16)


def _band_bias(rel_table, tq):
    w = C_BAND + tq
    t = jnp.arange(tq)[:, None]
    c = jnp.arange(w)[None, :] - C_BAND
    k = jnp.arange(tq + w - 1)
    rel = jnp.clip(w - 1 - C_BAND - k, -REL_CLIP, REL_CLIP) + REL_CLIP
    bias = _toeplitz(rel_table.astype(F32)[rel].T, tq, w)
    qc = t // CHUNK
    kc = jnp.floor_divide(c, CHUNK)
    mask = (kc <= qc) & (kc >= qc - C_BAND_CHUNKS)
    return jnp.where(mask[None], bias, NEG)


def _band(q_hm, k_ext, v_ext, bm, tq, n_invalid):
    b, h, t, _ = q_hm.shape
    w = C_BAND + tq
    kspec = pl.BlockSpec((pl.Element(1), pl.Element(h), pl.Element(w), pl.Element(HEAD_DIM)),
                         lambda bi, i: (bi, 0, i * tq, 0))
    qspec = pl.BlockSpec((1, h, tq, HEAD_DIM), lambda bi, i: (bi, 0, i, 0))
    return pl.pallas_call(
        functools.partial(_band_kernel, tq=tq, w=w, n_invalid=n_invalid),
        grid=(b, t // tq),
        in_specs=[qspec, kspec, kspec, pl.BlockSpec((h, tq, w), lambda bi, i: (0, 0, 0))],
        out_specs=qspec,
        out_shape=jax.ShapeDtypeStruct((b, h, t, HEAD_DIM), BF16),
        compiler_params=_cparams(("arbitrary", "arbitrary")),
        name="band",
    )(q_hm, k_ext, v_ext, bm)


def _merge_kernel(x_ref, h_ref, oa_ref, ob_ref, oc_ref, g1_ref, sc2_ref, sh2_ref, n2_ref,
                  wg_ref, bg_ref, wb_ref, wo_ref, wr_ref, br_ref,
                  xo_ref, h2_ref, gates_ref):
    d = x_ref.shape[-1]
    hb = h_ref[0]
    mix = None
    off = 0
    for j, o_ref in enumerate((oa_ref, ob_ref, oc_ref)):
        width = o_ref.shape[-1]
        y = _dot(o_ref[0], wb_ref[off:off + width, :])
        off += width
        g = jax.nn.sigmoid(_dot(hb, wg_ref[:, j * d:(j + 1) * d]) + bg_ref[:, j * d:(j + 1) * d])
        mix = g * y if mix is None else mix + g * y
    x = x_ref[0] + g1_ref[0] * _dot(mix.astype(BF16), wo_ref[...])
    xo_ref[0] = x
    h2 = _rms_mod(x, n2_ref[...], sc2_ref[0], sh2_ref[0])
    h2_ref[0] = h2.astype(BF16)

    lr = jnp.dot(h2, wr_ref[...], preferred_element_type=F32, precision=HIGHEST) + br_ref[...]
    tm = lr.shape[0]
    lane = lax.broadcasted_iota(jnp.int32, (tm, LANES), 1)
    lanef = lane.astype(F32)
    ninf = -jnp.inf
    lg = jnp.where(lane < N_GROUPS, lr[:, :LANES], ninf)
    eg = jnp.exp(lg - jnp.max(lg, axis=1, keepdims=True))
    pg = eg / jnp.sum(eg, axis=1, keepdims=True)
    pg_top = jnp.max(pg, axis=1, keepdims=True)
    g_top = jnp.min(jnp.where(pg == pg_top, lanef, float(LANES)), axis=1, keepdims=True)
    in_group = (lane // EXP_PER_GROUP).astype(F32) == g_top
    le = jnp.where(lane < N_EXPERTS, jnp.where(in_group, lr[:, LANES:], ninf), ninf)
    m1 = jnp.max(le, axis=1, keepdims=True)
    i1 = jnp.min(jnp.where(le == m1, lanef, float(LANES)), axis=1, keepdims=True)
    le2 = jnp.where(lanef == i1, ninf, le)
    m2 = jnp.max(le2, axis=1, keepdims=True)
    i2 = jnp.min(jnp.where(le2 == m2, lanef, float(LANES)), axis=1, keepdims=True)
    e2 = jnp.exp(m2 - m1)
    den = 1.0 + e2
    gates = (jnp.where(lanef == i1, pg_top / den, 0.0)
             + jnp.where(lanef == i2, pg_top * e2 / den, 0.0))
    gates_ref[0] = gates[:, :N_EXPERTS]


def _merge(x, h, oa, ob, oc, g1, sc2, sh2, n2, wg, bg, wb, wo, wr, br, tm):
    b, t, d = x.shape

    def tok(width):
        return pl.BlockSpec((1, tm, width), lambda bi, i: (bi, i, 0))

    vec = pl.BlockSpec((1, 1, d), lambda bi, i: (bi, 0, 0))

    def full(a):
        return pl.BlockSpec(a.shape, lambda bi, i: (0,) * a.ndim)

    n2r, bgr = n2.reshape(1, d), bg.reshape(1, 3 * d)
    return pl.pallas_call(
        _merge_kernel,
        grid=(b, t // tm),
        in_specs=[tok(d), tok(d), tok(oa.shape[-1]), tok(ob.shape[-1]), tok(oc.shape[-1]),
                  vec, vec, vec, full(n2r), full(wg), full(bgr), full(wb), full(wo), full(wr), full(br)],
        out_specs=[tok(d), tok(d), tok(N_EXPERTS)],
        out_shape=[jax.ShapeDtypeStruct((b, t, d), F32), jax.ShapeDtypeStruct((b, t, d), BF16),
                   jax.ShapeDtypeStruct((b, t, N_EXPERTS), F32)],
        compiler_params=_cparams(("arbitrary", "arbitrary")),
        name="merge",
    )(x, h, oa, ob, oc, g1.reshape(b, 1, d), sc2.reshape(b, 1, d), sh2.reshape(b, 1, d),
      n2r, wg, bgr, wb, wo, wr, br)


def _moe_kernel(x_ref, h2_ref, gates_ref, g2_ref, w1_ref, w3_ref, w2_ref, o_ref, acc_ref):
    e = pl.program_id(2)

    @pl.when(e == 0)
    def _():
        acc_ref[...] = jnp.zeros_like(acc_ref)

    hb = h2_ref[0]
    a = _dot(hb, w1_ref[0])
    bb = _dot(hb, w3_ref[0])
    u = (a * jax.nn.sigmoid(a)) * bb
    out = _dot(u.astype(BF16), w2_ref[0])
    gates = gates_ref[0]
    lane = lax.broadcasted_iota(jnp.int32, gates.shape, 1)
    ge = jnp.sum(jnp.where(lane == e, gates, 0.0), axis=1, keepdims=True)
    acc_ref[...] += ge * out

    @pl.when(e == pl.num_programs(2) - 1)
    def _():
        o_ref[0] = x_ref[0] + g2_ref[0] * acc_ref[...]


def _moe(x, h2, gates, g2, w1, w3, w2, tm):
    b, t, d = x.shape
    ne, _, f = w1.shape
    tok = lambda width: pl.BlockSpec((1, tm, width), lambda bi, i, e: (bi, i, 0))
    return pl.pallas_call(
        _moe_kernel,
        grid=(b, t // tm, ne),
        in_specs=[tok(d), tok(d), tok(N_EXPERTS),
                  pl.BlockSpec((1, 1, d), lambda bi, i, e: (bi, 0, 0)),
                  pl.BlockSpec((1, d, f), lambda bi, i, e: (e, 0, 0)),
                  pl.BlockSpec((1, d, f), lambda bi, i, e: (e, 0, 0)),
                  pl.BlockSpec((1, f, d), lambda bi, i, e: (e, 0, 0))],
        out_specs=tok(d),
        out_shape=jax.ShapeDtypeStruct((b, t, d), F32),
        scratch_shapes=[pltpu.VMEM((tm, d), F32)],
        compiler_params=_cparams(("arbitrary", "arbitrary", "arbitrary")),
        name="moe",
    )(x, h2, gates, g2.reshape(b, 1, d), w1, w3, w2)


def _final_norm_kernel(x_ref, g_ref, o_ref):
    x = x_ref[0]
    ms = jnp.mean(x * x, axis=-1, keepdims=True)
    o_ref[0] = x * lax.rsqrt(ms + RMS_EPS) * g_ref[...]


def _final_norm(x, g, tm):
    b, t, d = x.shape
    spec = pl.BlockSpec((1, tm, d), lambda bi, i: (bi, i, 0))
    return pl.pallas_call(
        _final_norm_kernel,
        grid=(b, t // tm),
        in_specs=[spec, pl.BlockSpec((1, d), lambda bi, i: (0, 0))],
        out_specs=spec,
        out_shape=jax.ShapeDtypeStruct((b, t, d), F32),
        compiler_params=_cparams(("arbitrary", "arbitrary")),
        name="final_norm",
    )(x, g.reshape(1, d))


def _prep_layer_weights(w_in, w_gate, w_branch, w_out, w_rg, b_rg, w_re, b_re, w1, w3, w2):
    d = w_in.shape[0]
    qa, ka, va, qi, ki, wi, qb, kb, vb, qc, kc, vc = _split_in(w_in)
    pad = jnp.zeros((d, 256 - (64 * 3 + H_IDX)), w_in.dtype)
    w_in_p = jnp.concatenate([qa, qi, ka, va, ki, wi, pad, qb, kb, vb, qc, kc, vc], axis=1).astype(BF16)
    wr = jnp.zeros((d, 2 * LANES), F32).at[:, :N_GROUPS].set(w_rg).at[:, LANES:LANES + N_EXPERTS].set(w_re)
    br = jnp.zeros((1, 2 * LANES), F32).at[0, :N_GROUPS].set(b_rg).at[0, LANES:LANES + N_EXPERTS].set(b_re)
    return (w_in_p, w_gate.astype(BF16), w_branch.astype(BF16), w_out.astype(BF16), wr, br,
            w1.astype(BF16), w3.astype(BF16), w2.astype(BF16))


def _split_in(w_in):
    sizes = (256, 64, 64, 256, 64, H_IDX, 256, 256, 256, 512, 512, 512)
    out, start = [], 0
    for n in sizes:
        out.append(w_in[:, start:start + n])
        start += n
    return out


def _from_hm(o_hm):
    b, h, t, dh = o_hm.shape
    return jnp.swapaxes(o_hm, 1, 2).reshape(b, t, h * dh)


def _to_hm(a, dtype):
    return jnp.swapaxes(a, 1, 2).astype(dtype)


def _pad_axis(a, axis, size):
    if a.shape[axis] == size:
        return a
    widths = [(0, 0)] * a.ndim
    widths[axis] = (0, size - a.shape[axis])
    return jnp.pad(a, widths)


def _round_up(n, m):
    return -(-n // m) * m


def _layer(x, mods, norms, lw, consts, cache):
    n1, n2 = norms
    sh1, sc1, g1, sh2, sc2, g2 = mods
    w_in_p, wg, bg, wb, wo, wr, br, w1, w3, w2 = lw
    dsa_bias, band_bias = consts
    b, t, d = x.shape
    tm = min(256, t)

    (h, qa_hm, qi_hm, sm, qb_hm, kb_hm, vb_hm, kb, vb, qc_hm, kc_hm, vc_hm, kcs, vcs) = _proj(
        x, sc1, sh1, n1, w_in_p, tm)
    ka, va, ki = sm[..., 0:64], sm[..., 64:128], sm[..., 128:192]
    wi_t = jnp.swapaxes(sm[..., 192:192 + H_IDX], 1, 2) * (H_IDX ** -0.5)

    if cache is None:
        qoff, tq_pad = 0, t
        ka_f, va_f, ki_f = ka, va, ki
        kb_f, vb_f = kb_hm, vb_hm
        kc_f = jnp.pad(kc_hm, ((0, 0), (0, 0), (C_BAND, 0), (0, 0)))
        vc_f = jnp.pad(vc_hm, ((0, 0), (0, 0), (C_BAND, 0), (0, 0)))
        band_tq, n_invalid = BAND_TQ, C_BAND
        state = (ka, va, ki, kb.reshape(b, t, H_B, HEAD_DIM), vb.reshape(b, t, H_B, HEAD_DIM),
                 kcs.reshape(b, -1, H_C, HEAD_DIM), vcs.reshape(b, -1, H_C, HEAD_DIM))
    else:
        ca_k, ca_v, ca_ki, cb_k, cb_v, cc_k, cc_v = cache
        qoff = ca_k.shape[1]
        tq_pad = _round_up(t, DSA_TQ)
        ka_f = jnp.concatenate([ca_k, ka], axis=1)
        va_f = jnp.concatenate([ca_v, va], axis=1)
        ki_f = jnp.concatenate([ca_ki, ki], axis=1)
        kb_f = jnp.concatenate([_to_hm(cb_k, BF16), kb_hm], axis=2)
        vb_f = jnp.concatenate([_to_hm(cb_v, BF16), vb_hm], axis=2)
        kc_f = jnp.concatenate([_to_hm(cc_k, BF16), kc_hm], axis=2)
        vc_f = jnp.concatenate([_to_hm(cc_v, BF16), vc_hm], axis=2)
        band_tq, n_invalid = CHUNK, 0
        kc_new = kcs.reshape(b, t, H_C, HEAD_DIM)
        vc_new = vcs.reshape(b, t, H_C, HEAD_DIM)
        state = (ka, va, ki, kb.reshape(b, t, H_B, HEAD_DIM), vb.reshape(b, t, H_B, HEAD_DIM),
                 jnp.concatenate([cc_k, kc_new], axis=1)[:, t:],
                 jnp.concatenate([cc_v, vc_new], axis=1)[:, t:])

    l_valid = ka_f.shape[1]
    topk = min(TOPK_MAX, l_valid // 4)
    lp = _round_up(l_valid, DSA_LC1)
    oa_t = _dsa(_pad_axis(qi_hm, 2, tq_pad), _pad_axis(qa_hm, 2, tq_pad), _pad_axis(wi_t, 2, tq_pad),
                _pad_axis(ki_f.astype(BF16), 1, lp), _pad_axis(ka_f.astype(BF16), 1, lp),
                _pad_axis(va_f.astype(BF16), 1, lp), dsa_bias, qoff, l_valid, topk)
    oa = jnp.transpose(oa_t[..., :t], (0, 3, 1, 2)).reshape(b, t, H_A * HEAD_DIM)

    tq_sb = _round_up(t, SB_T)
    lp_b = qoff + tq_sb
    ob_hm = _sb(_pad_axis(qb_hm, 2, tq_sb), _pad_axis(kb_f, 2, lp_b), _pad_axis(vb_f, 2, lp_b), qoff)
    ob = _from_hm(ob_hm[:, :, :t])

    oc = _from_hm(_band(qc_hm, kc_f, vc_f, band_bias[band_tq], band_tq, n_invalid))

    x, h2, gates = _merge(x, h, oa, ob, oc, g1, sc2, sh2, n2, wg, bg, wb, wo, wr, br, tm)
    x = _moe(x, h2, gates, g2, w1, w3, w2, min(512, t))
    return x, state


def kernel(x_prompt, x_sample, c_prompt, c_sample, cache_a_k, cache_a_v, cache_a_kidx, cache_b_k, cache_b_v, cache_c_k, cache_c_v, norm1, norm2, final_norm, w_ada, b_ada, w_in, t5_table, rel_c, w_gate, b_gate, w_branch, w_out, w_rg, b_rg, w_re, b_re, w1, w3, w2):
    depth = norm1.shape[0]
    bp = x_prompt.shape[0]
    mods_all = _ada_mods(jnp.concatenate([c_prompt, c_sample], axis=0), w_ada, b_ada)
    dsa_bias = _dsa_bias_tiles(t5_table)
    xp, xs = x_prompt, x_sample
    st_p, st_s = [], []
    for l in range(depth):
        lw = _prep_layer_weights(w_in[l], w_gate[l], w_branch[l], w_out[l], w_rg[l], b_rg[l],
                                 w_re[l], b_re[l], w1[l], w3[l], w2[l])
        lw = lw[:2] + (b_gate[l],) + lw[2:]
        band_bias = {tq: _band_bias(rel_c[l], tq) for tq in (CHUNK, BAND_TQ)}
        consts = (dsa_bias, band_bias)
        norms = (norm1[l], norm2[l])
        mods = jnp.split(mods_all[l], 6, axis=-1)
        xp, sp = _layer(xp, [m[:bp] for m in mods], norms, lw, consts, None)
        cache = (cache_a_k[l], cache_a_v[l], cache_a_kidx[l], cache_b_k[l], cache_b_v[l],
                 cache_c_k[l], cache_c_v[l])
        xs, ss = _layer(xs, [m[bp:] for m in mods], norms, lw, consts, cache)
        st_p.append(sp)
        st_s.append(ss)
    y_prompt = _final_norm(xp, final_norm, min(512, xp.shape[1]))
    y_sample = _final_norm(xs, final_norm, min(512, xs.shape[1]))
    stack = lambda states, i: jnp.stack([s[i] for s in states], axis=0)
    return ((y_prompt, y_sample) + tuple(stack(st_p, i) for i in range(7))
            + tuple(stack(st_s, i) for i in range(7)))
```

```python
import functools
import math

import jax
import jax.numpy as jnp
from jax import lax
from jax.experimental import pallas as pl
from jax.experimental.pallas import tpu as pltpu

F32 = jnp.float32
BF16 = jnp.bfloat16
HIGHEST = lax.Precision.HIGHEST

CHUNK = 64
HEAD_DIM = 64
D_IDX = 64
H_A = 4
H_IDX = 4
H_B = 4
H_C = 8
C_BAND_CHUNKS = 8
C_BAND = C_BAND_CHUNKS * CHUNK
REL_CLIP = 256
N_BUCKETS = 32
T5_MAX_DIST = 1024
TOPK_MAX = 256
N_GROUPS = 4
EXP_PER_GROUP = 4
N_EXPERTS = N_GROUPS * EXP_PER_GROUP
RMS_EPS = 1e-6

NEG = -1e30
INT_MIN = -(2 ** 31)
LANES = 128
VMEM_LIMIT = 56 * 1024 * 1024

DSA_TQ = 128
DSA_LC1 = 512
DSA_LC3 = 256
SB_T = 256
SB_DEAD = -104.0
BAND_TQ = 2 * CHUNK
DSA_NEAR = -(-(DSA_LC3 - 1 + T5_MAX_DIST) // DSA_TQ)

_C_QA, _C_QI, _C_SM, _C_QB, _C_KB, _C_VB, _C_QC, _C_KC, _C_VC, _C_END = (
    0, 256, 512, 768, 1024, 1280, 1536, 2048, 2560, 3072)


def _cparams(sem):
    return pltpu.CompilerParams(dimension_semantics=sem, vmem_limit_bytes=VMEM_LIMIT)


def _nt_dot(a, b):
    return lax.dot_general(a, b, (((1,), (1,)), ((), ())), preferred_element_type=F32)


def _dot(a, b):
    return jnp.dot(a, b, preferred_element_type=F32)


def _col_reduce(x, op):
    r, c = x.shape
    if r > 64:
        x = op(x.reshape(r // 64, 64, c), axis=0)
    return op(x, axis=0, keepdims=True)


def _rms_mod(x, g, sc, sh):
    ms = jnp.mean(x * x, axis=-1, keepdims=True)
    return (x * lax.rsqrt(ms + RMS_EPS) * g) * (1.0 + sc) + sh


def _ada_kernel(c_ref, w_ref, b_ref, o_ref):
    c = c_ref[...]
    s = c * jax.nn.sigmoid(c)
    o_ref[0] = jnp.dot(s, w_ref[0], preferred_element_type=F32, precision=HIGHEST) + b_ref[0]


def _ada_mods(c_all, w_ada, b_ada):
    depth, d, e = w_ada.shape
    r = c_all.shape[0]
    tn = 1024
    return pl.pallas_call(
        _ada_kernel,
        grid=(depth, e // tn),
        in_specs=[pl.BlockSpec((r, d), lambda l, j: (0, 0)),
                  pl.BlockSpec((1, d, tn), lambda l, j: (l, 0, j)),
                  pl.BlockSpec((1, 1, tn), lambda l, j: (l, 0, j))],
        out_specs=pl.BlockSpec((1, r, tn), lambda l, j: (l, 0, j)),
        out_shape=jax.ShapeDtypeStruct((depth, r, e), F32),
        compiler_params=_cparams(("arbitrary", "arbitrary")),
        name="ada_mods",
    )(c_all, w_ada, b_ada.reshape(depth, 1, e))


def _proj_kernel(x_ref, sc_ref, sh_ref, g_ref, w_ref,
                 h_ref, qa_ref, qi_ref, sm_ref, qb_ref, kbh_ref, vbh_ref, kb_ref, vb_ref,
                 qc_ref, kch_ref, vch_ref, kcs_ref, vcs_ref, *, n_tiles, n_state_tiles):
    i = pl.program_id(1)
    h = _rms_mod(x_ref[0], g_ref[...], sc_ref[0], sh_ref[0])
    hb = h.astype(BF16)
    h_ref[0] = hb

    def mm(lo, hi):
        return _dot(hb, w_ref[:, lo:hi])

    def heads(ref, y, n, scale=None):
        for hh in range(n):
            part = y[:, hh * HEAD_DIM:(hh + 1) * HEAD_DIM]
            if scale is not None:
                part = part * scale
            ref[0, hh] = part.astype(BF16)

    qscale = HEAD_DIM ** -0.5
    heads(qa_ref, mm(_C_QA, _C_QI), H_A, qscale)
    heads(qi_ref, mm(_C_QI, _C_SM), H_IDX, D_IDX ** -0.5)
    sm_ref[0] = mm(_C_SM, _C_QB)
    heads(qb_ref, mm(_C_QB, _C_KB), H_B, qscale)
    kb = mm(_C_KB, _C_VB)
    kb_ref[0] = kb
    heads(kbh_ref, kb, H_B)
    vb = mm(_C_VB, _C_QC)
    vb_ref[0] = vb
    heads(vbh_ref, vb, H_B)
    heads(qc_ref, mm(_C_QC, _C_KC), H_C, qscale)
    kc = mm(_C_KC, _C_VC)
    heads(kch_ref, kc, H_C)
    vc = mm(_C_VC, _C_END)
    heads(vch_ref, vc, H_C)

    @pl.when(i >= n_tiles - n_state_tiles)
    def _():
        kcs_ref[0] = kc
        vcs_ref[0] = vc


def _proj(x, sc, sh, g, w_in_p, tm):
    b, t, d = x.shape
    n_tiles = t // tm
    n_state_tiles = min(C_BAND, t) // tm
    nbuf = n_state_tiles * tm

    def hm(n):
        return (pl.BlockSpec((1, n, tm, HEAD_DIM), lambda bi, i: (bi, 0, i, 0)),
                jax.ShapeDtypeStruct((b, n, t, HEAD_DIM), BF16))

    def tok(width, dtype):
        return (pl.BlockSpec((1, tm, width), lambda bi, i: (bi, i, 0)),
                jax.ShapeDtypeStruct((b, t, width), dtype))

    def state(width):
        return (pl.BlockSpec((1, tm, width),
                             lambda bi, i: (bi, jnp.maximum(i - (n_tiles - n_state_tiles), 0), 0)),
                jax.ShapeDtypeStruct((b, nbuf, width), F32))

    outs = [tok(d, BF16), hm(H_A), hm(H_IDX), tok(256, F32), hm(H_B), hm(H_B), hm(H_B),
            tok(256, F32), tok(256, F32), hm(H_C), hm(H_C), hm(H_C), state(512), state(512)]
    vec = pl.BlockSpec((1, 1, d), lambda bi, i: (bi, 0, 0))
    return pl.pallas_call(
        functools.partial(_proj_kernel, n_tiles=n_tiles, n_state_tiles=n_state_tiles),
        grid=(b, n_tiles),
        in_specs=[pl.BlockSpec((1, tm, d), lambda bi, i: (bi, i, 0)), vec, vec,
                  pl.BlockSpec((1, d), lambda bi, i: (0, 0)),
                  pl.BlockSpec((d, _C_END), lambda bi, i: (0, 0))],
        out_specs=[o[0] for o in outs],
        out_shape=[o[1] for o in outs],
        compiler_params=_cparams(("arbitrary", "arbitrary")),
        name="proj",
    )(x, sc.reshape(b, 1, d), sh.reshape(b, 1, d), g.reshape(1, d), w_in_p)


def _dsa_kernel(qi_ref, qa_ref, wi_ref, ki_ref, ka_ref, vt_ref, bias_ref, o_ref, key_ref, hi_ref, lo_ref,
                *, qoff, l_valid, topk):
    tq, lc1, lc3 = DSA_TQ, DSA_LC1, DSA_LC3
    i = pl.program_id(1)
    q0 = qoff + i * tq
    lane = lax.broadcasted_iota(jnp.int32, (1, tq), 1)
    qpos = q0 + lane
    lim = jnp.minimum((qpos // CHUNK + 1) * CHUNK, l_valid)
    n_adm = jnp.minimum(((q0 + tq - 1) // CHUNK + 1) * CHUNK, l_valid)
    nch1 = (n_adm + lc1 - 1) // lc1
    qi = qi_ref[0].reshape(H_IDX * tq, D_IDX)
    qa = qa_ref[0].reshape(H_A * tq, HEAD_DIM)
    wi = wi_ref[0]

    def score_chunk(c, carry):
        s0 = pl.multiple_of(c * lc1, lc1)
        s_all = _nt_dot(ki_ref[0, pl.ds(s0, lc1), :], qi)
        sc = None
        for h in range(H_IDX):
            term = wi[h:h + 1, :] * jnp.maximum(s_all[:, h * tq:(h + 1) * tq], 0.0)
            sc = term if sc is None else sc + term
        bits = lax.bitcast_convert_type(sc, jnp.int32)
        key = bits ^ (lax.shift_right_arithmetic(bits, 31) & 0x7FFFFFFF)
        sidx = s0 + lax.broadcasted_iota(jnp.int32, (lc1, tq), 0)
        key = jnp.where(sidx < lim, key, INT_MIN)
        key_ref[pl.ds(s0, lc1), :] = key
        hi_ref[pl.ds(s0, lc1), :] = lax.shift_right_arithmetic(key, 16).astype(jnp.int16)
        return carry

    lax.fori_loop(0, nch1, score_chunk, 0)

    i16 = jnp.int16
    pk = 16

    def packed(x32):
        return jnp.broadcast_to(x32, (pk, tq)).astype(i16)[None]

    def count16(ref, pred):
        def body(c, acc):
            s0 = pl.multiple_of(c * lc1, lc1)
            kk = ref[pl.ds(s0, lc1), :].reshape(lc1 // pk, pk, tq)
            one = jnp.where(pred(kk), i16(1), i16(0))
            for j in range(lc1 // (8 * pk)):
                acc = acc + one[j * 8:(j + 1) * 8]
            return acc
        acc = lax.fori_loop(0, nch1, body, jnp.zeros((8, pk, tq), i16))
        return jnp.sum(acc.astype(F32).reshape(8 * pk, tq), axis=0, keepdims=True)

    kf = float(topk)
    half_min = -(1 << 15)

    def search16(ref, base):
        def step(it, t):
            cand = t + lax.shift_left(jnp.int32(1), 15 - it)
            cand16 = packed(cand)
            n = count16(ref, lambda kk: kk >= cand16)
            return jnp.where(base + n >= kf, cand, t)
        return lax.fori_loop(0, 16, step, jnp.full((1, tq), half_min, jnp.int32))

    thi = search16(hi_ref, 0.0)
    thi16 = packed(thi)
    n_hi_gt = count16(hi_ref, lambda kk: kk > thi16)

    def low_chunk(c, carry):
        s0 = pl.multiple_of(c * lc1, lc1)
        k32 = key_ref[pl.ds(s0, lc1), :]
        lo = (k32 & 0xFFFF) + half_min
        lo_ref[pl.ds(s0, lc1), :] = jnp.where(
            lax.shift_right_arithmetic(k32, 16) == thi, lo, half_min).astype(i16)
        return carry

    lax.fori_loop(0, nch1, low_chunk, 0)
    tlo = search16(lo_ref, n_hi_gt)
    thr = thi * (1 << 16) + (tlo - half_min)
    thr = jnp.maximum(thr, INT_MIN + 1)

    idx_bits = 14
    no_idx = (1 << 15) - 1
    assert l_valid < (1 << idx_bits)

    def tie_chunk(c, accs):
        s0 = pl.multiple_of(c * lc1, lc1)
        k32 = key_ref[pl.ds(s0, lc1), :]
        sidx = s0 + lax.broadcasted_iota(jnp.int32, (lc1, tq), 0)
        eq = k32 == thr
        lo_ref[pl.ds(s0, lc1), :] = jnp.where(eq, sidx, no_idx).astype(i16)
        n_gt, n_eq = accs
        n_gt = n_gt + jnp.sum(jnp.where(k32 > thr, 1.0, 0.0).reshape(lc1 // 64, 64, tq), axis=0)
        n_eq = n_eq + jnp.sum(jnp.where(eq, 1.0, 0.0).reshape(lc1 // 64, 64, tq), axis=0)
        return n_gt, n_eq

    zero64 = jnp.zeros((64, tq), F32)
    n_gt, n_eq = lax.fori_loop(0, nch1, tie_chunk, (zero64, zero64))
    n_gt = jnp.sum(n_gt, axis=0, keepdims=True)
    n_eq = jnp.sum(n_eq, axis=0, keepdims=True)
    room = kf - n_gt

    def tie_search():
        def step(it, end):
            cand = end + lax.shift_left(jnp.int32(1), idx_bits - 1 - it)
            cand16 = packed(cand)
            n = count16(lo_ref, lambda kk: kk < cand16)
            return jnp.where(n <= room, cand, end)
        return lax.fori_loop(0, idx_bits, step, jnp.zeros((1, tq), jnp.int32))

    tie_end = lax.cond(jnp.max(n_eq - room) > 0.0, tie_search,
                       lambda: jnp.full((1, tq), 1 << idx_bits, jnp.int32))

    def attend_chunk(c, carry):
        m, l, acc = carry
        s0 = pl.multiple_of(c * lc3, lc3)
        lg = _nt_dot(ka_ref[0, pl.ds(s0, lc3), :], qa)
        kk = key_ref[pl.ds(s0, lc3), :]
        sidx = s0 + lax.broadcasted_iota(jnp.int32, (lc3, tq), 0)
        madd = jnp.where(kk > thr, 0.0,
                         jnp.where(kk == thr, jnp.where(sidx < tie_end, 0.0, NEG), NEG))
        d = jnp.clip((q0 - s0) // tq, 0, DSA_NEAR)
        lgb = jnp.concatenate(
            [lg[:, h * tq:(h + 1) * tq] + bias_ref[d, h] + madd for h in range(H_A)], axis=1)
        m_new = jnp.maximum(m, _col_reduce(lgb, jnp.max))
        alpha = jnp.exp(m - m_new)
        p = jnp.exp(lgb - m_new)
        l = l * alpha + _col_reduce(p, jnp.sum)
        acc = acc * alpha + _dot(vt_ref[0, c], p.astype(BF16))
        return m_new, l, acc

    m0 = jnp.full((1, H_A * tq), NEG, F32)
    l0 = jnp.zeros((1, H_A * tq), F32)
    a0 = jnp.zeros((HEAD_DIM, H_A * tq), F32)
    def attend_pair(c1, carry):
        for sub in range(lc1 // lc3):
            carry = attend_chunk(c1 * (lc1 // lc3) + sub, carry)
        return carry

    _, l, acc = lax.fori_loop(0, nch1, attend_pair, (m0, l0, a0))
    out = acc / l
    for h in range(H_A):
        o_ref[0, h] = out[:, h * tq:(h + 1) * tq].astype(BF16)


def _t5_bucket(rel):
    nb = N_BUCKETS // 2
    max_exact = nb // 2
    ret = jnp.where(rel > 0, nb, 0).astype(jnp.int32)
    n = jnp.abs(rel)
    n_f = jnp.maximum(n, 1).astype(F32)
    large = max_exact + (jnp.log(n_f / max_exact) / math.log(T5_MAX_DIST / max_exact)
                         * (nb - max_exact)).astype(jnp.int32)
    large = jnp.minimum(large, nb - 1)
    return ret + jnp.where(n < max_exact, n, large).astype(jnp.int32)


def _toeplitz(vec, n, m):
    length = n + m - 1
    lead = vec.shape[:-1]
    flat = jnp.tile(vec, (1,) * len(lead) + (n + 1,))[..., :n * (length + 1)]
    hankel = flat.reshape(lead + (n, length + 1))[..., :m]
    return hankel[..., ::-1]


def _dsa_bias_tiles(t5_table):
    d = jnp.arange(DSA_NEAR + 1)[:, None]
    k = jnp.arange(DSA_LC3 + DSA_TQ - 1)[None, :]
    rel = k - (DSA_TQ - 1) - DSA_TQ * d
    rel = jnp.where(d == DSA_NEAR, -T5_MAX_DIST, rel)
    vec = jnp.moveaxis(t5_table.astype(F32)[_t5_bucket(rel)], -1, 1)
    return _toeplitz(vec, DSA_LC3, DSA_TQ)


def _dsa(qi_hm, qa_hm, wi_t, ki, ka, va, bias_tiles, qoff, l_valid, topk):
    b, _, tq_all, _ = qi_hm.shape
    lp = ki.shape[1]
    nq = tq_all // DSA_TQ
    nc3 = lp // DSA_LC3
    vt = jnp.swapaxes(va.reshape(b, nc3, DSA_LC3, HEAD_DIM), 2, 3)
    qspec = pl.BlockSpec((1, H_A, DSA_TQ, HEAD_DIM), lambda bi, i: (bi, 0, i, 0))
    kspec = pl.BlockSpec((1, lp, HEAD_DIM), lambda bi, i: (bi, 0, 0))
    return pl.pallas_call(
        functools.partial(_dsa_kernel, qoff=qoff, l_valid=l_valid, topk=topk),
        grid=(b, nq),
        in_specs=[qspec, qspec,
                  pl.BlockSpec((1, H_IDX, DSA_TQ), lambda bi, i: (bi, 0, i)),
                  kspec, kspec,
                  pl.BlockSpec((1, nc3, HEAD_DIM, DSA_LC3), lambda bi, i: (bi, 0, 0, 0)),
                  pl.BlockSpec(bias_tiles.shape, lambda bi, i: (0, 0, 0, 0))],
        out_specs=pl.BlockSpec((1, H_A, HEAD_DIM, DSA_TQ), lambda bi, i: (bi, 0, 0, i)),
        out_shape=jax.ShapeDtypeStruct((b, H_A, HEAD_DIM, tq_all), BF16),
        scratch_shapes=[pltpu.VMEM((lp, DSA_TQ), jnp.int32), pltpu.VMEM((lp, DSA_TQ), jnp.int16),
                        pltpu.VMEM((lp, DSA_TQ), jnp.int16)],
        compiler_params=_cparams(("arbitrary", "arbitrary")),
        name="dsa",
    )(qi_hm, qa_hm, wi_t, ki, ka, vt, bias_tiles)


def _sb_kernel(q_ref, k_ref, v_ref, u_ref, o_ref, *, qoff):
    t = SB_T
    n_heads = q_ref.shape[1]
    i = pl.program_id(1)
    u = u_ref[...]
    kb_diag = qoff // t + i
    row = lax.broadcasted_iota(jnp.int32, (t, t), 0)
    col = lax.broadcasted_iota(jnp.int32, (t, t), 1)
    causal = col < row

    def block(kb, carries, accs, diag):
        s0 = pl.multiple_of(kb * t, t)
        new_c, new_a = [], []
        for h in range(n_heads):
            k = k_ref[0, h, pl.ds(s0, t), :]
            v = v_ref[0, h, pl.ds(s0, t), :]
            z = _nt_dot(q_ref[0, h], k)
            sp = jnp.maximum(z, 0.0) + jnp.log1p(jnp.exp(-jnp.abs(z)))
            lm = -sp
            if diag:
                lm = jnp.where(causal, lm, 0.0)
            hi = lm.astype(BF16)
            lo = (lm - hi.astype(F32)).astype(BF16)
            ext = _dot(hi, u) + _dot(lo, u)
            e = z - sp + ext[:, :t]
            a = jnp.exp(jnp.concatenate(
                [e[:, j * LANES:(j + 1) * LANES] + carries[h] for j in range(t // LANES)], axis=1))
            if diag:
                a = jnp.where(causal, a, 0.0)
            new_a.append(accs[h] + _dot(a.astype(BF16), v))
            new_c.append(carries[h] + ext[:, t:])
        return tuple(new_c), tuple(new_a)

    zeros_c = tuple(jnp.zeros((t, LANES), F32) for _ in range(n_heads))
    zeros_a = tuple(jnp.zeros((t, HEAD_DIM), F32) for _ in range(n_heads))
    carries, accs = block(kb_diag, zeros_c, zeros_a, True)

    def worst(cs):
        m = cs[0]
        for c in cs[1:]:
            m = jnp.maximum(m, c)
        return jnp.max(m)

    def cond(st):
        return jnp.logical_and(st[0] < kb_diag, st[1] > SB_DEAD)

    def body(st):
        j, _, cs, acs = st
        cs, acs = block(kb_diag - 1 - j, cs, acs, False)
        return j + 1, worst(cs), cs, acs

    _, _, _, accs = lax.while_loop(cond, body, (jnp.int32(0), worst(carries), carries, accs))
    for h in range(n_heads):
        o_ref[0, h] = accs[h].astype(BF16)


def _sb(q_hm, k_hm, v_hm, qoff):
    b, h, tq_all, _ = q_hm.shape
    lp = k_hm.shape[2]
    t = SB_T
    assert qoff % t == 0 and tq_all % t == 0 and lp >= qoff + tq_all
    jj = lax.broadcasted_iota(jnp.int32, (t, t + LANES), 0)
    ss = lax.broadcasted_iota(jnp.int32, (t, t + LANES), 1)
    u = jnp.logical_or(jj > ss, ss >= t).astype(BF16)
    kspec = pl.BlockSpec((1, h, lp, HEAD_DIM), lambda bi, i: (bi, 0, 0, 0))
    qspec = pl.BlockSpec((1, h, t, HEAD_DIM), lambda bi, i: (bi, 0, i, 0))
    return pl.pallas_call(
        functools.partial(_sb_kernel, qoff=qoff),
        grid=(b, tq_all // t),
        in_specs=[qspec, kspec, kspec, pl.BlockSpec((t, t + LANES), lambda bi, i: (0, 0))],
        out_specs=qspec,
        out_shape=jax.ShapeDtypeStruct((b, h, tq_all, HEAD_DIM), BF16),
        compiler_params=_cparams(("arbitrary", "arbitrary")),
        name="sb",
    )(q_hm, k_hm, v_hm, u)


def _band_kernel(q_ref, k_ref, v_ref, bm_ref, o_ref, *, tq, w, n_invalid):
    i = pl.program_id(1)
    if n_invalid:
        col = lax.broadcasted_iota(jnp.int32, (1, w), 1)
        valid = jnp.where(i * tq + col >= n_invalid, 0.0, NEG)
    for h in range(q_ref.shape[1]):
        lg = _nt_dot(q_ref[0, h], k_ref[0, h]) + bm_ref[h]
        if n_invalid:
            lg = lg + valid
        m = jnp.max(lg, axis=1, keepdims=True)
        p = jnp.exp(lg - m)
        l = jnp.sum(p, axis=1, keepdims=True)
        o_ref[0, h] = (_dot(p.astype(BF16), v_ref[0, h]) / l).astype(BF16)


def _band_bias(rel_table, tq):
    w = C_BAND + tq
    t = jnp.arange(tq)[:, None]
    c = jnp.arange(w)[None, :] - C_BAND
    k = jnp.arange(tq + w - 1)
    rel = jnp.clip(w - 1 - C_BAND - k, -REL_CLIP, REL_CLIP) + REL_CLIP
    bias = _toeplitz(rel_table.astype(F32)[rel].T, tq, w)
    qc = t // CHUNK
    kc = jnp.floor_divide(c, CHUNK)
    mask = (kc <= qc) & (kc >= qc - C_BAND_CHUNKS)
    return jnp.where(mask[None], bias, NEG)


def _band(q_hm, k_ext, v_ext, bm, tq, n_invalid):
    b, h, t, _ = q_hm.shape
    w = C_BAND + tq
    kspec = pl.BlockSpec((pl.Element(1), pl.Element(h), pl.Element(w), pl.Element(HEAD_DIM)),
                         lambda bi, i: (bi, 0, i * tq, 0))
    qspec = pl.BlockSpec((1, h, tq, HEAD_DIM), lambda bi, i: (bi, 0, i, 0))
    return pl.pallas_call(
        functools.partial(_band_kernel, tq=tq, w=w, n_invalid=n_invalid),
        grid=(b, t // tq),
        in_specs=[qspec, kspec, kspec, pl.BlockSpec((h, tq, w), lambda bi, i: (0, 0, 0))],
        out_specs=qspec,
        out_shape=jax.ShapeDtypeStruct((b, h, t, HEAD_DIM), BF16),
        compiler_params=_cparams(("arbitrary", "arbitrary")),
        name="band",
    )(q_hm, k_ext, v_ext, bm)


def _merge_kernel(x_ref, h_ref, oa_ref, ob_ref, oc_ref, g1_ref, sc2_ref, sh2_ref, n2_ref,
                  wg_ref, bg_ref, wb_ref, wo_ref, wr_ref, br_ref,
                  xo_ref, h2_ref, gates_ref):
    d = x_ref.shape[-1]
    hb = h_ref[0]
    mix = None
    off = 0
    for j, o_ref in enumerate((oa_ref, ob_ref, oc_ref)):
        width = o_ref.shape[-1]
        y = _dot(o_ref[0], wb_ref[off:off + width, :])
        off += width
        g = jax.nn.sigmoid(_dot(hb, wg_ref[:, j * d:(j + 1) * d]) + bg_ref[:, j * d:(j + 1) * d])
        mix = g * y if mix is None else mix + g * y
    x = x_ref[0] + g1_ref[0] * _dot(mix.astype(BF16), wo_ref[...])
    xo_ref[0] = x
    h2 = _rms_mod(x, n2_ref[...], sc2_ref[0], sh2_ref[0])
    h2_ref[0] = h2.astype(BF16)

    h2_hi = h2.astype(BF16)
    h2_lo = (h2 - h2_hi.astype(F32)).astype(BF16)
    wr_hi, wr_lo = wr_ref[0], wr_ref[1]
    lr = _dot(h2_hi, wr_hi) + (_dot(h2_lo, wr_hi) + _dot(h2_hi, wr_lo)) + br_ref[...]
    tm = lr.shape[0]
    lane = lax.broadcasted_iota(jnp.int32, (tm, LANES), 1)
    lanef = lane.astype(F32)
    ninf = -jnp.inf
    is_expert = lane < N_EXPERTS
    lg = jnp.where(is_expert, ninf, jnp.where(lane < N_EXPERTS + N_GROUPS, lr, ninf))
    eg = jnp.exp(lg - jnp.max(lg, axis=1, keepdims=True))
    pg = eg / jnp.sum(eg, axis=1, keepdims=True)
    pg_top = jnp.max(pg, axis=1, keepdims=True)
    g_top = jnp.min(jnp.where(pg == pg_top, lanef, float(LANES)), axis=1, keepdims=True) - float(N_EXPERTS)
    in_group = (lane // EXP_PER_GROUP).astype(F32) == g_top
    le = jnp.where(is_expert, jnp.where(in_group, lr, ninf), ninf)
    m1 = jnp.max(le, axis=1, keepdims=True)
    i1 = jnp.min(jnp.where(le == m1, lanef, float(LANES)), axis=1, keepdims=True)
    le2 = jnp.where(lanef == i1, ninf, le)
    m2 = jnp.max(le2, axis=1, keepdims=True)
    i2 = jnp.min(jnp.where(le2 == m2, lanef, float(LANES)), axis=1, keepdims=True)
    e2 = jnp.exp(m2 - m1)
    den = 1.0 + e2
    gates = (jnp.where(lanef == i1, pg_top / den, 0.0)
             + jnp.where(lanef == i2, pg_top * e2 / den, 0.0))
    gates_ref[0] = gates[:, :N_EXPERTS]


def _merge(x, h, oa, ob, oc, g1, sc2, sh2, n2, wg, bg, wb, wo, wr, br, tm):
    b, t, d = x.shape

    def tok(width):
        return pl.BlockSpec((1, tm, width), lambda bi, i: (bi, i, 0))

    vec = pl.BlockSpec((1, 1, d), lambda bi, i: (bi, 0, 0))

    def full(a):
        return pl.BlockSpec(a.shape, lambda bi, i: (0,) * a.ndim)

    n2r, bgr = n2.reshape(1, d), bg.reshape(1, 3 * d)
    return pl.pallas_call(
        _merge_kernel,
        grid=(b, t // tm),
        in_specs=[tok(d), tok(d), tok(oa.shape[-1]), tok(ob.shape[-1]), tok(oc.shape[-1]),
                  vec, vec, vec, full(n2r), full(wg), full(bgr), full(wb), full(wo), full(wr), full(br)],
        out_specs=[tok(d), tok(d), tok(N_EXPERTS)],
        out_shape=[jax.ShapeDtypeStruct((b, t, d), F32), jax.ShapeDtypeStruct((b, t, d), BF16),
                   jax.ShapeDtypeStruct((b, t, N_EXPERTS), F32)],
        compiler_params=_cparams(("arbitrary", "arbitrary")),
        name="merge",
    )(x, h, oa, ob, oc, g1.reshape(b, 1, d), sc2.reshape(b, 1, d), sh2.reshape(b, 1, d),
      n2r, wg, bgr, wb, wo, wr, br)


def _moe_kernel(x_ref, h2_ref, gates_ref, g2_ref, w1_ref, w3_ref, w2_ref, o_ref, acc_ref):
    e = pl.program_id(2)

    @pl.when(e == 0)
    def _():
        acc_ref[...] = jnp.zeros_like(acc_ref)

    hb = h2_ref[0]
    a = _dot(hb, w1_ref[0])
    bb = _dot(hb, w3_ref[0])
    u = (a * jax.nn.sigmoid(a)) * bb
    out = _dot(u.astype(BF16), w2_ref[0])
    gates = gates_ref[0]
    lane = lax.broadcasted_iota(jnp.int32, gates.shape, 1)
    ge = jnp.sum(jnp.where(lane == e, gates, 0.0), axis=1, keepdims=True)
    acc_ref[...] += ge * out

    @pl.when(e == pl.num_programs(2) - 1)
    def _():
        o_ref[0] = x_ref[0] + g2_ref[0] * acc_ref[...]


def _moe(x, h2, gates, g2, w1, w3, w2, tm):
    b, t, d = x.shape
    ne, _, f = w1.shape
    tok = lambda width: pl.BlockSpec((1, tm, width), lambda bi, i, e: (bi, i, 0))
    return pl.pallas_call(
        _moe_kernel,
        grid=(b, t // tm, ne),
        in_specs=[tok(d), tok(d), tok(N_EXPERTS),
                  pl.BlockSpec((1, 1, d), lambda bi, i, e: (bi, 0, 0)),
                  pl.BlockSpec((1, d, f), lambda bi, i, e: (e, 0, 0)),
                  pl.BlockSpec((1, d, f), lambda bi, i, e: (e, 0, 0)),
                  pl.BlockSpec((1, f, d), lambda bi, i, e: (e, 0, 0))],
        out_specs=tok(d),
        out_shape=jax.ShapeDtypeStruct((b, t, d), F32),
        scratch_shapes=[pltpu.VMEM((tm, d), F32)],
        compiler_params=_cparams(("arbitrary", "arbitrary", "arbitrary")),
        name="moe",
    )(x, h2, gates, g2.reshape(b, 1, d), w1, w3, w2)


def _final_norm_kernel(x_ref, g_ref, o_ref):
    x = x_ref[0]
    ms = jnp.mean(x * x, axis=-1, keepdims=True)
    o_ref[0] = x * lax.rsqrt(ms + RMS_EPS) * g_ref[...]


def _final_norm(x, g, tm):
    b, t, d = x.shape
    spec = pl.BlockSpec((1, tm, d), lambda bi, i: (bi, i, 0))
    return pl.pallas_call(
        _final_norm_kernel,
        grid=(b, t // tm),
        in_specs=[spec, pl.BlockSpec((1, d), lambda bi, i: (0, 0))],
        out_specs=spec,
        out_shape=jax.ShapeDtypeStruct((b, t, d), F32),
        compiler_params=_cparams(("arbitrary", "arbitrary")),
        name="final_norm",
    )(x, g.reshape(1, d))


def _prep_layer_weights(w_in, w_gate, w_branch, w_out, w_rg, b_rg, w_re, b_re, w1, w3, w2):
    d = w_in.shape[0]
    qa, ka, va, qi, ki, wi, qb, kb, vb, qc, kc, vc = _split_in(w_in)
    pad = jnp.zeros((d, 256 - (64 * 3 + H_IDX)), w_in.dtype)
    w_in_p = jnp.concatenate([qa, qi, ka, va, ki, wi, pad, qb, kb, vb, qc, kc, vc], axis=1).astype(BF16)
    wr = jnp.zeros((d, LANES), F32).at[:, :N_EXPERTS].set(w_re).at[:, N_EXPERTS:N_EXPERTS + N_GROUPS].set(w_rg)
    br = jnp.zeros((1, LANES), F32).at[0, :N_EXPERTS].set(b_re).at[0, N_EXPERTS:N_EXPERTS + N_GROUPS].set(b_rg)
    wr_hi = wr.astype(BF16)
    wr = jnp.stack([wr_hi, (wr - wr_hi.astype(F32)).astype(BF16)])
    return (w_in_p, w_gate.astype(BF16), w_branch.astype(BF16), w_out.astype(BF16), wr, br,
            w1.astype(BF16), w3.astype(BF16), w2.astype(BF16))


def _split_in(w_in):
    sizes = (256, 64, 64, 256, 64, H_IDX, 256, 256, 256, 512, 512, 512)
    out, start = [], 0
    for n in sizes:
        out.append(w_in[:, start:start + n])
        start += n
    return out


def _from_hm(o_hm):
    b, h, t, dh = o_hm.shape
    return jnp.swapaxes(o_hm, 1, 2).reshape(b, t, h * dh)


def _to_hm(a, dtype):
    return jnp.swapaxes(a, 1, 2).astype(dtype)


def _pad_axis(a, axis, size):
    if a.shape[axis] == size:
        return a
    widths = [(0, 0)] * a.ndim
    widths[axis] = (0, size - a.shape[axis])
    return jnp.pad(a, widths)


def _round_up(n, m):
    return -(-n // m) * m


def _layer(x, mods, norms, lw, consts, cache):
    n1, n2 = norms
    sh1, sc1, g1, sh2, sc2, g2 = mods
    w_in_p, wg, bg, wb, wo, wr, br, w1, w3, w2 = lw
    dsa_bias, band_bias = consts
    b, t, d = x.shape
    tm = min(256, t)

    (h, qa_hm, qi_hm, sm, qb_hm, kb_hm, vb_hm, kb, vb, qc_hm, kc_hm, vc_hm, kcs, vcs) = _proj(
        x, sc1, sh1, n1, w_in_p, tm)
    ka, va, ki = sm[..., 0:64], sm[..., 64:128], sm[..., 128:192]
    wi_t = jnp.swapaxes(sm[..., 192:192 + H_IDX], 1, 2) * (H_IDX ** -0.5)

    if cache is None:
        qoff, tq_pad = 0, t
        ka_f, va_f, ki_f = ka, va, ki
        kb_f, vb_f = kb_hm, vb_hm
        kc_f = jnp.pad(kc_hm, ((0, 0), (0, 0), (C_BAND, 0), (0, 0)))
        vc_f = jnp.pad(vc_hm, ((0, 0), (0, 0), (C_BAND, 0), (0, 0)))
        band_tq, n_invalid = BAND_TQ, C_BAND
        state = (ka, va, ki, kb.reshape(b, t, H_B, HEAD_DIM), vb.reshape(b, t, H_B, HEAD_DIM),
                 kcs.reshape(b, -1, H_C, HEAD_DIM), vcs.reshape(b, -1, H_C, HEAD_DIM))
    else:
        ca_k, ca_v, ca_ki, cb_k, cb_v, cc_k, cc_v = cache
        qoff = ca_k.shape[1]
        tq_pad = _round_up(t, DSA_TQ)
        ka_f = jnp.concatenate([ca_k, ka], axis=1)
        va_f = jnp.concatenate([ca_v, va], axis=1)
        ki_f = jnp.concatenate([ca_ki, ki], axis=1)
        kb_f = jnp.concatenate([_to_hm(cb_k, BF16), kb_hm], axis=2)
        vb_f = jnp.concatenate([_to_hm(cb_v, BF16), vb_hm], axis=2)
        kc_f = jnp.concatenate([_to_hm(cc_k, BF16), kc_hm], axis=2)
        vc_f = jnp.concatenate([_to_hm(cc_v, BF16), vc_hm], axis=2)
        band_tq, n_invalid = CHUNK, 0
        kc_new = kcs.reshape(b, t, H_C, HEAD_DIM)
        vc_new = vcs.reshape(b, t, H_C, HEAD_DIM)
        state = (ka, va, ki, kb.reshape(b, t, H_B, HEAD_DIM), vb.reshape(b, t, H_B, HEAD_DIM),
                 jnp.concatenate([cc_k, kc_new], axis=1)[:, t:],
                 jnp.concatenate([cc_v, vc_new], axis=1)[:, t:])

    l_valid = ka_f.shape[1]
    topk = min(TOPK_MAX, l_valid // 4)
    lp = _round_up(l_valid, DSA_LC1)
    oa_t = _dsa(_pad_axis(qi_hm, 2, tq_pad), _pad_axis(qa_hm, 2, tq_pad), _pad_axis(wi_t, 2, tq_pad),
                _pad_axis(ki_f.astype(BF16), 1, lp), _pad_axis(ka_f.astype(BF16), 1, lp),
                _pad_axis(va_f.astype(BF16), 1, lp), dsa_bias, qoff, l_valid, topk)
    oa = jnp.transpose(oa_t[..., :t], (0, 3, 1, 2)).reshape(b, t, H_A * HEAD_DIM)

    tq_sb = _round_up(t, SB_T)
    lp_b = qoff + tq_sb
    ob_hm = _sb(_pad_axis(qb_hm, 2, tq_sb), _pad_axis(kb_f, 2, lp_b), _pad_axis(vb_f, 2, lp_b), qoff)
    ob = _from_hm(ob_hm[:, :, :t])

    oc = _from_hm(_band(qc_hm, kc_f, vc_f, band_bias[band_tq], band_tq, n_invalid))

    x, h2, gates = _merge(x, h, oa, ob, oc, g1, sc2, sh2, n2, wg, bg, wb, wo, wr, br, tm)
    x = _moe(x, h2, gates, g2, w1, w3, w2, min(512, t))
    return x, state


def kernel(x_prompt, x_sample, c_prompt, c_sample, cache_a_k, cache_a_v, cache_a_kidx, cache_b_k, cache_b_v, cache_c_k, cache_c_v, norm1, norm2, final_norm, w_ada, b_ada, w_in, t5_table, rel_c, w_gate, b_gate, w_branch, w_out, w_rg, b_rg, w_re, b_re, w1, w3, w2):
    depth = norm1.shape[0]
    bp = x_prompt.shape[0]
    mods_all = _ada_mods(jnp.concatenate([c_prompt, c_sample], axis=0), w_ada, b_ada)
    dsa_bias = _dsa_bias_tiles(t5_table)
    xp, xs = x_prompt, x_sample
    st_p, st_s = [], []
    for l in range(depth):
        lw = _prep_layer_weights(w_in[l], w_gate[l], w_branch[l], w_out[l], w_rg[l], b_rg[l],
                                 w_re[l], b_re[l], w1[l], w3[l], w2[l])
        lw = lw[:2] + (b_gate[l],) + lw[2:]
        band_bias = {tq: _band_bias(rel_c[l], tq) for tq in (CHUNK, BAND_TQ)}
        consts = (dsa_bias, band_bias)
        norms = (norm1[l], norm2[l])
        mods = jnp.split(mods_all[l], 6, axis=-1)
        xp, sp = _layer(xp, [m[:bp] for m in mods], norms, lw, consts, None)
        cache = (cache_a_k[l], cache_a_v[l], cache_a_kidx[l], cache_b_k[l], cache_b_v[l],
                 cache_c_k[l], cache_c_v[l])
        xs, ss = _layer(xs, [m[bp:] for m in mods], norms, lw, consts, cache)
        st_p.append(sp)
        st_s.append(ss)
    y_prompt = _final_norm(xp, final_norm, min(512, xp.shape[1]))
    y_sample = _final_norm(xs, final_norm, min(512, xs.shape[1]))
    stack = lambda states, i: jnp.stack([s[i] for s in states], axis=0)
    return ((y_prompt, y_sample) + tuple(stack(st_p, i) for i in range(7))
            + tuple(stack(st_s, i) for i in range(7)))
```

```python
import functools
import math

import jax
import jax.numpy as jnp
from jax import lax
from jax.experimental import pallas as pl
from jax.experimental.pallas import tpu as pltpu

F32 = jnp.float32
BF16 = jnp.bfloat16
HIGHEST = lax.Precision.HIGHEST

CHUNK = 64
HEAD_DIM = 64
D_IDX = 64
H_A = 4
H_IDX = 4
H_B = 4
H_C = 8
C_BAND_CHUNKS = 8
C_BAND = C_BAND_CHUNKS * CHUNK
REL_CLIP = 256
N_BUCKETS = 32
T5_MAX_DIST = 1024
TOPK_MAX = 256
N_GROUPS = 4
EXP_PER_GROUP = 4
N_EXPERTS = N_GROUPS * EXP_PER_GROUP
RMS_EPS = 1e-6

NEG = -1e30
MIN_NORMAL_BITS = 0x00800000
ZERO_KEY_TOP = 1 << 14
INT_MIN = -(2 ** 31)
LANES = 128
VMEM_LIMIT = 56 * 1024 * 1024

DSA_TQ = 128
DSA_LC1 = 512
DSA_LC3 = 256
SB_T = 256
SB_DEAD = -104.0
BAND_TQ = 4 * CHUNK
DSA_NEAR = -(-(DSA_LC3 - 1 + T5_MAX_DIST) // DSA_TQ)

_C_QA, _C_QI, _C_SM, _C_QB, _C_KB, _C_VB, _C_QC, _C_KC, _C_VC, _C_END = (
    0, 256, 512, 768, 1024, 1280, 1536, 2048, 2560, 3072)


def _cparams(sem):
    return pltpu.CompilerParams(dimension_semantics=sem, vmem_limit_bytes=VMEM_LIMIT)


def _nt_dot(a, b):
    return lax.dot_general(a, b, (((1,), (1,)), ((), ())), preferred_element_type=F32)


def _dot(a, b):
    return jnp.dot(a, b, preferred_element_type=F32)


def _col_reduce(x, op):
    r, c = x.shape
    if r > 64:
        x = op(x.reshape(r // 64, 64, c), axis=0)
    return op(x, axis=0, keepdims=True)


def _rms_mod(x, g, sc, sh):
    ms = jnp.mean(x * x, axis=-1, keepdims=True)
    return (x * lax.rsqrt(ms + RMS_EPS) * g) * (1.0 + sc) + sh


def _ada_kernel(c_ref, w_ref, b_ref, o_ref):
    c = c_ref[...]
    s = c * jax.nn.sigmoid(c)
    o_ref[0] = jnp.dot(s, w_ref[0], preferred_element_type=F32, precision=HIGHEST) + b_ref[0]


def _ada_mods(c_all, w_ada, b_ada):
    depth, d, e = w_ada.shape
    r = c_all.shape[0]
    tn = 1024
    return pl.pallas_call(
        _ada_kernel,
        grid=(depth, e // tn),
        in_specs=[pl.BlockSpec((r, d), lambda l, j: (0, 0)),
                  pl.BlockSpec((1, d, tn), lambda l, j: (l, 0, j)),
                  pl.BlockSpec((1, 1, tn), lambda l, j: (l, 0, j))],
        out_specs=pl.BlockSpec((1, r, tn), lambda l, j: (l, 0, j)),
        out_shape=jax.ShapeDtypeStruct((depth, r, e), F32),
        compiler_params=_cparams(("arbitrary", "arbitrary")),
        name="ada_mods",
    )(c_all, w_ada, b_ada.reshape(depth, 1, e))


def _proj_kernel(x_ref, sc_ref, sh_ref, g_ref, w_ref,
                 h_ref, qa_ref, qi_ref, sm_ref, qb_ref, kbh_ref, vbh_ref, kb_ref, vb_ref,
                 qc_ref, kch_ref, vch_ref, kcs_ref, vcs_ref, *, n_tiles, n_state_tiles):
    i = pl.program_id(1)
    h = _rms_mod(x_ref[0], g_ref[...], sc_ref[0], sh_ref[0])
    hb = h.astype(BF16)
    h_ref[0] = hb

    def mm(lo, hi):
        return _dot(hb, w_ref[:, lo:hi])

    def heads(ref, y, n, scale=None):
        for hh in range(n):
            part = y[:, hh * HEAD_DIM:(hh + 1) * HEAD_DIM]
            if scale is not None:
                part = part * scale
            ref[0, hh] = part.astype(BF16)

    qscale = HEAD_DIM ** -0.5
    heads(qa_ref, mm(_C_QA, _C_QI), H_A, qscale)
    heads(qi_ref, mm(_C_QI, _C_SM), H_IDX, D_IDX ** -0.5)
    sm_ref[0] = mm(_C_SM, _C_QB)
    heads(qb_ref, mm(_C_QB, _C_KB), H_B, qscale)
    kb = mm(_C_KB, _C_VB)
    kb_ref[0] = kb
    heads(kbh_ref, kb, H_B)
    vb = mm(_C_VB, _C_QC)
    vb_ref[0] = vb
    heads(vbh_ref, vb, H_B)
    heads(qc_ref, mm(_C_QC, _C_KC), H_C, qscale)
    kc = mm(_C_KC, _C_VC)
    heads(kch_ref, kc, H_C)
    vc = mm(_C_VC, _C_END)
    heads(vch_ref, vc, H_C)

    @pl.when(i >= n_tiles - n_state_tiles)
    def _():
        kcs_ref[0] = kc
        vcs_ref[0] = vc


def _proj(x, sc, sh, g, w_in_p, tm):
    b, t, d = x.shape
    n_tiles = t // tm
    n_state_tiles = min(C_BAND, t) // tm
    nbuf = n_state_tiles * tm

    def hm(n):
        return (pl.BlockSpec((1, n, tm, HEAD_DIM), lambda bi, i: (bi, 0, i, 0)),
                jax.ShapeDtypeStruct((b, n, t, HEAD_DIM), BF16))

    def tok(width, dtype):
        return (pl.BlockSpec((1, tm, width), lambda bi, i: (bi, i, 0)),
                jax.ShapeDtypeStruct((b, t, width), dtype))

    def state(width):
        return (pl.BlockSpec((1, tm, width),
                             lambda bi, i: (bi, jnp.maximum(i - (n_tiles - n_state_tiles), 0), 0)),
                jax.ShapeDtypeStruct((b, nbuf, width), F32))

    outs = [tok(d, BF16), hm(H_A), hm(H_IDX), tok(256, F32), hm(H_B), hm(H_B), hm(H_B),
            tok(256, F32), tok(256, F32), hm(H_C), hm(H_C), hm(H_C), state(512), state(512)]
    vec = pl.BlockSpec((1, 1, d), lambda bi, i: (bi, 0, 0))
    return pl.pallas_call(
        functools.partial(_proj_kernel, n_tiles=n_tiles, n_state_tiles=n_state_tiles),
        grid=(b, n_tiles),
        in_specs=[pl.BlockSpec((1, tm, d), lambda bi, i: (bi, i, 0)), vec, vec,
                  pl.BlockSpec((1, d), lambda bi, i: (0, 0)),
                  pl.BlockSpec((d, _C_END), lambda bi, i: (0, 0))],
        out_specs=[o[0] for o in outs],
        out_shape=[o[1] for o in outs],
        compiler_params=_cparams(("arbitrary", "arbitrary")),
        name="proj",
    )(x, sc.reshape(b, 1, d), sh.reshape(b, 1, d), g.reshape(1, d), w_in_p)


def _dsa_kernel(qi_ref, qa_ref, wi_ref, ki_ref, ka_ref, vt_ref, bias_ref, o_ref, key_ref,
                *, qoff, l_valid, topk):
    tq, lc1, lc3 = DSA_TQ, DSA_LC1, DSA_LC3
    i = pl.program_id(1)
    q0 = qoff + i * tq
    lane = lax.broadcasted_iota(jnp.int32, (1, tq), 1)
    qpos = q0 + lane
    lim = jnp.minimum((qpos // CHUNK + 1) * CHUNK, l_valid)
    n_adm = jnp.minimum(((q0 + tq - 1) // CHUNK + 1) * CHUNK, l_valid)
    nch1 = (n_adm + lc1 - 1) // lc1
    qi = qi_ref[0].reshape(H_IDX * tq, D_IDX)
    qa = qa_ref[0].reshape(H_A * tq, HEAD_DIM)
    wi = wi_ref[0]

    def score_chunk(c, carry):
        s0 = pl.multiple_of(c * lc1, lc1)
        s_all = _nt_dot(ki_ref[0, pl.ds(s0, lc1), :], qi)
        sc = None
        for h in range(H_IDX):
            term = wi[h:h + 1, :] * jnp.maximum(s_all[:, h * tq:(h + 1) * tq], 0.0)
            sc = term if sc is None else sc + term
        bits = lax.bitcast_convert_type(sc, jnp.int32)
        key = bits ^ (lax.shift_right_arithmetic(bits, 31) & 0x7FFFFFFF)
        sidx = s0 + lax.broadcasted_iota(jnp.int32, (lc1, tq), 0)
        tiny = (bits & 0x7FFFFFFF) < MIN_NORMAL_BITS
        key = jnp.where(tiny, jnp.where(bits < 0, -1 - sidx, ZERO_KEY_TOP - sidx), key)
        key_ref[pl.ds(s0, lc1), :] = jnp.where(sidx < lim, key, INT_MIN)
        return carry

    lax.fori_loop(0, nch1, score_chunk, 0)

    def count(*preds):
        def body(c, accs):
            s0 = pl.multiple_of(c * lc1, lc1)
            kk = key_ref[pl.ds(s0, lc1), :]
            sidx = s0 + lax.broadcasted_iota(jnp.int32, (lc1, tq), 0)
            return tuple(acc + jnp.sum(pred(kk, sidx).reshape(lc1 // 64, 64, tq), axis=0)
                         for acc, pred in zip(accs, preds))
        accs = lax.fori_loop(0, nch1, body, tuple(jnp.zeros((64, tq), F32) for _ in preds))
        return tuple(jnp.sum(acc, axis=0, keepdims=True) for acc in accs)

    kf = float(topk)

    def bit_step(it, thr):
        cand = thr + lax.shift_left(jnp.int32(1), 31 - it)
        n_ge, = count(lambda kk, sidx: jnp.where(kk >= cand, 1.0, 0.0))
        return jnp.where(n_ge >= kf, cand, thr)

    thr = lax.fori_loop(0, 32, bit_step, jnp.full((1, tq), INT_MIN, jnp.int32))
    thr = jnp.maximum(thr, INT_MIN + 1)

    n_gt, n_eq = count(lambda kk, sidx: jnp.where(kk > thr, 1.0, 0.0),
                       lambda kk, sidx: jnp.where(kk == thr, 1.0, 0.0))
    room = kf - n_gt
    idx_bits = 14
    assert l_valid < (1 << idx_bits) <= ZERO_KEY_TOP

    def tie_search():
        def step(it, end):
            cand = end + lax.shift_left(jnp.int32(1), idx_bits - 1 - it)
            n, = count(lambda kk, sidx: jnp.where(kk == thr, jnp.where(sidx < cand, 1.0, 0.0), 0.0))
            return jnp.where(n <= room, cand, end)
        return lax.fori_loop(0, idx_bits, step, jnp.zeros((1, tq), jnp.int32))

    tie_end = lax.cond(jnp.max(n_eq - room) > 0.0, tie_search,
                       lambda: jnp.full((1, tq), 1 << idx_bits, jnp.int32))

    def attend_chunk(c, carry):
        m, l, acc = carry
        s0 = pl.multiple_of(c * lc3, lc3)
        lg = _nt_dot(ka_ref[0, pl.ds(s0, lc3), :], qa)
        kk = key_ref[pl.ds(s0, lc3), :]
        sidx = s0 + lax.broadcasted_iota(jnp.int32, (lc3, tq), 0)
        madd = jnp.where(kk > thr, 0.0,
                         jnp.where(kk == thr, jnp.where(sidx < tie_end, 0.0, NEG), NEG))
        d = jnp.clip((q0 - s0) // tq, 0, DSA_NEAR)
        lgb = jnp.concatenate(
            [lg[:, h * tq:(h + 1) * tq] + bias_ref[d, h] + madd for h in range(H_A)], axis=1)
        m_new = jnp.maximum(m, _col_reduce(lgb, jnp.max))
        alpha = jnp.exp(m - m_new)
        p = jnp.exp(lgb - m_new)
        l = l * alpha + _col_reduce(p, jnp.sum)
        acc = acc * alpha + _dot(vt_ref[0, c], p.astype(BF16))
        return m_new, l, acc

    def attend_pair(c1, carry):
        for sub in range(lc1 // lc3):
            carry = attend_chunk(c1 * (lc1 // lc3) + sub, carry)
        return carry

    m0 = jnp.full((1, H_A * tq), NEG, F32)
    l0 = jnp.zeros((1, H_A * tq), F32)
    a0 = jnp.zeros((HEAD_DIM, H_A * tq), F32)
    _, l, acc = lax.fori_loop(0, nch1, attend_pair, (m0, l0, a0))
    out = acc / l
    for h in range(H_A):
        o_ref[0, h] = out[:, h * tq:(h + 1) * tq].astype(BF16)


def _t5_bucket(rel):
    nb = N_BUCKETS // 2
    max_exact = nb // 2
    ret = jnp.where(rel > 0, nb, 0).astype(jnp.int32)
    n = jnp.abs(rel)
    n_f = jnp.maximum(n, 1).astype(F32)
    large = max_exact + (jnp.log(n_f / max_exact) / math.log(T5_MAX_DIST / max_exact)
                         * (nb - max_exact)).astype(jnp.int32)
    large = jnp.minimum(large, nb - 1)
    return ret + jnp.where(n < max_exact, n, large).astype(jnp.int32)


def _toeplitz(vec, n, m):
    length = n + m - 1
    lead = vec.shape[:-1]
    flat = jnp.tile(vec, (1,) * len(lead) + (n + 1,))[..., :n * (length + 1)]
    hankel = flat.reshape(lead + (n, length + 1))[..., :m]
    return hankel[..., ::-1]


def _dsa_bias_tiles(t5_table):
    d = jnp.arange(DSA_NEAR + 1)[:, None]
    k = jnp.arange(DSA_LC3 + DSA_TQ - 1)[None, :]
    rel = k - (DSA_TQ - 1) - DSA_TQ * d
    rel = jnp.where(d == DSA_NEAR, -T5_MAX_DIST, rel)
    vec = jnp.moveaxis(t5_table.astype(F32)[_t5_bucket(rel)], -1, 1)
    return _toeplitz(vec, DSA_LC3, DSA_TQ)


def _dsa(qi_hm, qa_hm, wi_t, ki, ka, va, bias_tiles, qoff, l_valid, topk):
    b, _, tq_all, _ = qi_hm.shape
    lp = ki.shape[1]
    nq = tq_all // DSA_TQ
    nc3 = lp // DSA_LC3
    vt = jnp.swapaxes(va.reshape(b, nc3, DSA_LC3, HEAD_DIM), 2, 3)
    qspec = pl.BlockSpec((1, H_A, DSA_TQ, HEAD_DIM), lambda bi, i: (bi, 0, i, 0))
    kspec = pl.BlockSpec((1, lp, HEAD_DIM), lambda bi, i: (bi, 0, 0))
    return pl.pallas_call(
        functools.partial(_dsa_kernel, qoff=qoff, l_valid=l_valid, topk=topk),
        grid=(b, nq),
        in_specs=[qspec, qspec,
                  pl.BlockSpec((1, H_IDX, DSA_TQ), lambda bi, i: (bi, 0, i)),
                  kspec, kspec,
                  pl.BlockSpec((1, nc3, HEAD_DIM, DSA_LC3), lambda bi, i: (bi, 0, 0, 0)),
                  pl.BlockSpec(bias_tiles.shape, lambda bi, i: (0, 0, 0, 0))],
        out_specs=pl.BlockSpec((1, H_A, HEAD_DIM, DSA_TQ), lambda bi, i: (bi, 0, 0, i)),
        out_shape=jax.ShapeDtypeStruct((b, H_A, HEAD_DIM, tq_all), BF16),
        scratch_shapes=[pltpu.VMEM((lp, DSA_TQ), jnp.int32)],
        compiler_params=_cparams(("arbitrary", "arbitrary")),
        name="dsa",
    )(qi_hm, qa_hm, wi_t, ki, ka, vt, bias_tiles)


def _sb_kernel(q_ref, k_ref, v_ref, u_ref, o_ref, *, qoff):
    t = SB_T
    n_heads = q_ref.shape[1]
    i = pl.program_id(1)
    u = u_ref[...]
    kb_diag = qoff // t + i
    row = lax.broadcasted_iota(jnp.int32, (t, t), 0)
    col = lax.broadcasted_iota(jnp.int32, (t, t), 1)
    causal = col < row

    def block(kb, carries, accs, diag):
        s0 = pl.multiple_of(kb * t, t)
        new_c, new_a = [], []
        for h in range(n_heads):
            k = k_ref[0, h, pl.ds(s0, t), :]
            v = v_ref[0, h, pl.ds(s0, t), :]
            z = _nt_dot(q_ref[0, h], k)
            sp = jnp.maximum(z, 0.0) + jnp.log1p(jnp.exp(-jnp.abs(z)))
            lm = -sp
            if diag:
                lm = jnp.where(causal, lm, 0.0)
            hi = lm.astype(BF16)
            lo = (lm - hi.astype(F32)).astype(BF16)
            ext = _dot(hi, u) + _dot(lo, u)
            e = z - sp + ext[:, :t]
            a = jnp.exp(jnp.concatenate(
                [e[:, j * LANES:(j + 1) * LANES] + carries[h] for j in range(t // LANES)], axis=1))
            if diag:
                a = jnp.where(causal, a, 0.0)
            new_a.append(accs[h] + _dot(a.astype(BF16), v))
            new_c.append(carries[h] + ext[:, t:])
        return tuple(new_c), tuple(new_a)

    zeros_c = tuple(jnp.zeros((t, LANES), F32) for _ in range(n_heads))
    zeros_a = tuple(jnp.zeros((t, HEAD_DIM), F32) for _ in range(n_heads))
    carries, accs = block(kb_diag, zeros_c, zeros_a, True)

    def worst(cs):
        m = cs[0]
        for c in cs[1:]:
            m = jnp.maximum(m, c)
        return jnp.max(m)

    def cond(st):
        return jnp.logical_and(st[0] < kb_diag, st[1] > SB_DEAD)

    def body(st):
        j, _, cs, acs = st
        cs, acs = block(kb_diag - 1 - j, cs, acs, False)
        return j + 1, worst(cs), cs, acs

    _, _, _, accs = lax.while_loop(cond, body, (jnp.int32(0), worst(carries), carries, accs))
    for h in range(n_heads):
        o_ref[0, h] = accs[h].astype(BF16)


def _sb(q_hm, k_hm, v_hm, qoff):
    b, h, tq_all, _ = q_hm.shape
    lp = k_hm.shape[2]
    t = SB_T
    assert qoff % t == 0 and tq_all % t == 0 and lp >= qoff + tq_all
    jj = lax.broadcasted_iota(jnp.int32, (t, t + LANES), 0)
    ss = lax.broadcasted_iota(jnp.int32, (t, t + LANES), 1)
    u = jnp.logical_or(jj > ss, ss >= t).astype(BF16)
    kspec = pl.BlockSpec((1, h, lp, HEAD_DIM), lambda bi, i: (bi, 0, 0, 0))
    qspec = pl.BlockSpec((1, h, t, HEAD_DIM), lambda bi, i: (bi, 0, i, 0))
    return pl.pallas_call(
        functools.partial(_sb_kernel, qoff=qoff),
        grid=(b, tq_all // t),
        in_specs=[qspec, kspec, kspec, pl.BlockSpec((t, t + LANES), lambda bi, i: (0, 0))],
        out_specs=qspec,
        out_shape=jax.ShapeDtypeStruct((b, h, tq_all, HEAD_DIM), BF16),
        compiler_params=_cparams(("arbitrary", "arbitrary")),
        name="sb",
    )(q_hm, k_hm, v_hm, u)


def _band_kernel(q_ref, k_ref, v_ref, bm_ref, o_ref, *, tq, w, n_invalid):
    i = pl.program_id(1)
    if n_invalid:
        col = lax.broadcasted_iota(jnp.int32, (1, w), 1)
        valid = jnp.where(i * tq + col >= n_invalid, 0.0, NEG)
    for h in range(q_ref.shape[1]):
        lg = _nt_dot(q_ref[0, h], k_ref[0, h]) + bm_ref[h]
        if n_invalid:
            lg = lg + valid
        m = jnp.max(lg, axis=1, keepdims=True)
        p = jnp.exp(lg - m)
        l = jnp.sum(p, axis=1, keepdims=True)
        o_ref[0, h] = (_dot(p.astype(BF16), v_ref[0, h]) / l).astype(BF16)


def _band_bias(rel_table, tq):
    w = C_BAND + tq
    t = jnp.arange(tq)[:, None]
    c = jnp.arange(w)[None, :] - C_BAND
    k = jnp.arange(tq + w - 1)
    rel = jnp.clip(w - 1 - C_BAND - k, -REL_CLIP, REL_CLIP) + REL_CLIP
    bias = _toeplitz(rel_table.astype(F32)[rel].T, tq, w)
    qc = t // CHUNK
    kc = jnp.floor_divide(c, CHUNK)
    mask = (kc <= qc) & (kc >= qc - C_BAND_CHUNKS)
    return jnp.where(mask[None], bias, NEG)


def _band(q_hm, k_ext, v_ext, bm, tq, n_invalid):
    b, h, t, _ = q_hm.shape
    w = C_BAND + tq
    kspec = pl.BlockSpec((pl.Element(1), pl.Element(h), pl.Element(w), pl.Element(HEAD_DIM)),
                         lambda bi, i: (bi, 0, i * tq, 0))
    qspec = pl.BlockSpec((1, h, tq, HEAD_DIM), lambda bi, i: (bi, 0, i, 0))
    return pl.pallas_call(
        functools.partial(_band_kernel, tq=tq, w=w, n_invalid=n_invalid),
        grid=(b, t // tq),
        in_specs=[qspec, kspec, kspec, pl.BlockSpec((h, tq, w), lambda bi, i: (0, 0, 0))],
        out_specs=qspec,
        out_shape=jax.ShapeDtypeStruct((b, h, t, HEAD_DIM), BF16),
        compiler_params=_cparams(("arbitrary", "arbitrary")),
        name="band",
    )(q_hm, k_ext, v_ext, bm)


def _merge_kernel(x_ref, h_ref, oa_ref, ob_ref, oc_ref, g1_ref, sc2_ref, sh2_ref, n2_ref,
                  wg_ref, bg_ref, wb_ref, wo_ref, wr_ref, br_ref,
                  xo_ref, h2_ref, gates_ref):
    d = x_ref.shape[-1]
    hb = h_ref[0]
    mix = None
    off = 0
    for j, o_ref in enumerate((oa_ref, ob_ref, oc_ref)):
        width = o_ref.shape[-1]
        y = _dot(o_ref[0], wb_ref[off:off + width, :])
        off += width
        g = jax.nn.sigmoid(_dot(hb, wg_ref[:, j * d:(j + 1) * d]) + bg_ref[:, j * d:(j + 1) * d])
        mix = g * y if mix is None else mix + g * y
    x = x_ref[0] + g1_ref[0] * _dot(mix.astype(BF16), wo_ref[...])
    xo_ref[0] = x
    h2 = _rms_mod(x, n2_ref[...], sc2_ref[0], sh2_ref[0])
    h2_ref[0] = h2.astype(BF16)

    h2_hi = h2.astype(BF16)
    h2_lo = (h2 - h2_hi.astype(F32)).astype(BF16)
    wr_hi, wr_lo = wr_ref[0], wr_ref[1]
    lr = _dot(h2_hi, wr_hi) + (_dot(h2_lo, wr_hi) + _dot(h2_hi, wr_lo)) + br_ref[...]
    tm = lr.shape[0]
    lane = lax.broadcasted_iota(jnp.int32, (tm, LANES), 1)
    lanef = lane.astype(F32)
    ninf = -jnp.inf
    is_expert = lane < N_EXPERTS
    lg = jnp.where(is_expert, ninf, jnp.where(lane < N_EXPERTS + N_GROUPS, lr, ninf))
    eg = jnp.exp(lg - jnp.max(lg, axis=1, keepdims=True))
    pg = eg / jnp.sum(eg, axis=1, keepdims=True)
    pg_top = jnp.max(pg, axis=1, keepdims=True)
    g_top = jnp.min(jnp.where(pg == pg_top, lanef, float(LANES)), axis=1, keepdims=True) - float(N_EXPERTS)
    in_group = (lane // EXP_PER_GROUP).astype(F32) == g_top
    le = jnp.where(is_expert, jnp.where(in_group, lr, ninf), ninf)
    m1 = jnp.max(le, axis=1, keepdims=True)
    i1 = jnp.min(jnp.where(le == m1, lanef, float(LANES)), axis=1, keepdims=True)
    le2 = jnp.where(lanef == i1, ninf, le)
    m2 = jnp.max(le2, axis=1, keepdims=True)
    i2 = jnp.min(jnp.where(le2 == m2, lanef, float(LANES)), axis=1, keepdims=True)
    e2 = jnp.exp(m2 - m1)
    den = 1.0 + e2
    gates_ref[0] = (jnp.where(lanef == i1, pg_top / den, 0.0)
                    + jnp.where(lanef == i2, pg_top * e2 / den, 0.0))


def _merge(x, h, oa, ob, oc, g1, sc2, sh2, n2, wg, bg, wb, wo, wr, br, tm):
    b, t, d = x.shape

    def tok(width):
        return pl.BlockSpec((1, tm, width), lambda bi, i: (bi, i, 0))

    vec = pl.BlockSpec((1, 1, d), lambda bi, i: (bi, 0, 0))

    def full(a):
        return pl.BlockSpec(a.shape, lambda bi, i: (0,) * a.ndim)

    n2r, bgr = n2.reshape(1, d), bg.reshape(1, 3 * d)
    return pl.pallas_call(
        _merge_kernel,
        grid=(b, t // tm),
        in_specs=[tok(d), tok(d), tok(oa.shape[-1]), tok(ob.shape[-1]), tok(oc.shape[-1]),
                  vec, vec, vec, full(n2r), full(wg), full(bgr), full(wb), full(wo), full(wr), full(br)],
        out_specs=[tok(d), tok(d), tok(LANES)],
        out_shape=[jax.ShapeDtypeStruct((b, t, d), F32), jax.ShapeDtypeStruct((b, t, d), BF16),
                   jax.ShapeDtypeStruct((b, t, LANES), F32)],
        compiler_params=_cparams(("arbitrary", "arbitrary")),
        name="merge",
    )(x, h, oa, ob, oc, g1.reshape(b, 1, d), sc2.reshape(b, 1, d), sh2.reshape(b, 1, d),
      n2r, wg, bgr, wb, wo, wr, br)


def _moe_kernel(x_ref, h2_ref, gates_ref, g2_ref, w1_ref, w3_ref, w2_ref, o_ref, acc_ref):
    e = pl.program_id(2)

    @pl.when(e == 0)
    def _():
        acc_ref[...] = jnp.zeros_like(acc_ref)

    hb = h2_ref[0]
    a = _dot(hb, w1_ref[0])
    bb = _dot(hb, w3_ref[0])
    u = (a * jax.nn.sigmoid(a)) * bb
    out = _dot(u.astype(BF16), w2_ref[0])
    gates = gates_ref[0]
    lane = lax.broadcasted_iota(jnp.int32, gates.shape, 1)
    ge = jnp.sum(jnp.where(lane == e, gates, 0.0), axis=1, keepdims=True)
    acc_ref[...] += ge * out

    @pl.when(e == pl.num_programs(2) - 1)
    def _():
        o_ref[0] = x_ref[0] + g2_ref[0] * acc_ref[...]


def _moe(x, h2, gates, g2, w1, w3, w2, tm):
    b, t, d = x.shape
    ne, _, f = w1.shape
    tok = lambda width: pl.BlockSpec((1, tm, width), lambda bi, i, e: (bi, i, 0))
    return pl.pallas_call(
        _moe_kernel,
        grid=(b, t // tm, ne),
        in_specs=[tok(d), tok(d), tok(LANES),
                  pl.BlockSpec((1, 1, d), lambda bi, i, e: (bi, 0, 0)),
                  pl.BlockSpec((1, d, f), lambda bi, i, e: (e, 0, 0)),
                  pl.BlockSpec((1, d, f), lambda bi, i, e: (e, 0, 0)),
                  pl.BlockSpec((1, f, d), lambda bi, i, e: (e, 0, 0))],
        out_specs=tok(d),
        out_shape=jax.ShapeDtypeStruct((b, t, d), F32),
        scratch_shapes=[pltpu.VMEM((tm, d), F32)],
        compiler_params=_cparams(("arbitrary", "arbitrary", "arbitrary")),
        name="moe",
    )(x, h2, gates, g2.reshape(b, 1, d), w1, w3, w2)


def _final_norm_kernel(x_ref, g_ref, o_ref):
    x = x_ref[0]
    ms = jnp.mean(x * x, axis=-1, keepdims=True)
    o_ref[0] = x * lax.rsqrt(ms + RMS_EPS) * g_ref[...]


def _final_norm(x, g, tm):
    b, t, d = x.shape
    spec = pl.BlockSpec((1, tm, d), lambda bi, i: (bi, i, 0))
    return pl.pallas_call(
        _final_norm_kernel,
        grid=(b, t // tm),
        in_specs=[spec, pl.BlockSpec((1, d), lambda bi, i: (0, 0))],
        out_specs=spec,
        out_shape=jax.ShapeDtypeStruct((b, t, d), F32),
        compiler_params=_cparams(("arbitrary", "arbitrary")),
        name="final_norm",
    )(x, g.reshape(1, d))


def _prep_layer_weights(w_in, w_gate, w_branch, w_out, w_rg, b_rg, w_re, b_re, w1, w3, w2):
    d = w_in.shape[0]
    qa, ka, va, qi, ki, wi, qb, kb, vb, qc, kc, vc = _split_in(w_in)
    pad = jnp.zeros((d, 256 - (64 * 3 + H_IDX)), w_in.dtype)
    w_in_p = jnp.concatenate([qa, qi, ka, va, ki, wi, pad, qb, kb, vb, qc, kc, vc], axis=1).astype(BF16)
    wr = jnp.zeros((d, LANES), F32).at[:, :N_EXPERTS].set(w_re).at[:, N_EXPERTS:N_EXPERTS + N_GROUPS].set(w_rg)
    br = jnp.zeros((1, LANES), F32).at[0, :N_EXPERTS].set(b_re).at[0, N_EXPERTS:N_EXPERTS + N_GROUPS].set(b_rg)
    wr_hi = wr.astype(BF16)
    wr = jnp.stack([wr_hi, (wr - wr_hi.astype(F32)).astype(BF16)])
    return (w_in_p, w_gate.astype(BF16), w_branch.astype(BF16), w_out.astype(BF16), wr, br,
            w1.astype(BF16), w3.astype(BF16), w2.astype(BF16))


def _split_in(w_in):
    sizes = (256, 64, 64, 256, 64, H_IDX, 256, 256, 256, 512, 512, 512)
    out, start = [], 0
    for n in sizes:
        out.append(w_in[:, start:start + n])
        start += n
    return out


def _from_hm(o_hm):
    b, h, t, dh = o_hm.shape
    return jnp.swapaxes(o_hm, 1, 2).reshape(b, t, h * dh)


def _to_hm(a, dtype):
    return jnp.swapaxes(a, 1, 2).astype(dtype)


def _pad_axis(a, axis, size):
    if a.shape[axis] == size:
        return a
    widths = [(0, 0)] * a.ndim
    widths[axis] = (0, size - a.shape[axis])
    return jnp.pad(a, widths)


def _round_up(n, m):
    return -(-n // m) * m


def _layer(x, mods, norms, lw, consts, cache):
    n1, n2 = norms
    sh1, sc1, g1, sh2, sc2, g2 = mods
    w_in_p, wg, bg, wb, wo, wr, br, w1, w3, w2 = lw
    dsa_bias, band_bias = consts
    b, t, d = x.shape
    tm = min(256, t)

    (h, qa_hm, qi_hm, sm, qb_hm, kb_hm, vb_hm, kb, vb, qc_hm, kc_hm, vc_hm, kcs, vcs) = _proj(
        x, sc1, sh1, n1, w_in_p, tm)
    ka, va, ki = sm[..., 0:64], sm[..., 64:128], sm[..., 128:192]
    wi_t = jnp.swapaxes(sm[..., 192:192 + H_IDX], 1, 2) * (H_IDX ** -0.5)

    if cache is None:
        qoff, tq_pad = 0, t
        ka_f, va_f, ki_f = ka, va, ki
        kb_f, vb_f = kb_hm, vb_hm
        kc_f = jnp.pad(kc_hm, ((0, 0), (0, 0), (C_BAND, 0), (0, 0)))
        vc_f = jnp.pad(vc_hm, ((0, 0), (0, 0), (C_BAND, 0), (0, 0)))
        band_tq, n_invalid = BAND_TQ, C_BAND
        state = (ka, va, ki, kb.reshape(b, t, H_B, HEAD_DIM), vb.reshape(b, t, H_B, HEAD_DIM),
                 kcs.reshape(b, -1, H_C, HEAD_DIM), vcs.reshape(b, -1, H_C, HEAD_DIM))
    else:
        ca_k, ca_v, ca_ki, cb_k, cb_v, cc_k, cc_v = cache
        qoff = ca_k.shape[1]
        tq_pad = _round_up(t, DSA_TQ)
        ka_f = jnp.concatenate([ca_k, ka], axis=1)
        va_f = jnp.concatenate([ca_v, va], axis=1)
        ki_f = jnp.concatenate([ca_ki, ki], axis=1)
        kb_f = jnp.concatenate([_to_hm(cb_k, BF16), kb_hm], axis=2)
        vb_f = jnp.concatenate([_to_hm(cb_v, BF16), vb_hm], axis=2)
        kc_f = jnp.concatenate([_to_hm(cc_k, BF16), kc_hm], axis=2)
        vc_f = jnp.concatenate([_to_hm(cc_v, BF16), vc_hm], axis=2)
        band_tq, n_invalid = CHUNK, 0
        kc_new = kcs.reshape(b, t, H_C, HEAD_DIM)
        vc_new = vcs.reshape(b, t, H_C, HEAD_DIM)
        state = (ka, va, ki, kb.reshape(b, t, H_B, HEAD_DIM), vb.reshape(b, t, H_B, HEAD_DIM),
                 jnp.concatenate([cc_k, kc_new], axis=1)[:, t:],
                 jnp.concatenate([cc_v, vc_new], axis=1)[:, t:])

    l_valid = ka_f.shape[1]
    topk = min(TOPK_MAX, l_valid // 4)
    lp = _round_up(l_valid, DSA_LC1)
    oa_t = _dsa(_pad_axis(qi_hm, 2, tq_pad), _pad_axis(qa_hm, 2, tq_pad), _pad_axis(wi_t, 2, tq_pad),
                _pad_axis(ki_f.astype(BF16), 1, lp), _pad_axis(ka_f.astype(BF16), 1, lp),
                _pad_axis(va_f.astype(BF16), 1, lp), dsa_bias, qoff, l_valid, topk)
    oa = jnp.transpose(oa_t[..., :t], (0, 3, 1, 2)).reshape(b, t, H_A * HEAD_DIM)

    tq_sb = _round_up(t, SB_T)
    lp_b = qoff + tq_sb
    ob_hm = _sb(_pad_axis(qb_hm, 2, tq_sb), _pad_axis(kb_f, 2, lp_b), _pad_axis(vb_f, 2, lp_b), qoff)
    ob = _from_hm(ob_hm[:, :, :t])

    oc = _from_hm(_band(qc_hm, kc_f, vc_f, band_bias[band_tq], band_tq, n_invalid))

    x, h2, gates = _merge(x, h, oa, ob, oc, g1, sc2, sh2, n2, wg, bg, wb, wo, wr, br, tm)
    x = _moe(x, h2, gates, g2, w1, w3, w2, min(512, t))
    return x, state


def kernel(x_prompt, x_sample, c_prompt, c_sample, cache_a_k, cache_a_v, cache_a_kidx, cache_b_k, cache_b_v, cache_c_k, cache_c_v, norm1, norm2, final_norm, w_ada, b_ada, w_in, t5_table, rel_c, w_gate, b_gate, w_branch, w_out, w_rg, b_rg, w_re, b_re, w1, w3, w2):
    depth = norm1.shape[0]
    bp = x_prompt.shape[0]
    mods_all = _ada_mods(jnp.concatenate([c_prompt, c_sample], axis=0), w_ada, b_ada)
    dsa_bias = _dsa_bias_tiles(t5_table)
    xp, xs = x_prompt, x_sample
    st_p, st_s = [], []
    for l in range(depth):
        lw = _prep_layer_weights(w_in[l], w_gate[l], w_branch[l], w_out[l], w_rg[l], b_rg[l],
                                 w_re[l], b_re[l], w1[l], w3[l], w2[l])
        lw = lw[:2] + (b_gate[l],) + lw[2:]
        band_bias = {tq: _band_bias(rel_c[l], tq) for tq in (CHUNK, BAND_TQ)}
        consts = (dsa_bias, band_bias)
        norms = (norm1[l], norm2[l])
        mods = jnp.split(mods_all[l], 6, axis=-1)
        xp, sp = _layer(xp, [m[:bp] for m in mods], norms, lw, consts, None)
        cache = (cache_a_k[l], cache_a_v[l], cache_a_kidx[l], cache_b_k[l], cache_b_v[l],
                 cache_c_k[l], cache_c_v[l])
        xs, ss = _layer(xs, [m[bp:] for m in mods], norms, lw, consts, cache)
        st_p.append(sp)
        st_s.append(ss)
    y_prompt = _final_norm(xp, final_norm, min(512, xp.shape[1]))
    y_sample = _final_norm(xs, final_norm, min(512, xs.shape[1]))
    stack = lambda states, i: jnp.stack([s[i] for s in states], axis=0)
    return ((y_prompt, y_sample) + tuple(stack(st_p, i) for i in range(7))
            + tuple(stack(st_s, i) for i in range(7)))
```

```python
import functools
import math

import jax
import jax.numpy as jnp
from jax import lax
from jax.experimental import pallas as pl
from jax.experimental.pallas import tpu as pltpu

F32 = jnp.float32
BF16 = jnp.bfloat16
HIGHEST = lax.Precision.HIGHEST

CHUNK = 64
HEAD_DIM = 64
D_IDX = 64
H_A = 4
H_IDX = 4
H_B = 4
H_C = 8
C_BAND_CHUNKS = 8
C_BAND = C_BAND_CHUNKS * CHUNK
REL_CLIP = 256
N_BUCKETS = 32
T5_MAX_DIST = 1024
TOPK_MAX = 256
N_GROUPS = 4
EXP_PER_GROUP = 4
N_EXPERTS = N_GROUPS * EXP_PER_GROUP
RMS_EPS = 1e-6

NEG = -1e30
MIN_NORMAL_BITS = 0x00800000
ZERO_KEY_TOP = 1 << 14
INT_MIN = -(2 ** 31)
LANES = 128
VMEM_LIMIT = 56 * 1024 * 1024

DSA_TQ = 128
DSA_LC1 = 512
DSA_LC3 = 256
SB_T = 256
SB_DEAD = -104.0
BAND_TQ = 4 * CHUNK
MOE_EXPERTS_PER_STEP = 4
DSA_NEAR = -(-(DSA_LC3 - 1 + T5_MAX_DIST) // DSA_TQ)

_C_QA, _C_QI, _C_SM, _C_QB, _C_KB, _C_VB, _C_QC, _C_KC, _C_VC, _C_END = (
    0, 256, 512, 768, 1024, 1280, 1536, 2048, 2560, 3072)


def _cparams(sem):
    return pltpu.CompilerParams(dimension_semantics=sem, vmem_limit_bytes=VMEM_LIMIT)


def _nt_dot(a, b):
    return lax.dot_general(a, b, (((1,), (1,)), ((), ())), preferred_element_type=F32)


def _dot(a, b):
    return jnp.dot(a, b, preferred_element_type=F32)


def _col_reduce(x, op):
    r, c = x.shape
    if r > 64:
        x = op(x.reshape(r // 64, 64, c), axis=0)
    return op(x, axis=0, keepdims=True)


def _rms_mod(x, g, sc, sh):
    ms = jnp.mean(x * x, axis=-1, keepdims=True)
    return (x * lax.rsqrt(ms + RMS_EPS) * g) * (1.0 + sc) + sh


def _ada_kernel(c_ref, w_ref, b_ref, o_ref):
    c = c_ref[...]
    s = c * jax.nn.sigmoid(c)
    o_ref[0] = jnp.dot(s, w_ref[0], preferred_element_type=F32, precision=HIGHEST) + b_ref[0]


def _ada_mods(c_all, w_ada, b_ada):
    depth, d, e = w_ada.shape
    r = c_all.shape[0]
    tn = 1024
    return pl.pallas_call(
        _ada_kernel,
        grid=(depth, e // tn),
        in_specs=[pl.BlockSpec((r, d), lambda l, j: (0, 0)),
                  pl.BlockSpec((1, d, tn), lambda l, j: (l, 0, j)),
                  pl.BlockSpec((1, 1, tn), lambda l, j: (l, 0, j))],
        out_specs=pl.BlockSpec((1, r, tn), lambda l, j: (l, 0, j)),
        out_shape=jax.ShapeDtypeStruct((depth, r, e), F32),
        compiler_params=_cparams(("arbitrary", "arbitrary")),
        name="ada_mods",
    )(c_all, w_ada, b_ada.reshape(depth, 1, e))


def _proj_kernel(x_ref, sc_ref, sh_ref, g_ref, w_ref,
                 h_ref, qa_ref, qi_ref, sm_ref, qb_ref, kbh_ref, vbh_ref, kb_ref, vb_ref,
                 qc_ref, kch_ref, vch_ref, kcs_ref, vcs_ref, *, n_tiles, n_state_tiles):
    i = pl.program_id(1)
    h = _rms_mod(x_ref[0], g_ref[...], sc_ref[0], sh_ref[0])
    hb = h.astype(BF16)
    h_ref[0] = hb

    def mm(lo, hi):
        return _dot(hb, w_ref[:, lo:hi])

    def heads(ref, y, n, scale=None):
        for hh in range(n):
            part = y[:, hh * HEAD_DIM:(hh + 1) * HEAD_DIM]
            if scale is not None:
                part = part * scale
            ref[0, hh] = part.astype(BF16)

    qscale = HEAD_DIM ** -0.5
    heads(qa_ref, mm(_C_QA, _C_QI), H_A, qscale)
    heads(qi_ref, mm(_C_QI, _C_SM), H_IDX, D_IDX ** -0.5)
    sm_ref[0] = mm(_C_SM, _C_QB)
    heads(qb_ref, mm(_C_QB, _C_KB), H_B, qscale)
    kb = mm(_C_KB, _C_VB)
    kb_ref[0] = kb
    heads(kbh_ref, kb, H_B)
    vb = mm(_C_VB, _C_QC)
    vb_ref[0] = vb
    heads(vbh_ref, vb, H_B)
    heads(qc_ref, mm(_C_QC, _C_KC), H_C, qscale)
    kc = mm(_C_KC, _C_VC)
    heads(kch_ref, kc, H_C)
    vc = mm(_C_VC, _C_END)
    heads(vch_ref, vc, H_C)

    @pl.when(i >= n_tiles - n_state_tiles)
    def _():
        kcs_ref[0] = kc
        vcs_ref[0] = vc


def _proj(x, sc, sh, g, w_in_p, tm):
    b, t, d = x.shape
    n_tiles = t // tm
    n_state_tiles = min(C_BAND, t) // tm
    nbuf = n_state_tiles * tm

    def hm(n):
        return (pl.BlockSpec((1, n, tm, HEAD_DIM), lambda bi, i: (bi, 0, i, 0)),
                jax.ShapeDtypeStruct((b, n, t, HEAD_DIM), BF16))

    def tok(width, dtype):
        return (pl.BlockSpec((1, tm, width), lambda bi, i: (bi, i, 0)),
                jax.ShapeDtypeStruct((b, t, width), dtype))

    def state(width):
        return (pl.BlockSpec((1, tm, width),
                             lambda bi, i: (bi, jnp.maximum(i - (n_tiles - n_state_tiles), 0), 0)),
                jax.ShapeDtypeStruct((b, nbuf, width), F32))

    outs = [tok(d, BF16), hm(H_A), hm(H_IDX), tok(256, F32), hm(H_B), hm(H_B), hm(H_B),
            tok(256, F32), tok(256, F32), hm(H_C), hm(H_C), hm(H_C), state(512), state(512)]
    vec = pl.BlockSpec((1, 1, d), lambda bi, i: (bi, 0, 0))
    return pl.pallas_call(
        functools.partial(_proj_kernel, n_tiles=n_tiles, n_state_tiles=n_state_tiles),
        grid=(b, n_tiles),
        in_specs=[pl.BlockSpec((1, tm, d), lambda bi, i: (bi, i, 0)), vec, vec,
                  pl.BlockSpec((1, d), lambda bi, i: (0, 0)),
                  pl.BlockSpec((d, _C_END), lambda bi, i: (0, 0))],
        out_specs=[o[0] for o in outs],
        out_shape=[o[1] for o in outs],
        compiler_params=_cparams(("arbitrary", "arbitrary")),
        name="proj",
    )(x, sc.reshape(b, 1, d), sh.reshape(b, 1, d), g.reshape(1, d), w_in_p)


def _dsa_kernel(qi_ref, qa_ref, wi_ref, ki_ref, ka_ref, vt_ref, bias_ref, o_ref, key_ref,
                *, qoff, l_valid, topk):
    tq, lc1, lc3 = DSA_TQ, DSA_LC1, DSA_LC3
    i = pl.program_id(1)
    q0 = qoff + i * tq
    lane = lax.broadcasted_iota(jnp.int32, (1, tq), 1)
    qpos = q0 + lane
    lim = jnp.minimum((qpos // CHUNK + 1) * CHUNK, l_valid)
    n_adm = jnp.minimum(((q0 + tq - 1) // CHUNK + 1) * CHUNK, l_valid)
    nch1 = (n_adm + lc1 - 1) // lc1
    qi = qi_ref[0].reshape(H_IDX * tq, D_IDX)
    qa = qa_ref[0].reshape(H_A * tq, HEAD_DIM)
    wi = wi_ref[0]

    def score_chunk(c, carry):
        s0 = pl.multiple_of(c * lc1, lc1)
        s_all = _nt_dot(ki_ref[0, pl.ds(s0, lc1), :], qi)
        sc = None
        for h in range(H_IDX):
            term = wi[h:h + 1, :] * jnp.maximum(s_all[:, h * tq:(h + 1) * tq], 0.0)
            sc = term if sc is None else sc + term
        bits = lax.bitcast_convert_type(sc, jnp.int32)
        key = bits ^ (lax.shift_right_arithmetic(bits, 31) & 0x7FFFFFFF)
        sidx = s0 + lax.broadcasted_iota(jnp.int32, (lc1, tq), 0)
        tiny = (bits & 0x7FFFFFFF) < MIN_NORMAL_BITS
        key = jnp.where(tiny, jnp.where(bits < 0, -1 - sidx, ZERO_KEY_TOP - sidx), key)
        key_ref[pl.ds(s0, lc1), :] = jnp.where(sidx < lim, key, INT_MIN)
        return carry

    lax.fori_loop(0, nch1, score_chunk, 0)

    def count(*preds):
        def body(c, accs):
            s0 = pl.multiple_of(c * lc1, lc1)
            kk = key_ref[pl.ds(s0, lc1), :]
            sidx = s0 + lax.broadcasted_iota(jnp.int32, (lc1, tq), 0)
            return tuple(acc + jnp.sum(pred(kk, sidx).reshape(lc1 // 64, 64, tq), axis=0)
                         for acc, pred in zip(accs, preds))
        accs = lax.fori_loop(0, nch1, body, tuple(jnp.zeros((64, tq), F32) for _ in preds))
        return tuple(jnp.sum(acc, axis=0, keepdims=True) for acc in accs)

    kf = float(topk)

    def bit_step(it, thr):
        cand = thr + lax.shift_left(jnp.int32(1), 31 - it)
        n_ge, = count(lambda kk, sidx: jnp.where(kk >= cand, 1.0, 0.0))
        return jnp.where(n_ge >= kf, cand, thr)

    thr = lax.fori_loop(0, 32, bit_step, jnp.full((1, tq), INT_MIN, jnp.int32))
    thr = jnp.maximum(thr, INT_MIN + 1)

    n_gt, n_eq = count(lambda kk, sidx: jnp.where(kk > thr, 1.0, 0.0),
                       lambda kk, sidx: jnp.where(kk == thr, 1.0, 0.0))
    room = kf - n_gt
    idx_bits = 14
    assert l_valid < (1 << idx_bits) <= ZERO_KEY_TOP

    def tie_search():
        def step(it, end):
            cand = end + lax.shift_left(jnp.int32(1), idx_bits - 1 - it)
            n, = count(lambda kk, sidx: jnp.where(kk == thr, jnp.where(sidx < cand, 1.0, 0.0), 0.0))
            return jnp.where(n <= room, cand, end)
        return lax.fori_loop(0, idx_bits, step, jnp.zeros((1, tq), jnp.int32))

    tie_end = lax.cond(jnp.max(n_eq - room) > 0.0, tie_search,
                       lambda: jnp.full((1, tq), 1 << idx_bits, jnp.int32))

    def attend_chunk(c, carry):
        m, l, acc = carry
        s0 = pl.multiple_of(c * lc3, lc3)
        lg = _nt_dot(ka_ref[0, pl.ds(s0, lc3), :], qa)
        kk = key_ref[pl.ds(s0, lc3), :]
        sidx = s0 + lax.broadcasted_iota(jnp.int32, (lc3, tq), 0)
        madd = jnp.where(kk > thr, 0.0,
                         jnp.where(kk == thr, jnp.where(sidx < tie_end, 0.0, NEG), NEG))
        d = jnp.clip((q0 - s0) // tq, 0, DSA_NEAR)
        lgb = jnp.concatenate(
            [lg[:, h * tq:(h + 1) * tq] + bias_ref[d, h] + madd for h in range(H_A)], axis=1)
        m_new = jnp.maximum(m, _col_reduce(lgb, jnp.max))
        alpha = jnp.exp(m - m_new)
        p = jnp.exp(lgb - m_new)
        l = l * alpha + _col_reduce(p, jnp.sum)
        acc = acc * alpha + _dot(vt_ref[0, c], p.astype(BF16))
        return m_new, l, acc

    def attend_pair(c1, carry):
        for sub in range(lc1 // lc3):
            carry = attend_chunk(c1 * (lc1 // lc3) + sub, carry)
        return carry

    m0 = jnp.full((1, H_A * tq), NEG, F32)
    l0 = jnp.zeros((1, H_A * tq), F32)
    a0 = jnp.zeros((HEAD_DIM, H_A * tq), F32)
    _, l, acc = lax.fori_loop(0, nch1, attend_pair, (m0, l0, a0))
    out = acc / l
    for h in range(H_A):
        o_ref[0, h] = out[:, h * tq:(h + 1) * tq].astype(BF16)


def _t5_bucket(rel):
    nb = N_BUCKETS // 2
    max_exact = nb // 2
    ret = jnp.where(rel > 0, nb, 0).astype(jnp.int32)
    n = jnp.abs(rel)
    n_f = jnp.maximum(n, 1).astype(F32)
    large = max_exact + (jnp.log(n_f / max_exact) / math.log(T5_MAX_DIST / max_exact)
                         * (nb - max_exact)).astype(jnp.int32)
    large = jnp.minimum(large, nb - 1)
    return ret + jnp.where(n < max_exact, n, large).astype(jnp.int32)


def _toeplitz(vec, n, m):
    length = n + m - 1
    lead = vec.shape[:-1]
    flat = jnp.tile(vec, (1,) * len(lead) + (n + 1,))[..., :n * (length + 1)]
    hankel = flat.reshape(lead + (n, length + 1))[..., :m]
    return hankel[..., ::-1]


def _dsa_bias_tiles(t5_table):
    d = jnp.arange(DSA_NEAR + 1)[:, None]
    k = jnp.arange(DSA_LC3 + DSA_TQ - 1)[None, :]
    rel = k - (DSA_TQ - 1) - DSA_TQ * d
    rel = jnp.where(d == DSA_NEAR, -T5_MAX_DIST, rel)
    vec = jnp.moveaxis(t5_table.astype(F32)[_t5_bucket(rel)], -1, 1)
    return _toeplitz(vec, DSA_LC3, DSA_TQ)


def _dsa(qi_hm, qa_hm, wi_t, ki, ka, va, bias_tiles, qoff, l_valid, topk):
    b, _, tq_all, _ = qi_hm.shape
    lp = ki.shape[1]
    nq = tq_all // DSA_TQ
    nc3 = lp // DSA_LC3
    vt = jnp.swapaxes(va.reshape(b, nc3, DSA_LC3, HEAD_DIM), 2, 3)
    qspec = pl.BlockSpec((1, H_A, DSA_TQ, HEAD_DIM), lambda bi, i: (bi, 0, i, 0))
    kspec = pl.BlockSpec((1, lp, HEAD_DIM), lambda bi, i: (bi, 0, 0))
    return pl.pallas_call(
        functools.partial(_dsa_kernel, qoff=qoff, l_valid=l_valid, topk=topk),
        grid=(b, nq),
        in_specs=[qspec, qspec,
                  pl.BlockSpec((1, H_IDX, DSA_TQ), lambda bi, i: (bi, 0, i)),
                  kspec, kspec,
                  pl.BlockSpec((1, nc3, HEAD_DIM, DSA_LC3), lambda bi, i: (bi, 0, 0, 0)),
                  pl.BlockSpec(bias_tiles.shape, lambda bi, i: (0, 0, 0, 0))],
        out_specs=pl.BlockSpec((1, H_A, HEAD_DIM, DSA_TQ), lambda bi, i: (bi, 0, 0, i)),
        out_shape=jax.ShapeDtypeStruct((b, H_A, HEAD_DIM, tq_all), BF16),
        scratch_shapes=[pltpu.VMEM((lp, DSA_TQ), jnp.int32)],
        compiler_params=_cparams(("arbitrary", "arbitrary")),
        name="dsa",
    )(qi_hm, qa_hm, wi_t, ki, ka, vt, bias_tiles)


def _sb_kernel(q_ref, k_ref, v_ref, u_ref, o_ref, *, qoff):
    t = SB_T
    n_heads = q_ref.shape[1]
    i = pl.program_id(1)
    u = u_ref[...]
    kb_diag = qoff // t + i
    row = lax.broadcasted_iota(jnp.int32, (t, t), 0)
    col = lax.broadcasted_iota(jnp.int32, (t, t), 1)
    causal = col < row

    def block(kb, carries, accs, diag):
        s0 = pl.multiple_of(kb * t, t)
        new_c, new_a = [], []
        for h in range(n_heads):
            k = k_ref[0, h, pl.ds(s0, t), :]
            v = v_ref[0, h, pl.ds(s0, t), :]
            z = _nt_dot(q_ref[0, h], k)
            sp = jnp.maximum(z, 0.0) + jnp.log1p(jnp.exp(-jnp.abs(z)))
            lm = -sp
            if diag:
                lm = jnp.where(causal, lm, 0.0)
            hi = lm.astype(BF16)
            lo = (lm - hi.astype(F32)).astype(BF16)
            ext = _dot(hi, u) + _dot(lo, u)
            e = z - sp + ext[:, :t]
            a = jnp.exp(jnp.concatenate(
                [e[:, j * LANES:(j + 1) * LANES] + carries[h] for j in range(t // LANES)], axis=1))
            if diag:
                a = jnp.where(causal, a, 0.0)
            new_a.append(accs[h] + _dot(a.astype(BF16), v))
            new_c.append(carries[h] + ext[:, t:])
        return tuple(new_c), tuple(new_a)

    zeros_c = tuple(jnp.zeros((t, LANES), F32) for _ in range(n_heads))
    zeros_a = tuple(jnp.zeros((t, HEAD_DIM), F32) for _ in range(n_heads))
    carries, accs = block(kb_diag, zeros_c, zeros_a, True)

    def worst(cs):
        m = cs[0]
        for c in cs[1:]:
            m = jnp.maximum(m, c)
        return jnp.max(m)

    def cond(st):
        return jnp.logical_and(st[0] < kb_diag, st[1] > SB_DEAD)

    def body(st):
        j, _, cs, acs = st
        cs, acs = block(kb_diag - 1 - j, cs, acs, False)
        return j + 1, worst(cs), cs, acs

    _, _, _, accs = lax.while_loop(cond, body, (jnp.int32(0), worst(carries), carries, accs))
    for h in range(n_heads):
        o_ref[0, h] = accs[h].astype(BF16)


def _sb(q_hm, k_hm, v_hm, qoff):
    b, h, tq_all, _ = q_hm.shape
    lp = k_hm.shape[2]
    t = SB_T
    assert qoff % t == 0 and tq_all % t == 0 and lp >= qoff + tq_all
    jj = lax.broadcasted_iota(jnp.int32, (t, t + LANES), 0)
    ss = lax.broadcasted_iota(jnp.int32, (t, t + LANES), 1)
    u = jnp.logical_or(jj > ss, ss >= t).astype(BF16)
    kspec = pl.BlockSpec((1, h, lp, HEAD_DIM), lambda bi, i: (bi, 0, 0, 0))
    qspec = pl.BlockSpec((1, h, t, HEAD_DIM), lambda bi, i: (bi, 0, i, 0))
    return pl.pallas_call(
        functools.partial(_sb_kernel, qoff=qoff),
        grid=(b, tq_all // t),
        in_specs=[qspec, kspec, kspec, pl.BlockSpec((t, t + LANES), lambda bi, i: (0, 0))],
        out_specs=qspec,
        out_shape=jax.ShapeDtypeStruct((b, h, tq_all, HEAD_DIM), BF16),
        compiler_params=_cparams(("arbitrary", "arbitrary")),
        name="sb",
    )(q_hm, k_hm, v_hm, u)


def _band_kernel(q_ref, k_ref, v_ref, bm_ref, o_ref, *, tq, w, n_invalid):
    i = pl.program_id(1)
    if n_invalid:
        col = lax.broadcasted_iota(jnp.int32, (1, w), 1)
        valid = jnp.where(i * tq + col >= n_invalid, 0.0, NEG)
    for h in range(q_ref.shape[1]):
        lg = _nt_dot(q_ref[0, h], k_ref[0, h]) + bm_ref[h]
        if n_invalid:
            lg = lg + valid
        m = jnp.max(lg, axis=1, keepdims=True)
        p = jnp.exp(lg - m)
        l = jnp.sum(p, axis=1, keepdims=True)
        o_ref[0, h] = (_dot(p.astype(BF16), v_ref[0, h]) / l).astype(BF16)


def _band_bias(rel_table, tq):
    w = C_BAND + tq
    t = jnp.arange(tq)[:, None]
    c = jnp.arange(w)[None, :] - C_BAND
    k = jnp.arange(tq + w - 1)
    rel = jnp.clip(w - 1 - C_BAND - k, -REL_CLIP, REL_CLIP) + REL_CLIP
    bias = _toeplitz(rel_table.astype(F32)[rel].T, tq, w)
    qc = t // CHUNK
    kc = jnp.floor_divide(c, CHUNK)
    mask = (kc <= qc) & (kc >= qc - C_BAND_CHUNKS)
    return jnp.where(mask[None], bias, NEG)


def _band(q_hm, k_ext, v_ext, bm, tq, n_invalid):
    b, h, t, _ = q_hm.shape
    w = C_BAND + tq
    kspec = pl.BlockSpec((pl.Element(1), pl.Element(h), pl.Element(w), pl.Element(HEAD_DIM)),
                         lambda bi, i: (bi, 0, i * tq, 0))
    qspec = pl.BlockSpec((1, h, tq, HEAD_DIM), lambda bi, i: (bi, 0, i, 0))
    return pl.pallas_call(
        functools.partial(_band_kernel, tq=tq, w=w, n_invalid=n_invalid),
        grid=(b, t // tq),
        in_specs=[qspec, kspec, kspec, pl.BlockSpec((h, tq, w), lambda bi, i: (0, 0, 0))],
        out_specs=qspec,
        out_shape=jax.ShapeDtypeStruct((b, h, t, HEAD_DIM), BF16),
        compiler_params=_cparams(("arbitrary", "arbitrary")),
        name="band",
    )(q_hm, k_ext, v_ext, bm)


def _merge_kernel(x_ref, h_ref, oa_ref, ob_ref, oc_ref, g1_ref, sc2_ref, sh2_ref, n2_ref,
                  wg_ref, bg_ref, wb_ref, wo_ref, wr_ref, br_ref,
                  xo_ref, h2_ref, gates_ref):
    d = x_ref.shape[-1]
    hb = h_ref[0]
    mix = None
    off = 0
    for j, o_ref in enumerate((oa_ref, ob_ref, oc_ref)):
        width = o_ref.shape[-1]
        y = _dot(o_ref[0], wb_ref[off:off + width, :])
        off += width
        g = jax.nn.sigmoid(_dot(hb, wg_ref[:, j * d:(j + 1) * d]) + bg_ref[:, j * d:(j + 1) * d])
        mix = g * y if mix is None else mix + g * y
    x = x_ref[0] + g1_ref[0] * _dot(mix.astype(BF16), wo_ref[...])
    xo_ref[0] = x
    h2 = _rms_mod(x, n2_ref[...], sc2_ref[0], sh2_ref[0])
    h2_ref[0] = h2.astype(BF16)

    h2_hi = h2.astype(BF16)
    h2_lo = (h2 - h2_hi.astype(F32)).astype(BF16)
    wr_hi, wr_lo = wr_ref[0], wr_ref[1]
    lr = _dot(h2_hi, wr_hi) + (_dot(h2_lo, wr_hi) + _dot(h2_hi, wr_lo)) + br_ref[...]
    tm = lr.shape[0]
    lane = lax.broadcasted_iota(jnp.int32, (tm, LANES), 1)
    lanef = lane.astype(F32)
    ninf = -jnp.inf
    is_expert = lane < N_EXPERTS
    lg = jnp.where(is_expert, ninf, jnp.where(lane < N_EXPERTS + N_GROUPS, lr, ninf))
    eg = jnp.exp(lg - jnp.max(lg, axis=1, keepdims=True))
    pg = eg / jnp.sum(eg, axis=1, keepdims=True)
    pg_top = jnp.max(pg, axis=1, keepdims=True)
    g_top = jnp.min(jnp.where(pg == pg_top, lanef, float(LANES)), axis=1, keepdims=True) - float(N_EXPERTS)
    in_group = (lane // EXP_PER_GROUP).astype(F32) == g_top
    le = jnp.where(is_expert, jnp.where(in_group, lr, ninf), ninf)
    m1 = jnp.max(le, axis=1, keepdims=True)
    i1 = jnp.min(jnp.where(le == m1, lanef, float(LANES)), axis=1, keepdims=True)
    le2 = jnp.where(lanef == i1, ninf, le)
    m2 = jnp.max(le2, axis=1, keepdims=True)
    i2 = jnp.min(jnp.where(le2 == m2, lanef, float(LANES)), axis=1, keepdims=True)
    e2 = jnp.exp(m2 - m1)
    den = 1.0 + e2
    gates_ref[0] = (jnp.where(lanef == i1, pg_top / den, 0.0)
                    + jnp.where(lanef == i2, pg_top * e2 / den, 0.0))


def _merge(x, h, oa, ob, oc, g1, sc2, sh2, n2, wg, bg, wb, wo, wr, br, tm):
    b, t, d = x.shape

    def tok(width):
        return pl.BlockSpec((1, tm, width), lambda bi, i: (bi, i, 0))

    vec = pl.BlockSpec((1, 1, d), lambda bi, i: (bi, 0, 0))

    def full(a):
        return pl.BlockSpec(a.shape, lambda bi, i: (0,) * a.ndim)

    n2r, bgr = n2.reshape(1, d), bg.reshape(1, 3 * d)
    return pl.pallas_call(
        _merge_kernel,
        grid=(b, t // tm),
        in_specs=[tok(d), tok(d), tok(oa.shape[-1]), tok(ob.shape[-1]), tok(oc.shape[-1]),
                  vec, vec, vec, full(n2r), full(wg), full(bgr), full(wb), full(wo), full(wr), full(br)],
        out_specs=[tok(d), tok(d), tok(LANES)],
        out_shape=[jax.ShapeDtypeStruct((b, t, d), F32), jax.ShapeDtypeStruct((b, t, d), BF16),
                   jax.ShapeDtypeStruct((b, t, LANES), F32)],
        compiler_params=_cparams(("arbitrary", "arbitrary")),
        name="merge",
    )(x, h, oa, ob, oc, g1.reshape(b, 1, d), sc2.reshape(b, 1, d), sh2.reshape(b, 1, d),
      n2r, wg, bgr, wb, wo, wr, br)


def _moe_kernel(x_ref, h2_ref, gates_ref, g2_ref, w1_ref, w3_ref, w2_ref, o_ref, acc_ref):
    step = pl.program_id(2)
    n_per = w1_ref.shape[0]

    @pl.when(step == 0)
    def _():
        acc_ref[...] = jnp.zeros_like(acc_ref)

    hb = h2_ref[0]
    gates = gates_ref[0]
    lane = lax.broadcasted_iota(jnp.int32, gates.shape, 1)
    total = None
    for j in range(n_per):
        a = _dot(hb, w1_ref[j])
        bb = _dot(hb, w3_ref[j])
        u = (a * jax.nn.sigmoid(a)) * bb
        out = _dot(u.astype(BF16), w2_ref[j])
        ge = jnp.sum(jnp.where(lane == step * n_per + j, gates, 0.0), axis=1, keepdims=True)
        total = ge * out if total is None else total + ge * out
    acc_ref[...] += total

    @pl.when(step == pl.num_programs(2) - 1)
    def _():
        o_ref[0] = x_ref[0] + g2_ref[0] * acc_ref[...]


def _moe(x, h2, gates, g2, w1, w3, w2, tm):
    b, t, d = x.shape
    ne, _, f = w1.shape
    n_per = MOE_EXPERTS_PER_STEP
    tok = lambda width: pl.BlockSpec((1, tm, width), lambda bi, i, e: (bi, i, 0))
    return pl.pallas_call(
        _moe_kernel,
        grid=(b, t // tm, ne // n_per),
        in_specs=[tok(d), tok(d), tok(LANES),
                  pl.BlockSpec((1, 1, d), lambda bi, i, e: (bi, 0, 0)),
                  pl.BlockSpec((n_per, d, f), lambda bi, i, e: (e, 0, 0)),
                  pl.BlockSpec((n_per, d, f), lambda bi, i, e: (e, 0, 0)),
                  pl.BlockSpec((n_per, f, d), lambda bi, i, e: (e, 0, 0))],
        out_specs=tok(d),
        out_shape=jax.ShapeDtypeStruct((b, t, d), F32),
        scratch_shapes=[pltpu.VMEM((tm, d), F32)],
        compiler_params=_cparams(("arbitrary", "arbitrary", "arbitrary")),
        name="moe",
    )(x, h2, gates, g2.reshape(b, 1, d), w1, w3, w2)


def _final_norm_kernel(x_ref, g_ref, o_ref):
    x = x_ref[0]
    ms = jnp.mean(x * x, axis=-1, keepdims=True)
    o_ref[0] = x * lax.rsqrt(ms + RMS_EPS) * g_ref[...]


def _final_norm(x, g, tm):
    b, t, d = x.shape
    spec = pl.BlockSpec((1, tm, d), lambda bi, i: (bi, i, 0))
    return pl.pallas_call(
        _final_norm_kernel,
        grid=(b, t // tm),
        in_specs=[spec, pl.BlockSpec((1, d), lambda bi, i: (0, 0))],
        out_specs=spec,
        out_shape=jax.ShapeDtypeStruct((b, t, d), F32),
        compiler_params=_cparams(("arbitrary", "arbitrary")),
        name="final_norm",
    )(x, g.reshape(1, d))


def _prep_layer_weights(w_in, w_gate, w_branch, w_out, w_rg, b_rg, w_re, b_re, w1, w3, w2):
    d = w_in.shape[0]
    qa, ka, va, qi, ki, wi, qb, kb, vb, qc, kc, vc = _split_in(w_in)
    pad = jnp.zeros((d, 256 - (64 * 3 + H_IDX)), w_in.dtype)
    w_in_p = jnp.concatenate([qa, qi, ka, va, ki, wi, pad, qb, kb, vb, qc, kc, vc], axis=1).astype(BF16)
    wr = jnp.zeros((d, LANES), F32).at[:, :N_EXPERTS].set(w_re).at[:, N_EXPERTS:N_EXPERTS + N_GROUPS].set(w_rg)
    br = jnp.zeros((1, LANES), F32).at[0, :N_EXPERTS].set(b_re).at[0, N_EXPERTS:N_EXPERTS + N_GROUPS].set(b_rg)
    wr_hi = wr.astype(BF16)
    wr = jnp.stack([wr_hi, (wr - wr_hi.astype(F32)).astype(BF16)])
    return (w_in_p, w_gate.astype(BF16), w_branch.astype(BF16), w_out.astype(BF16), wr, br,
            w1.astype(BF16), w3.astype(BF16), w2.astype(BF16))


def _split_in(w_in):
    sizes = (256, 64, 64, 256, 64, H_IDX, 256, 256, 256, 512, 512, 512)
    out, start = [], 0
    for n in sizes:
        out.append(w_in[:, start:start + n])
        start += n
    return out


def _from_hm(o_hm):
    b, h, t, dh = o_hm.shape
    return jnp.swapaxes(o_hm, 1, 2).reshape(b, t, h * dh)


def _to_hm(a, dtype):
    return jnp.swapaxes(a, 1, 2).astype(dtype)


def _pad_axis(a, axis, size):
    if a.shape[axis] == size:
        return a
    widths = [(0, 0)] * a.ndim
    widths[axis] = (0, size - a.shape[axis])
    return jnp.pad(a, widths)


def _round_up(n, m):
    return -(-n // m) * m


def _layer(x, mods, norms, lw, consts, cache):
    n1, n2 = norms
    sh1, sc1, g1, sh2, sc2, g2 = mods
    w_in_p, wg, bg, wb, wo, wr, br, w1, w3, w2 = lw
    dsa_bias, band_bias = consts
    b, t, d = x.shape
    tm = min(512, t)

    (h, qa_hm, qi_hm, sm, qb_hm, kb_hm, vb_hm, kb, vb, qc_hm, kc_hm, vc_hm, kcs, vcs) = _proj(
        x, sc1, sh1, n1, w_in_p, tm)
    ka, va, ki = sm[..., 0:64], sm[..., 64:128], sm[..., 128:192]
    wi_t = jnp.swapaxes(sm[..., 192:192 + H_IDX], 1, 2) * (H_IDX ** -0.5)

    if cache is None:
        qoff, tq_pad = 0, t
        ka_f, va_f, ki_f = ka, va, ki
        kb_f, vb_f = kb_hm, vb_hm
        kc_f = jnp.pad(kc_hm, ((0, 0), (0, 0), (C_BAND, 0), (0, 0)))
        vc_f = jnp.pad(vc_hm, ((0, 0), (0, 0), (C_BAND, 0), (0, 0)))
        band_tq, n_invalid = BAND_TQ, C_BAND
        state = (ka, va, ki, kb.reshape(b, t, H_B, HEAD_DIM), vb.reshape(b, t, H_B, HEAD_DIM),
                 kcs.reshape(b, -1, H_C, HEAD_DIM), vcs.reshape(b, -1, H_C, HEAD_DIM))
    else:
        ca_k, ca_v, ca_ki, cb_k, cb_v, cc_k, cc_v = cache
        qoff = ca_k.shape[1]
        tq_pad = _round_up(t, DSA_TQ)
        ka_f = jnp.concatenate([ca_k, ka], axis=1)
        va_f = jnp.concatenate([ca_v, va], axis=1)
        ki_f = jnp.concatenate([ca_ki, ki], axis=1)
        kb_f = jnp.concatenate([_to_hm(cb_k, BF16), kb_hm], axis=2)
        vb_f = jnp.concatenate([_to_hm(cb_v, BF16), vb_hm], axis=2)
        kc_f = jnp.concatenate([_to_hm(cc_k, BF16), kc_hm], axis=2)
        vc_f = jnp.concatenate([_to_hm(cc_v, BF16), vc_hm], axis=2)
        band_tq, n_invalid = CHUNK, 0
        kc_new = kcs.reshape(b, t, H_C, HEAD_DIM)
        vc_new = vcs.reshape(b, t, H_C, HEAD_DIM)
        state = (ka, va, ki, kb.reshape(b, t, H_B, HEAD_DIM), vb.reshape(b, t, H_B, HEAD_DIM),
                 jnp.concatenate([cc_k, kc_new], axis=1)[:, t:],
                 jnp.concatenate([cc_v, vc_new], axis=1)[:, t:])

    l_valid = ka_f.shape[1]
    topk = min(TOPK_MAX, l_valid // 4)
    lp = _round_up(l_valid, DSA_LC1)
    oa_t = _dsa(_pad_axis(qi_hm, 2, tq_pad), _pad_axis(qa_hm, 2, tq_pad), _pad_axis(wi_t, 2, tq_pad),
                _pad_axis(ki_f.astype(BF16), 1, lp), _pad_axis(ka_f.astype(BF16), 1, lp),
                _pad_axis(va_f.astype(BF16), 1, lp), dsa_bias, qoff, l_valid, topk)
    oa = jnp.transpose(oa_t[..., :t], (0, 3, 1, 2)).reshape(b, t, H_A * HEAD_DIM)

    tq_sb = _round_up(t, SB_T)
    lp_b = qoff + tq_sb
    ob_hm = _sb(_pad_axis(qb_hm, 2, tq_sb), _pad_axis(kb_f, 2, lp_b), _pad_axis(vb_f, 2, lp_b), qoff)
    ob = _from_hm(ob_hm[:, :, :t])

    oc = _from_hm(_band(qc_hm, kc_f, vc_f, band_bias[band_tq], band_tq, n_invalid))

    x, h2, gates = _merge(x, h, oa, ob, oc, g1, sc2, sh2, n2, wg, bg, wb, wo, wr, br, min(512, t))
    x = _moe(x, h2, gates, g2, w1, w3, w2, min(512, t))
    return x, state


def kernel(x_prompt, x_sample, c_prompt, c_sample, cache_a_k, cache_a_v, cache_a_kidx, cache_b_k, cache_b_v, cache_c_k, cache_c_v, norm1, norm2, final_norm, w_ada, b_ada, w_in, t5_table, rel_c, w_gate, b_gate, w_branch, w_out, w_rg, b_rg, w_re, b_re, w1, w3, w2):
    depth = norm1.shape[0]
    bp = x_prompt.shape[0]
    mods_all = _ada_mods(jnp.concatenate([c_prompt, c_sample], axis=0), w_ada, b_ada)
    dsa_bias = _dsa_bias_tiles(t5_table)
    xp, xs = x_prompt, x_sample
    st_p, st_s = [], []
    for l in range(depth):
        lw = _prep_layer_weights(w_in[l], w_gate[l], w_branch[l], w_out[l], w_rg[l], b_rg[l],
                                 w_re[l], b_re[l], w1[l], w3[l], w2[l])
        lw = lw[:2] + (b_gate[l],) + lw[2:]
        band_bias = {tq: _band_bias(rel_c[l], tq) for tq in (CHUNK, BAND_TQ)}
        consts = (dsa_bias, band_bias)
        norms = (norm1[l], norm2[l])
        mods = jnp.split(mods_all[l], 6, axis=-1)
        xp, sp = _layer(xp, [m[:bp] for m in mods], norms, lw, consts, None)
        cache = (cache_a_k[l], cache_a_v[l], cache_a_kidx[l], cache_b_k[l], cache_b_v[l],
                 cache_c_k[l], cache_c_v[l])
        xs, ss = _layer(xs, [m[bp:] for m in mods], norms, lw, consts, cache)
        st_p.append(sp)
        st_s.append(ss)
    y_prompt = _final_norm(xp, final_norm, min(512, xp.shape[1]))
    y_sample = _final_norm(xs, final_norm, min(512, xs.shape[1]))
    stack = lambda states, i: jnp.stack([s[i] for s in states], axis=0)
    return ((y_prompt, y_sample) + tuple(stack(st_p, i) for i in range(7))
            + tuple(stack(st_s, i) for i in range(7)))
```

```python
import functools
import math

import jax
import jax.numpy as jnp
from jax import lax
from jax.experimental import pallas as pl
from jax.experimental.pallas import tpu as pltpu

F32 = jnp.float32
BF16 = jnp.bfloat16
HIGHEST = lax.Precision.HIGHEST

CHUNK = 64
HEAD_DIM = 64
D_IDX = 64
H_A = 4
H_IDX = 4
H_B = 4
H_C = 8
C_BAND_CHUNKS = 8
C_BAND = C_BAND_CHUNKS * CHUNK
REL_CLIP = 256
N_BUCKETS = 32
T5_MAX_DIST = 1024
TOPK_MAX = 256
N_GROUPS = 4
EXP_PER_GROUP = 4
N_EXPERTS = N_GROUPS * EXP_PER_GROUP
RMS_EPS = 1e-6

NEG = -1e30
MIN_NORMAL_BITS = 0x00800000
ZERO_KEY_TOP = 1 << 14
INT_MIN = -(2 ** 31)
LANES = 128
VMEM_LIMIT = 56 * 1024 * 1024

DSA_TQ = 128
DSA_LC1 = 512
DSA_LC3 = 256
SB_T = 256
SB_DEAD = -104.0
BAND_TQ = 4 * CHUNK
MOE_EXPERTS_PER_STEP = 4
DSA_NEAR = -(-(DSA_LC3 - 1 + T5_MAX_DIST) // DSA_TQ)

_C_QA, _C_QI, _C_SM, _C_QB, _C_KB, _C_VB, _C_QC, _C_KC, _C_VC, _C_END = (
    0, 256, 512, 768, 1024, 1280, 1536, 2048, 2560, 3072)


def _cparams(sem):
    return pltpu.CompilerParams(dimension_semantics=sem, vmem_limit_bytes=VMEM_LIMIT)


def _nt_dot(a, b):
    return lax.dot_general(a, b, (((1,), (1,)), ((), ())), preferred_element_type=F32)


def _dot(a, b):
    return jnp.dot(a, b, preferred_element_type=F32)


def _col_reduce(x, op):
    r, c = x.shape
    if r > 64:
        x = op(x.reshape(r // 64, 64, c), axis=0)
    return op(x, axis=0, keepdims=True)


def _rms_mod(x, g, sc, sh):
    ms = jnp.mean(x * x, axis=-1, keepdims=True)
    return (x * lax.rsqrt(ms + RMS_EPS) * g) * (1.0 + sc) + sh


def _ada_kernel(c_ref, w_ref, b_ref, o_ref):
    c = c_ref[...]
    s = c * jax.nn.sigmoid(c)
    o_ref[0] = jnp.dot(s, w_ref[0], preferred_element_type=F32, precision=HIGHEST) + b_ref[0]


def _ada_mods(c_all, w_ada, b_ada):
    depth, d, e = w_ada.shape
    r = c_all.shape[0]
    tn = 1024
    return pl.pallas_call(
        _ada_kernel,
        grid=(depth, e // tn),
        in_specs=[pl.BlockSpec((r, d), lambda l, j: (0, 0)),
                  pl.BlockSpec((1, d, tn), lambda l, j: (l, 0, j)),
                  pl.BlockSpec((1, 1, tn), lambda l, j: (l, 0, j))],
        out_specs=pl.BlockSpec((1, r, tn), lambda l, j: (l, 0, j)),
        out_shape=jax.ShapeDtypeStruct((depth, r, e), F32),
        compiler_params=_cparams(("arbitrary", "arbitrary")),
        name="ada_mods",
    )(c_all, w_ada, b_ada.reshape(depth, 1, e))


def _proj_kernel(x_ref, sc_ref, sh_ref, g_ref, w_ref, *refs, n_tiles, n_state_tiles, n_alias):
    (h_ref, qa_ref, qi_ref, sm_ref, qb_ref, kbh_ref, vbh_ref, kb_ref, vb_ref,
     qc_ref, kch_ref, vch_ref, kcs_ref, vcs_ref) = refs[n_alias:]
    i = pl.program_id(1)
    h = _rms_mod(x_ref[0], g_ref[...], sc_ref[0], sh_ref[0])
    hb = h.astype(BF16)
    h_ref[0] = hb

    def mm(lo, hi):
        return _dot(hb, w_ref[:, lo:hi])

    def heads(ref, y, n, scale=None):
        for hh in range(n):
            part = y[:, hh * HEAD_DIM:(hh + 1) * HEAD_DIM]
            if scale is not None:
                part = part * scale
            ref[0, hh] = part.astype(BF16)

    qscale = HEAD_DIM ** -0.5
    heads(qa_ref, mm(_C_QA, _C_QI), H_A, qscale)
    heads(qi_ref, mm(_C_QI, _C_SM), H_IDX, D_IDX ** -0.5)
    sm_ref[0] = mm(_C_SM, _C_QB)
    heads(qb_ref, mm(_C_QB, _C_KB), H_B, qscale)
    kb = mm(_C_KB, _C_VB)
    kb_ref[0] = kb
    heads(kbh_ref, kb, H_B)
    vb = mm(_C_VB, _C_QC)
    vb_ref[0] = vb
    heads(vbh_ref, vb, H_B)
    heads(qc_ref, mm(_C_QC, _C_KC), H_C, qscale)
    kc = mm(_C_KC, _C_VC)
    heads(kch_ref, kc, H_C)
    vc = mm(_C_VC, _C_END)
    heads(vch_ref, vc, H_C)

    @pl.when(i >= n_tiles - n_state_tiles)
    def _():
        kcs_ref[0] = kc
        vcs_ref[0] = vc


def _proj(x, sc, sh, g, w_in_p, tm, kpad):
    b, t, d = x.shape
    n_tiles = t // tm
    n_state_tiles = min(C_BAND, t) // tm
    nbuf = n_state_tiles * tm
    assert kpad % tm == 0

    def hm(n, pad=0):
        return (pl.BlockSpec((1, n, tm, HEAD_DIM), lambda bi, i: (bi, 0, i + pad // tm, 0)),
                jax.ShapeDtypeStruct((b, n, pad + t, HEAD_DIM), BF16))

    def tok(width, dtype):
        return (pl.BlockSpec((1, tm, width), lambda bi, i: (bi, i, 0)),
                jax.ShapeDtypeStruct((b, t, width), dtype))

    def state(width):
        return (pl.BlockSpec((1, tm, width),
                             lambda bi, i: (bi, jnp.maximum(i - (n_tiles - n_state_tiles), 0), 0)),
                jax.ShapeDtypeStruct((b, nbuf, width), F32))

    outs = [tok(d, BF16), hm(H_A), hm(H_IDX), tok(256, F32), hm(H_B), hm(H_B), hm(H_B),
            tok(256, F32), tok(256, F32), hm(H_C), hm(H_C, kpad), hm(H_C, kpad), state(512), state(512)]
    vec = pl.BlockSpec((1, 1, d), lambda bi, i: (bi, 0, 0))
    in_specs = [pl.BlockSpec((1, tm, d), lambda bi, i: (bi, i, 0)), vec, vec,
                pl.BlockSpec((1, d), lambda bi, i: (0, 0)),
                pl.BlockSpec((d, _C_END), lambda bi, i: (0, 0))]
    args = [x, sc.reshape(b, 1, d), sh.reshape(b, 1, d), g.reshape(1, d), w_in_p]
    aliases = {}
    if kpad:
        for out_idx in (10, 11):
            aliases[len(args)] = out_idx
            in_specs.append(pl.BlockSpec(memory_space=pl.ANY))
            args.append(jnp.zeros(outs[out_idx][1].shape, BF16))
    return pl.pallas_call(
        functools.partial(_proj_kernel, n_tiles=n_tiles, n_state_tiles=n_state_tiles, n_alias=len(aliases)),
        grid=(b, n_tiles),
        in_specs=in_specs,
        out_specs=[o[0] for o in outs],
        out_shape=[o[1] for o in outs],
        input_output_aliases=aliases,
        compiler_params=_cparams(("arbitrary", "arbitrary")),
        name="proj",
    )(*args)


def _dsa_kernel(qi_ref, qa_ref, wi_ref, ki_ref, ka_ref, vt_ref, bias_ref, o_ref, key_ref,
                *, qoff, l_valid, topk):
    tq, lc1, lc3 = DSA_TQ, DSA_LC1, DSA_LC3
    i = pl.program_id(1)
    q0 = qoff + i * tq
    lane = lax.broadcasted_iota(jnp.int32, (1, tq), 1)
    qpos = q0 + lane
    lim = jnp.minimum((qpos // CHUNK + 1) * CHUNK, l_valid)
    n_adm = jnp.minimum(((q0 + tq - 1) // CHUNK + 1) * CHUNK, l_valid)
    nch1 = (n_adm + lc1 - 1) // lc1
    qi = qi_ref[0].reshape(H_IDX * tq, D_IDX)
    qa = qa_ref[0].reshape(H_A * tq, HEAD_DIM)
    wi = wi_ref[0]

    def score_chunk(c, carry):
        s0 = pl.multiple_of(c * lc1, lc1)
        s_all = _nt_dot(ki_ref[0, pl.ds(s0, lc1), :], qi)
        sc = None
        for h in range(H_IDX):
            term = wi[h:h + 1, :] * jnp.maximum(s_all[:, h * tq:(h + 1) * tq], 0.0)
            sc = term if sc is None else sc + term
        bits = lax.bitcast_convert_type(sc, jnp.int32)
        key = bits ^ (lax.shift_right_arithmetic(bits, 31) & 0x7FFFFFFF)
        sidx = s0 + lax.broadcasted_iota(jnp.int32, (lc1, tq), 0)
        tiny = (bits & 0x7FFFFFFF) < MIN_NORMAL_BITS
        key = jnp.where(tiny, jnp.where(bits < 0, -1 - sidx, ZERO_KEY_TOP - sidx), key)
        key_ref[pl.ds(s0, lc1), :] = jnp.where(sidx < lim, key, INT_MIN)
        return carry

    lax.fori_loop(0, nch1, score_chunk, 0)

    def count(*preds):
        def body(c, accs):
            s0 = pl.multiple_of(c * lc1, lc1)
            kk = key_ref[pl.ds(s0, lc1), :]
            sidx = s0 + lax.broadcasted_iota(jnp.int32, (lc1, tq), 0)
            return tuple(acc + jnp.sum(pred(kk, sidx).reshape(lc1 // 64, 64, tq), axis=0)
                         for acc, pred in zip(accs, preds))
        accs = lax.fori_loop(0, nch1, body, tuple(jnp.zeros((64, tq), F32) for _ in preds))
        return tuple(jnp.sum(acc, axis=0, keepdims=True) for acc in accs)

    kf = float(topk)

    def bit_step(it, thr):
        cand = thr + lax.shift_left(jnp.int32(1), 31 - it)
        n_ge, = count(lambda kk, sidx: jnp.where(kk >= cand, 1.0, 0.0))
        return jnp.where(n_ge >= kf, cand, thr)

    thr = lax.fori_loop(0, 32, bit_step, jnp.full((1, tq), INT_MIN, jnp.int32))
    thr = jnp.maximum(thr, INT_MIN + 1)

    n_gt, n_eq = count(lambda kk, sidx: jnp.where(kk > thr, 1.0, 0.0),
                       lambda kk, sidx: jnp.where(kk == thr, 1.0, 0.0))
    room = kf - n_gt
    idx_bits = 14
    assert l_valid < (1 << idx_bits) <= ZERO_KEY_TOP

    def tie_search():
        def step(it, end):
            cand = end + lax.shift_left(jnp.int32(1), idx_bits - 1 - it)
            n, = count(lambda kk, sidx: jnp.where(kk == thr, jnp.where(sidx < cand, 1.0, 0.0), 0.0))
            return jnp.where(n <= room, cand, end)
        return lax.fori_loop(0, idx_bits, step, jnp.zeros((1, tq), jnp.int32))

    tie_end = lax.cond(jnp.max(n_eq - room) > 0.0, tie_search,
                       lambda: jnp.full((1, tq), 1 << idx_bits, jnp.int32))

    n_sub = lc1 // lc3

    def logits_chunk(c):
        s0 = pl.multiple_of(c * lc3, lc3)
        lg = _nt_dot(ka_ref[0, pl.ds(s0, lc3), :], qa)
        kk = key_ref[pl.ds(s0, lc3), :]
        sidx = s0 + lax.broadcasted_iota(jnp.int32, (lc3, tq), 0)
        madd = jnp.where(kk > thr, 0.0,
                         jnp.where(kk == thr, jnp.where(sidx < tie_end, 0.0, NEG), NEG))
        d = jnp.clip((q0 - s0) // tq, 0, DSA_NEAR)
        return jnp.concatenate(
            [lg[:, h * tq:(h + 1) * tq] + bias_ref[d, h] + madd for h in range(H_A)], axis=1)

    def attend_step(c1, carry):
        m, l, acc = carry
        lgs = [logits_chunk(c1 * n_sub + sub) for sub in range(n_sub)]
        m_new = m
        for lgb in lgs:
            m_new = jnp.maximum(m_new, _col_reduce(lgb, jnp.max))
        alpha = jnp.exp(m - m_new)
        l = l * alpha
        acc = acc * alpha
        for sub, lgb in enumerate(lgs):
            p = jnp.exp(lgb - m_new)
            l = l + _col_reduce(p, jnp.sum)
            acc = acc + _dot(vt_ref[0, c1 * n_sub + sub], p.astype(BF16))
        return m_new, l, acc

    m0 = jnp.full((1, H_A * tq), NEG, F32)
    l0 = jnp.zeros((1, H_A * tq), F32)
    a0 = jnp.zeros((HEAD_DIM, H_A * tq), F32)
    _, l, acc = lax.fori_loop(0, nch1, attend_step, (m0, l0, a0))
    out = acc / l
    for h in range(H_A):
        o_ref[0, h] = out[:, h * tq:(h + 1) * tq].astype(BF16)


def _t5_bucket(rel):
    nb = N_BUCKETS // 2
    max_exact = nb // 2
    ret = jnp.where(rel > 0, nb, 0).astype(jnp.int32)
    n = jnp.abs(rel)
    n_f = jnp.maximum(n, 1).astype(F32)
    large = max_exact + (jnp.log(n_f / max_exact) / math.log(T5_MAX_DIST / max_exact)
                         * (nb - max_exact)).astype(jnp.int32)
    large = jnp.minimum(large, nb - 1)
    return ret + jnp.where(n < max_exact, n, large).astype(jnp.int32)


def _toeplitz(vec, n, m):
    length = n + m - 1
    lead = vec.shape[:-1]
    flat = jnp.tile(vec, (1,) * len(lead) + (n + 1,))[..., :n * (length + 1)]
    hankel = flat.reshape(lead + (n, length + 1))[..., :m]
    return hankel[..., ::-1]


def _dsa_bias_tiles(t5_table):
    d = jnp.arange(DSA_NEAR + 1)[:, None]
    k = jnp.arange(DSA_LC3 + DSA_TQ - 1)[None, :]
    rel = k - (DSA_TQ - 1) - DSA_TQ * d
    rel = jnp.where(d == DSA_NEAR, -T5_MAX_DIST, rel)
    vec = jnp.moveaxis(t5_table.astype(F32)[_t5_bucket(rel)], -1, 1)
    return _toeplitz(vec, DSA_LC3, DSA_TQ)


def _dsa(qi_hm, qa_hm, wi_t, ki, ka, va, bias_tiles, qoff, l_valid, topk):
    b, _, tq_all, _ = qi_hm.shape
    lp = ki.shape[1]
    nq = tq_all // DSA_TQ
    nc3 = lp // DSA_LC3
    vt = jnp.swapaxes(va.reshape(b, nc3, DSA_LC3, HEAD_DIM), 2, 3)
    qspec = pl.BlockSpec((1, H_A, DSA_TQ, HEAD_DIM), lambda bi, i: (bi, 0, i, 0))
    kspec = pl.BlockSpec((1, lp, HEAD_DIM), lambda bi, i: (bi, 0, 0))
    return pl.pallas_call(
        functools.partial(_dsa_kernel, qoff=qoff, l_valid=l_valid, topk=topk),
        grid=(b, nq),
        in_specs=[qspec, qspec,
                  pl.BlockSpec((1, H_IDX, DSA_TQ), lambda bi, i: (bi, 0, i)),
                  kspec, kspec,
                  pl.BlockSpec((1, nc3, HEAD_DIM, DSA_LC3), lambda bi, i: (bi, 0, 0, 0)),
                  pl.BlockSpec(bias_tiles.shape, lambda bi, i: (0, 0, 0, 0))],
        out_specs=pl.BlockSpec((1, H_A, HEAD_DIM, DSA_TQ), lambda bi, i: (bi, 0, 0, i)),
        out_shape=jax.ShapeDtypeStruct((b, H_A, HEAD_DIM, tq_all), BF16),
        scratch_shapes=[pltpu.VMEM((lp, DSA_TQ), jnp.int32)],
        compiler_params=_cparams(("arbitrary", "arbitrary")),
        name="dsa",
    )(qi_hm, qa_hm, wi_t, ki, ka, vt, bias_tiles)


def _sb_kernel(q_ref, k_ref, v_ref, u_ref, o_ref, *, qoff):
    t = SB_T
    n_heads = q_ref.shape[1]
    i = pl.program_id(1)
    u = u_ref[...]
    kb_diag = qoff // t + i
    row = lax.broadcasted_iota(jnp.int32, (t, t), 0)
    col = lax.broadcasted_iota(jnp.int32, (t, t), 1)
    causal = col < row

    def block(kb, carries, accs, diag):
        s0 = pl.multiple_of(kb * t, t)
        new_c, new_a = [], []
        for h in range(n_heads):
            k = k_ref[0, h, pl.ds(s0, t), :]
            v = v_ref[0, h, pl.ds(s0, t), :]
            z = _nt_dot(q_ref[0, h], k)
            sp = jnp.maximum(z, 0.0) + jnp.log1p(jnp.exp(-jnp.abs(z)))
            lm = -sp
            if diag:
                lm = jnp.where(causal, lm, 0.0)
            hi = lm.astype(BF16)
            lo = (lm - hi.astype(F32)).astype(BF16)
            ext = _dot(hi, u) + _dot(lo, u)
            e = z - sp + ext[:, :t]
            a = jnp.exp(jnp.concatenate(
                [e[:, j * LANES:(j + 1) * LANES] + carries[h] for j in range(t // LANES)], axis=1))
            if diag:
                a = jnp.where(causal, a, 0.0)
            new_a.append(accs[h] + _dot(a.astype(BF16), v))
            new_c.append(carries[h] + ext[:, t:])
        return tuple(new_c), tuple(new_a)

    zeros_c = tuple(jnp.zeros((t, LANES), F32) for _ in range(n_heads))
    zeros_a = tuple(jnp.zeros((t, HEAD_DIM), F32) for _ in range(n_heads))
    carries, accs = block(kb_diag, zeros_c, zeros_a, True)

    def worst(cs):
        m = cs[0]
        for c in cs[1:]:
            m = jnp.maximum(m, c)
        return jnp.max(m)

    def cond(st):
        return jnp.logical_and(st[0] < kb_diag, st[1] > SB_DEAD)

    def body(st):
        j, _, cs, acs = st
        cs, acs = block(kb_diag - 1 - j, cs, acs, False)
        return j + 1, worst(cs), cs, acs

    _, _, _, accs = lax.while_loop(cond, body, (jnp.int32(0), worst(carries), carries, accs))
    for h in range(n_heads):
        o_ref[0, h] = accs[h].astype(BF16)


def _sb(q_hm, k_hm, v_hm, qoff):
    b, h, tq_all, _ = q_hm.shape
    lp = k_hm.shape[2]
    t = SB_T
    assert qoff % t == 0 and tq_all % t == 0 and lp >= qoff + tq_all
    jj = lax.broadcasted_iota(jnp.int32, (t, t + LANES), 0)
    ss = lax.broadcasted_iota(jnp.int32, (t, t + LANES), 1)
    u = jnp.logical_or(jj > ss, ss >= t).astype(BF16)
    kspec = pl.BlockSpec((1, h, lp, HEAD_DIM), lambda bi, i: (bi, 0, 0, 0))
    qspec = pl.BlockSpec((1, h, t, HEAD_DIM), lambda bi, i: (bi, 0, i, 0))
    return pl.pallas_call(
        functools.partial(_sb_kernel, qoff=qoff),
        grid=(b, tq_all // t),
        in_specs=[qspec, kspec, kspec, pl.BlockSpec((t, t + LANES), lambda bi, i: (0, 0))],
        out_specs=qspec,
        out_shape=jax.ShapeDtypeStruct((b, h, tq_all, HEAD_DIM), BF16),
        compiler_params=_cparams(("arbitrary", "arbitrary")),
        name="sb",
    )(q_hm, k_hm, v_hm, u)


def _band_kernel(q_ref, k_ref, v_ref, bm_ref, o_ref, *, tq, w, n_invalid):
    i = pl.program_id(1)
    if n_invalid:
        col = lax.broadcasted_iota(jnp.int32, (1, w), 1)
        valid = jnp.where(i * tq + col >= n_invalid, 0.0, NEG)
    for h in range(q_ref.shape[1]):
        lg = _nt_dot(q_ref[0, h], k_ref[0, h]) + bm_ref[h]
        if n_invalid:
            lg = lg + valid
        m = jnp.max(lg, axis=1, keepdims=True)
        p = jnp.exp(lg - m)
        l = jnp.sum(p, axis=1, keepdims=True)
        o_ref[0, h] = (_dot(p.astype(BF16), v_ref[0, h]) / l).astype(BF16)


def _band_bias(rel_table, tq):
    w = C_BAND + tq
    t = jnp.arange(tq)[:, None]
    c = jnp.arange(w)[None, :] - C_BAND
    k = jnp.arange(tq + w - 1)
    rel = jnp.clip(w - 1 - C_BAND - k, -REL_CLIP, REL_CLIP) + REL_CLIP
    bias = _toeplitz(rel_table.astype(F32)[rel].T, tq, w)
    qc = t // CHUNK
    kc = jnp.floor_divide(c, CHUNK)
    mask = (kc <= qc) & (kc >= qc - C_BAND_CHUNKS)
    return jnp.where(mask[None], bias, NEG)


def _band(q_hm, k_ext, v_ext, bm, tq, n_invalid):
    b, h, t, _ = q_hm.shape
    w = C_BAND + tq
    kspec = pl.BlockSpec((pl.Element(1), pl.Element(h), pl.Element(w), pl.Element(HEAD_DIM)),
                         lambda bi, i: (bi, 0, i * tq, 0))
    qspec = pl.BlockSpec((1, h, tq, HEAD_DIM), lambda bi, i: (bi, 0, i, 0))
    return pl.pallas_call(
        functools.partial(_band_kernel, tq=tq, w=w, n_invalid=n_invalid),
        grid=(b, t // tq),
        in_specs=[qspec, kspec, kspec, pl.BlockSpec((h, tq, w), lambda bi, i: (0, 0, 0))],
        out_specs=qspec,
        out_shape=jax.ShapeDtypeStruct((b, h, t, HEAD_DIM), BF16),
        compiler_params=_cparams(("arbitrary", "arbitrary")),
        name="band",
    )(q_hm, k_ext, v_ext, bm)


def _merge_kernel(x_ref, h_ref, oa_ref, ob_ref, oc_ref, g1_ref, sc2_ref, sh2_ref, n2_ref,
                  wg_ref, bg_ref, wb_ref, wo_ref, wr_ref, br_ref,
                  xo_ref, h2_ref, gates_ref):
    d = x_ref.shape[-1]
    hb = h_ref[0]
    mix = None
    off = 0
    for j, o_ref in enumerate((oa_ref, ob_ref, oc_ref)):
        width = o_ref.shape[-1]
        y = _dot(o_ref[0], wb_ref[off:off + width, :])
        off += width
        g = jax.nn.sigmoid(_dot(hb, wg_ref[:, j * d:(j + 1) * d]) + bg_ref[:, j * d:(j + 1) * d])
        mix = g * y if mix is None else mix + g * y
    x = x_ref[0] + g1_ref[0] * _dot(mix.astype(BF16), wo_ref[...])
    xo_ref[0] = x
    h2 = _rms_mod(x, n2_ref[...], sc2_ref[0], sh2_ref[0])
    h2_ref[0] = h2.astype(BF16)

    h2_hi = h2.astype(BF16)
    h2_lo = (h2 - h2_hi.astype(F32)).astype(BF16)
    wr_hi, wr_lo = wr_ref[0], wr_ref[1]
    lr = _dot(h2_hi, wr_hi) + (_dot(h2_lo, wr_hi) + _dot(h2_hi, wr_lo)) + br_ref[...]
    tm = lr.shape[0]
    lane = lax.broadcasted_iota(jnp.int32, (tm, LANES), 1)
    lanef = lane.astype(F32)
    ninf = -jnp.inf
    is_expert = lane < N_EXPERTS
    lg = jnp.where(is_expert, ninf, jnp.where(lane < N_EXPERTS + N_GROUPS, lr, ninf))
    eg = jnp.exp(lg - jnp.max(lg, axis=1, keepdims=True))
    pg = eg / jnp.sum(eg, axis=1, keepdims=True)
    pg_top = jnp.max(pg, axis=1, keepdims=True)
    g_top = jnp.min(jnp.where(pg == pg_top, lanef, float(LANES)), axis=1, keepdims=True) - float(N_EXPERTS)
    in_group = (lane // EXP_PER_GROUP).astype(F32) == g_top
    le = jnp.where(is_expert, jnp.where(in_group, lr, ninf), ninf)
    m1 = jnp.max(le, axis=1, keepdims=True)
    i1 = jnp.min(jnp.where(le == m1, lanef, float(LANES)), axis=1, keepdims=True)
    le2 = jnp.where(lanef == i1, ninf, le)
    m2 = jnp.max(le2, axis=1, keepdims=True)
    i2 = jnp.min(jnp.where(le2 == m2, lanef, float(LANES)), axis=1, keepdims=True)
    e2 = jnp.exp(m2 - m1)
    den = 1.0 + e2
    gates_ref[0] = (jnp.where(lanef == i1, pg_top / den, 0.0)
                    + jnp.where(lanef == i2, pg_top * e2 / den, 0.0))


def _merge(x, h, oa, ob, oc, g1, sc2, sh2, n2, wg, bg, wb, wo, wr, br, tm):
    b, t, d = x.shape

    def tok(width):
        return pl.BlockSpec((1, tm, width), lambda bi, i: (bi, i, 0))

    vec = pl.BlockSpec((1, 1, d), lambda bi, i: (bi, 0, 0))

    def full(a):
        return pl.BlockSpec(a.shape, lambda bi, i: (0,) * a.ndim)

    n2r, bgr = n2.reshape(1, d), bg.reshape(1, 3 * d)
    return pl.pallas_call(
        _merge_kernel,
        grid=(b, t // tm),
        in_specs=[tok(d), tok(d), tok(oa.shape[-1]), tok(ob.shape[-1]), tok(oc.shape[-1]),
                  vec, vec, vec, full(n2r), full(wg), full(bgr), full(wb), full(wo), full(wr), full(br)],
        out_specs=[tok(d), tok(d), tok(LANES)],
        out_shape=[jax.ShapeDtypeStruct((b, t, d), F32), jax.ShapeDtypeStruct((b, t, d), BF16),
                   jax.ShapeDtypeStruct((b, t, LANES), F32)],
        compiler_params=_cparams(("arbitrary", "arbitrary")),
        name="merge",
    )(x, h, oa, ob, oc, g1.reshape(b, 1, d), sc2.reshape(b, 1, d), sh2.reshape(b, 1, d),
      n2r, wg, bgr, wb, wo, wr, br)


def _moe_kernel(x_ref, h2_ref, gates_ref, g2_ref, w1_ref, w3_ref, w2_ref, o_ref, acc_ref):
    step = pl.program_id(2)
    n_per = w1_ref.shape[0]

    @pl.when(step == 0)
    def _():
        acc_ref[...] = jnp.zeros_like(acc_ref)

    hb = h2_ref[0]
    gates = gates_ref[0]
    lane = lax.broadcasted_iota(jnp.int32, gates.shape, 1)
    total = None
    for j in range(n_per):
        a = _dot(hb, w1_ref[j])
        bb = _dot(hb, w3_ref[j])
        u = (a * jax.nn.sigmoid(a)) * bb
        out = _dot(u.astype(BF16), w2_ref[j])
        ge = jnp.sum(jnp.where(lane == step * n_per + j, gates, 0.0), axis=1, keepdims=True)
        total = ge * out if total is None else total + ge * out
    acc_ref[...] += total

    @pl.when(step == pl.num_programs(2) - 1)
    def _():
        o_ref[0] = x_ref[0] + g2_ref[0] * acc_ref[...]


def _moe(x, h2, gates, g2, w1, w3, w2, tm):
    b, t, d = x.shape
    ne, _, f = w1.shape
    n_per = MOE_EXPERTS_PER_STEP
    tok = lambda width: pl.BlockSpec((1, tm, width), lambda bi, i, e: (bi, i, 0))
    return pl.pallas_call(
        _moe_kernel,
        grid=(b, t // tm, ne // n_per),
        in_specs=[tok(d), tok(d), tok(LANES),
                  pl.BlockSpec((1, 1, d), lambda bi, i, e: (bi, 0, 0)),
                  pl.BlockSpec((n_per, d, f), lambda bi, i, e: (e, 0, 0)),
                  pl.BlockSpec((n_per, d, f), lambda bi, i, e: (e, 0, 0)),
                  pl.BlockSpec((n_per, f, d), lambda bi, i, e: (e, 0, 0))],
        out_specs=tok(d),
        out_shape=jax.ShapeDtypeStruct((b, t, d), F32),
        scratch_shapes=[pltpu.VMEM((tm, d), F32)],
        compiler_params=_cparams(("arbitrary", "arbitrary", "arbitrary")),
        name="moe",
    )(x, h2, gates, g2.reshape(b, 1, d), w1, w3, w2)


def _final_norm_kernel(x_ref, g_ref, o_ref):
    x = x_ref[0]
    ms = jnp.mean(x * x, axis=-1, keepdims=True)
    o_ref[0] = x * lax.rsqrt(ms + RMS_EPS) * g_ref[...]


def _final_norm(x, g, tm):
    b, t, d = x.shape
    spec = pl.BlockSpec((1, tm, d), lambda bi, i: (bi, i, 0))
    return pl.pallas_call(
        _final_norm_kernel,
        grid=(b, t // tm),
        in_specs=[spec, pl.BlockSpec((1, d), lambda bi, i: (0, 0))],
        out_specs=spec,
        out_shape=jax.ShapeDtypeStruct((b, t, d), F32),
        compiler_params=_cparams(("arbitrary", "arbitrary")),
        name="final_norm",
    )(x, g.reshape(1, d))


def _prep_layer_weights(w_in, w_gate, w_branch, w_out, w_rg, b_rg, w_re, b_re, w1, w3, w2):
    d = w_in.shape[0]
    qa, ka, va, qi, ki, wi, qb, kb, vb, qc, kc, vc = _split_in(w_in)
    pad = jnp.zeros((d, 256 - (64 * 3 + H_IDX)), w_in.dtype)
    w_in_p = jnp.concatenate([qa, qi, ka, va, ki, wi, pad, qb, kb, vb, qc, kc, vc], axis=1).astype(BF16)
    wr = jnp.zeros((d, LANES), F32).at[:, :N_EXPERTS].set(w_re).at[:, N_EXPERTS:N_EXPERTS + N_GROUPS].set(w_rg)
    br = jnp.zeros((1, LANES), F32).at[0, :N_EXPERTS].set(b_re).at[0, N_EXPERTS:N_EXPERTS + N_GROUPS].set(b_rg)
    wr_hi = wr.astype(BF16)
    wr = jnp.stack([wr_hi, (wr - wr_hi.astype(F32)).astype(BF16)])
    return (w_in_p, w_gate.astype(BF16), w_branch.astype(BF16), w_out.astype(BF16), wr, br,
            w1.astype(BF16), w3.astype(BF16), w2.astype(BF16))


def _split_in(w_in):
    sizes = (256, 64, 64, 256, 64, H_IDX, 256, 256, 256, 512, 512, 512)
    out, start = [], 0
    for n in sizes:
        out.append(w_in[:, start:start + n])
        start += n
    return out


def _from_hm(o_hm):
    b, h, t, dh = o_hm.shape
    return jnp.swapaxes(o_hm, 1, 2).reshape(b, t, h * dh)


def _to_hm(a, dtype):
    return jnp.swapaxes(a, 1, 2).astype(dtype)


def _pad_axis(a, axis, size):
    if a.shape[axis] == size:
        return a
    widths = [(0, 0)] * a.ndim
    widths[axis] = (0, size - a.shape[axis])
    return jnp.pad(a, widths)


def _round_up(n, m):
    return -(-n // m) * m


def _layer(x, mods, norms, lw, consts, cache):
    n1, n2 = norms
    sh1, sc1, g1, sh2, sc2, g2 = mods
    w_in_p, wg, bg, wb, wo, wr, br, w1, w3, w2 = lw
    dsa_bias, band_bias = consts
    b, t, d = x.shape
    tm = min(512, t)

    (h, qa_hm, qi_hm, sm, qb_hm, kb_hm, vb_hm, kb, vb, qc_hm, kc_hm, vc_hm, kcs, vcs) = _proj(
        x, sc1, sh1, n1, w_in_p, tm, C_BAND if cache is None else 0)
    ka, va, ki = sm[..., 0:64], sm[..., 64:128], sm[..., 128:192]
    wi_t = jnp.swapaxes(sm[..., 192:192 + H_IDX], 1, 2) * (H_IDX ** -0.5)

    if cache is None:
        qoff, tq_pad = 0, t
        ka_f, va_f, ki_f = ka, va, ki
        kb_f, vb_f = kb_hm, vb_hm
        kc_f, vc_f = kc_hm, vc_hm
        band_tq, n_invalid = BAND_TQ, C_BAND
        state = (ka, va, ki, kb.reshape(b, t, H_B, HEAD_DIM), vb.reshape(b, t, H_B, HEAD_DIM),
                 kcs.reshape(b, -1, H_C, HEAD_DIM), vcs.reshape(b, -1, H_C, HEAD_DIM))
    else:
        ca_k, ca_v, ca_ki, cb_k, cb_v, cc_k, cc_v = cache
        qoff = ca_k.shape[1]
        tq_pad = _round_up(t, DSA_TQ)
        ka_f = jnp.concatenate([ca_k, ka], axis=1)
        va_f = jnp.concatenate([ca_v, va], axis=1)
        ki_f = jnp.concatenate([ca_ki, ki], axis=1)
        kb_f = jnp.concatenate([_to_hm(cb_k, BF16), kb_hm], axis=2)
        vb_f = jnp.concatenate([_to_hm(cb_v, BF16), vb_hm], axis=2)
        kc_f = jnp.concatenate([_to_hm(cc_k, BF16), kc_hm], axis=2)
        vc_f = jnp.concatenate([_to_hm(cc_v, BF16), vc_hm], axis=2)
        band_tq, n_invalid = CHUNK, 0
        kc_new = kcs.reshape(b, t, H_C, HEAD_DIM)
        vc_new = vcs.reshape(b, t, H_C, HEAD_DIM)
        state = (ka, va, ki, kb.reshape(b, t, H_B, HEAD_DIM), vb.reshape(b, t, H_B, HEAD_DIM),
                 jnp.concatenate([cc_k, kc_new], axis=1)[:, t:],
                 jnp.concatenate([cc_v, vc_new], axis=1)[:, t:])

    l_valid = ka_f.shape[1]
    topk = min(TOPK_MAX, l_valid // 4)
    lp = _round_up(l_valid, DSA_LC1)
    oa_t = _dsa(_pad_axis(qi_hm, 2, tq_pad), _pad_axis(qa_hm, 2, tq_pad), _pad_axis(wi_t, 2, tq_pad),
                _pad_axis(ki_f.astype(BF16), 1, lp), _pad_axis(ka_f.astype(BF16), 1, lp),
                _pad_axis(va_f.astype(BF16), 1, lp), dsa_bias, qoff, l_valid, topk)
    oa = jnp.transpose(oa_t[..., :t], (0, 3, 1, 2)).reshape(b, t, H_A * HEAD_DIM)

    tq_sb = _round_up(t, SB_T)
    lp_b = qoff + tq_sb
    ob_hm = _sb(_pad_axis(qb_hm, 2, tq_sb), _pad_axis(kb_f, 2, lp_b), _pad_axis(vb_f, 2, lp_b), qoff)
    ob = _from_hm(ob_hm[:, :, :t])

    oc = _from_hm(_band(qc_hm, kc_f, vc_f, band_bias[band_tq], band_tq, n_invalid))

    x, h2, gates = _merge(x, h, oa, ob, oc, g1, sc2, sh2, n2, wg, bg, wb, wo, wr, br, min(512, t))
    x = _moe(x, h2, gates, g2, w1, w3, w2, min(512, t))
    return x, state


def kernel(x_prompt, x_sample, c_prompt, c_sample, cache_a_k, cache_a_v, cache_a_kidx, cache_b_k, cache_b_v, cache_c_k, cache_c_v, norm1, norm2, final_norm, w_ada, b_ada, w_in, t5_table, rel_c, w_gate, b_gate, w_branch, w_out, w_rg, b_rg, w_re, b_re, w1, w3, w2):
    depth = norm1.shape[0]
    bp = x_prompt.shape[0]
    mods_all = _ada_mods(jnp.concatenate([c_prompt, c_sample], axis=0), w_ada, b_ada)
    dsa_bias = _dsa_bias_tiles(t5_table)
    xp, xs = x_prompt, x_sample
    st_p, st_s = [], []
    for l in range(depth):
        lw = _prep_layer_weights(w_in[l], w_gate[l], w_branch[l], w_out[l], w_rg[l], b_rg[l],
                                 w_re[l], b_re[l], w1[l], w3[l], w2[l])
        lw = lw[:2] + (b_gate[l],) + lw[2:]
        band_bias = {tq: _band_bias(rel_c[l], tq) for tq in (CHUNK, BAND_TQ)}
        consts = (dsa_bias, band_bias)
        norms = (norm1[l], norm2[l])
        mods = jnp.split(mods_all[l], 6, axis=-1)
        xp, sp = _layer(xp, [m[:bp] for m in mods], norms, lw, consts, None)
        cache = (cache_a_k[l], cache_a_v[l], cache_a_kidx[l], cache_b_k[l], cache_b_v[l],
                 cache_c_k[l], cache_c_v[l])
        xs, ss = _layer(xs, [m[bp:] for m in mods], norms, lw, consts, cache)
        st_p.append(sp)
        st_s.append(ss)
    y_prompt = _final_norm(xp, final_norm, min(512, xp.shape[1]))
    y_sample = _final_norm(xs, final_norm, min(512, xs.shape[1]))
    stack = lambda states, i: jnp.stack([s[i] for s in states], axis=0)
    return ((y_prompt, y_sample) + tuple(stack(st_p, i) for i in range(7))
            + tuple(stack(st_s, i) for i in range(7)))
```

```python
import functools
import math

import jax
import jax.numpy as jnp
from jax import lax
from jax.experimental import pallas as pl
from jax.experimental.pallas import tpu as pltpu

F32 = jnp.float32
BF16 = jnp.bfloat16
HIGHEST = lax.Precision.HIGHEST

CHUNK = 64
HEAD_DIM = 64
D_IDX = 64
H_A = 4
H_IDX = 4
H_B = 4
H_C = 8
C_BAND_CHUNKS = 8
C_BAND = C_BAND_CHUNKS * CHUNK
REL_CLIP = 256
N_BUCKETS = 32
T5_MAX_DIST = 1024
TOPK_MAX = 256
N_GROUPS = 4
EXP_PER_GROUP = 4
N_EXPERTS = N_GROUPS * EXP_PER_GROUP
RMS_EPS = 1e-6

NEG = -1e30
MIN_NORMAL_BITS = 0x00800000
ZERO_KEY_TOP = 1 << 14
INT_MIN = -(2 ** 31)
LANES = 128
VMEM_LIMIT = 56 * 1024 * 1024

DSA_TQ = 128
DSA_LC1 = 512
DSA_LC3 = 256
SB_T = 256
SB_DEAD = -104.0
BAND_TQ = 4 * CHUNK
MOE_EXPERTS_PER_STEP = 4
DSA_NEAR = -(-(DSA_LC3 - 1 + T5_MAX_DIST) // DSA_TQ)

_C_QA, _C_QI, _C_SM, _C_QB, _C_KB, _C_VB, _C_QC, _C_KC, _C_VC, _C_END = (
    0, 256, 512, 768, 1024, 1280, 1536, 2048, 2560, 3072)


def _cparams(sem):
    return pltpu.CompilerParams(dimension_semantics=sem, vmem_limit_bytes=VMEM_LIMIT)


def _nt_dot(a, b):
    return lax.dot_general(a, b, (((1,), (1,)), ((), ())), preferred_element_type=F32)


def _dot(a, b):
    return jnp.dot(a, b, preferred_element_type=F32)


def _col_reduce(x, op):
    r, c = x.shape
    group = 64 if r % 64 == 0 else 8
    if r > group:
        x = op(x.reshape(r // group, group, c), axis=0)
    return op(x, axis=0, keepdims=True)


def _bit_planes(words):
    a = list(words)
    j, m = 16, 0x0000FFFF
    while j:
        k = 0
        while k < 32:
            t = (a[k] ^ lax.shift_right_logical(a[k + j], j)) & m
            a[k] = a[k] ^ t
            a[k + j] = a[k + j] ^ lax.shift_left(t, j)
            k = (k + j + 1) & ~j
        j >>= 1
        m ^= (m << j) & 0xFFFFFFFF
    return [a[31 - b] for b in range(32)]


def _rms_mod(x, g, sc, sh):
    ms = jnp.mean(x * x, axis=-1, keepdims=True)
    return (x * lax.rsqrt(ms + RMS_EPS) * g) * (1.0 + sc) + sh


def _ada_kernel(c_ref, w_ref, b_ref, o_ref):
    c = c_ref[...]
    s = c * jax.nn.sigmoid(c)
    o_ref[0] = jnp.dot(s, w_ref[0], preferred_element_type=F32, precision=HIGHEST) + b_ref[0]


def _ada_mods(c_all, w_ada, b_ada):
    depth, d, e = w_ada.shape
    r = c_all.shape[0]
    tn = 1024
    return pl.pallas_call(
        _ada_kernel,
        grid=(depth, e // tn),
        in_specs=[pl.BlockSpec((r, d), lambda l, j: (0, 0)),
                  pl.BlockSpec((1, d, tn), lambda l, j: (l, 0, j)),
                  pl.BlockSpec((1, 1, tn), lambda l, j: (l, 0, j))],
        out_specs=pl.BlockSpec((1, r, tn), lambda l, j: (l, 0, j)),
        out_shape=jax.ShapeDtypeStruct((depth, r, e), F32),
        compiler_params=_cparams(("arbitrary", "arbitrary")),
        name="ada_mods",
    )(c_all, w_ada, b_ada.reshape(depth, 1, e))


def _proj_kernel(x_ref, sc_ref, sh_ref, g_ref, w_ref, *refs, n_tiles, n_state_tiles, n_alias):
    (h_ref, qa_ref, qi_ref, sm_ref, qb_ref, kbh_ref, vbh_ref, kb_ref, vb_ref,
     qc_ref, kch_ref, vch_ref, kcs_ref, vcs_ref) = refs[n_alias:]
    i = pl.program_id(1)
    h = _rms_mod(x_ref[0], g_ref[...], sc_ref[0], sh_ref[0])
    hb = h.astype(BF16)
    h_ref[0] = hb

    def mm(lo, hi):
        return _dot(hb, w_ref[:, lo:hi])

    def heads(ref, y, n, scale=None):
        for hh in range(n):
            part = y[:, hh * HEAD_DIM:(hh + 1) * HEAD_DIM]
            if scale is not None:
                part = part * scale
            ref[0, hh] = part.astype(BF16)

    qscale = HEAD_DIM ** -0.5
    heads(qa_ref, mm(_C_QA, _C_QI), H_A, qscale)
    heads(qi_ref, mm(_C_QI, _C_SM), H_IDX, D_IDX ** -0.5)
    sm_ref[0] = mm(_C_SM, _C_QB)
    heads(qb_ref, mm(_C_QB, _C_KB), H_B, qscale)
    kb = mm(_C_KB, _C_VB)
    kb_ref[0] = kb
    heads(kbh_ref, kb, H_B)
    vb = mm(_C_VB, _C_QC)
    vb_ref[0] = vb
    heads(vbh_ref, vb, H_B)
    heads(qc_ref, mm(_C_QC, _C_KC), H_C, qscale)
    kc = mm(_C_KC, _C_VC)
    heads(kch_ref, kc, H_C)
    vc = mm(_C_VC, _C_END)
    heads(vch_ref, vc, H_C)

    @pl.when(i >= n_tiles - n_state_tiles)
    def _():
        kcs_ref[0] = kc
        vcs_ref[0] = vc


def _proj(x, sc, sh, g, w_in_p, tm, kpad):
    b, t, d = x.shape
    n_tiles = t // tm
    n_state_tiles = min(C_BAND, t) // tm
    nbuf = n_state_tiles * tm
    assert kpad % tm == 0

    def hm(n, pad=0):
        return (pl.BlockSpec((1, n, tm, HEAD_DIM), lambda bi, i: (bi, 0, i + pad // tm, 0)),
                jax.ShapeDtypeStruct((b, n, pad + t, HEAD_DIM), BF16))

    def tok(width, dtype):
        return (pl.BlockSpec((1, tm, width), lambda bi, i: (bi, i, 0)),
                jax.ShapeDtypeStruct((b, t, width), dtype))

    def state(width):
        return (pl.BlockSpec((1, tm, width),
                             lambda bi, i: (bi, jnp.maximum(i - (n_tiles - n_state_tiles), 0), 0)),
                jax.ShapeDtypeStruct((b, nbuf, width), F32))

    outs = [tok(d, BF16), hm(H_A), hm(H_IDX), tok(256, F32), hm(H_B), hm(H_B), hm(H_B),
            tok(256, F32), tok(256, F32), hm(H_C), hm(H_C, kpad), hm(H_C, kpad), state(512), state(512)]
    vec = pl.BlockSpec((1, 1, d), lambda bi, i: (bi, 0, 0))
    in_specs = [pl.BlockSpec((1, tm, d), lambda bi, i: (bi, i, 0)), vec, vec,
                pl.BlockSpec((1, d), lambda bi, i: (0, 0)),
                pl.BlockSpec((d, _C_END), lambda bi, i: (0, 0))]
    args = [x, sc.reshape(b, 1, d), sh.reshape(b, 1, d), g.reshape(1, d), w_in_p]
    aliases = {}
    if kpad:
        for out_idx in (10, 11):
            aliases[len(args)] = out_idx
            in_specs.append(pl.BlockSpec(memory_space=pl.ANY))
            args.append(jnp.zeros(outs[out_idx][1].shape, BF16))
    return pl.pallas_call(
        functools.partial(_proj_kernel, n_tiles=n_tiles, n_state_tiles=n_state_tiles, n_alias=len(aliases)),
        grid=(b, n_tiles),
        in_specs=in_specs,
        out_specs=[o[0] for o in outs],
        out_shape=[o[1] for o in outs],
        input_output_aliases=aliases,
        compiler_params=_cparams(("arbitrary", "arbitrary")),
        name="proj",
    )(*args)


def _dsa_kernel(qi_ref, qa_ref, wi_ref, ki_ref, ka_ref, vt_ref, bias_ref, o_ref, key_ref, plane_ref,
                *, qoff, l_valid, topk):
    tq, lc1, lc3 = DSA_TQ, DSA_LC1, DSA_LC3
    grp = 256
    n_rows = plane_ref.shape[1]
    i = pl.program_id(1)
    q0 = qoff + i * tq
    lane = lax.broadcasted_iota(jnp.int32, (1, tq), 1)
    qpos = q0 + lane
    lim = jnp.minimum((qpos // CHUNK + 1) * CHUNK, l_valid)
    n_adm = jnp.minimum(((q0 + tq - 1) // CHUNK + 1) * CHUNK, l_valid)
    nch1 = (n_adm + lc1 - 1) // lc1
    qi = qi_ref[0].reshape(H_IDX * tq, D_IDX)
    qa = qa_ref[0].reshape(H_A * tq, HEAD_DIM)
    wi = wi_ref[0]

    def score_chunk(c, carry):
        s0 = pl.multiple_of(c * lc1, lc1)
        s_all = _nt_dot(ki_ref[0, pl.ds(s0, lc1), :], qi)
        sc = None
        for h in range(H_IDX):
            term = wi[h:h + 1, :] * jnp.maximum(s_all[:, h * tq:(h + 1) * tq], 0.0)
            sc = term if sc is None else sc + term
        bits = lax.bitcast_convert_type(sc, jnp.int32)
        key = bits ^ (lax.shift_right_arithmetic(bits, 31) & 0x7FFFFFFF)
        sidx = s0 + lax.broadcasted_iota(jnp.int32, (lc1, tq), 0)
        tiny = (bits & 0x7FFFFFFF) < MIN_NORMAL_BITS
        key = jnp.where(tiny, jnp.where(bits < 0, -1 - sidx, ZERO_KEY_TOP - sidx), key)
        key = jnp.where(sidx < lim, key, INT_MIN)
        key_ref[pl.ds(s0, lc1), :] = key
        ukey = (key ^ INT_MIN).reshape(lc1 // 8, 8, tq)
        for g in range(lc1 // grp):
            planes = _bit_planes([ukey[g * 32 + r] for r in range(32)])
            row0 = pl.multiple_of((c * (lc1 // grp) + g) * 8, 8)
            for bit in range(32):
                plane_ref[bit, pl.ds(row0, 8), :] = planes[bit]
        return carry

    lax.fori_loop(0, nch1, score_chunk, 0)

    def count(*preds):
        def body(c, accs):
            s0 = pl.multiple_of(c * lc1, lc1)
            kk = key_ref[pl.ds(s0, lc1), :]
            sidx = s0 + lax.broadcasted_iota(jnp.int32, (lc1, tq), 0)
            return tuple(acc + jnp.sum(pred(kk, sidx).reshape(lc1 // 64, 64, tq), axis=0)
                         for acc, pred in zip(accs, preds))
        accs = lax.fori_loop(0, nch1, body, tuple(jnp.zeros((64, tq), F32) for _ in preds))
        return tuple(jnp.sum(acc, axis=0, keepdims=True) for acc in accs)

    kf = float(topk)
    row = lax.broadcasted_iota(jnp.int32, (n_rows, tq), 0)
    eq0 = jnp.where(row < nch1 * (lc1 // grp) * 8, -1, 0)

    def bit_step(it, carry):
        eq, n_gt, thr_u = carry
        bit = 31 - it
        w = eq & plane_ref[bit]
        n_set = _col_reduce(lax.population_count(w).astype(F32), jnp.sum)
        take = n_gt + n_set >= kf
        eq = jnp.where(take, w, eq ^ w)
        n_gt = jnp.where(take, n_gt, n_gt + n_set)
        thr_u = jnp.where(take, thr_u | lax.shift_left(jnp.int32(1), bit), thr_u)
        return eq, n_gt, thr_u

    eq, n_gt, thr_u = lax.fori_loop(
        0, 32, bit_step, (eq0, jnp.zeros((1, tq), F32), jnp.zeros((1, tq), jnp.int32)))
    n_eq = _col_reduce(lax.population_count(eq).astype(F32), jnp.sum)
    at_floor = (thr_u ^ INT_MIN) == INT_MIN
    thr = jnp.maximum(thr_u ^ INT_MIN, INT_MIN + 1)
    n_eq = jnp.where(at_floor, 0.0, n_eq)

    room = kf - n_gt
    idx_bits = 14
    assert l_valid < (1 << idx_bits) <= ZERO_KEY_TOP

    def tie_search():
        def step(it, end):
            cand = end + lax.shift_left(jnp.int32(1), idx_bits - 1 - it)
            n, = count(lambda kk, sidx: jnp.where(kk == thr, jnp.where(sidx < cand, 1.0, 0.0), 0.0))
            return jnp.where(n <= room, cand, end)
        return lax.fori_loop(0, idx_bits, step, jnp.zeros((1, tq), jnp.int32))

    tie_end = lax.cond(jnp.max(n_eq - room) > 0.0, tie_search,
                       lambda: jnp.full((1, tq), 1 << idx_bits, jnp.int32))

    n_sub = lc1 // lc3

    def logits_chunk(c):
        s0 = pl.multiple_of(c * lc3, lc3)
        lg = _nt_dot(ka_ref[0, pl.ds(s0, lc3), :], qa)
        kk = key_ref[pl.ds(s0, lc3), :]
        sidx = s0 + lax.broadcasted_iota(jnp.int32, (lc3, tq), 0)
        madd = jnp.where(kk > thr, 0.0,
                         jnp.where(kk == thr, jnp.where(sidx < tie_end, 0.0, NEG), NEG))
        d = jnp.clip((q0 - s0) // tq, 0, DSA_NEAR)
        return jnp.concatenate(
            [lg[:, h * tq:(h + 1) * tq] + bias_ref[d, h] + madd for h in range(H_A)], axis=1)

    def attend_step(c1, carry):
        m, l, acc = carry
        lgs = [logits_chunk(c1 * n_sub + sub) for sub in range(n_sub)]
        m_new = m
        for lgb in lgs:
            m_new = jnp.maximum(m_new, _col_reduce(lgb, jnp.max))
        alpha = jnp.exp(m - m_new)
        l = l * alpha
        acc = acc * alpha
        for sub, lgb in enumerate(lgs):
            p = jnp.exp(lgb - m_new)
            l = l + _col_reduce(p, jnp.sum)
            acc = acc + _dot(vt_ref[0, c1 * n_sub + sub], p.astype(BF16))
        return m_new, l, acc

    m0 = jnp.full((1, H_A * tq), NEG, F32)
    l0 = jnp.zeros((1, H_A * tq), F32)
    a0 = jnp.zeros((HEAD_DIM, H_A * tq), F32)
    _, l, acc = lax.fori_loop(0, nch1, attend_step, (m0, l0, a0))
    out = acc / l
    for h in range(H_A):
        o_ref[0, h] = out[:, h * tq:(h + 1) * tq].astype(BF16)


def _t5_bucket(rel):
    nb = N_BUCKETS // 2
    max_exact = nb // 2
    ret = jnp.where(rel > 0, nb, 0).astype(jnp.int32)
    n = jnp.abs(rel)
    n_f = jnp.maximum(n, 1).astype(F32)
    large = max_exact + (jnp.log(n_f / max_exact) / math.log(T5_MAX_DIST / max_exact)
                         * (nb - max_exact)).astype(jnp.int32)
    large = jnp.minimum(large, nb - 1)
    return ret + jnp.where(n < max_exact, n, large).astype(jnp.int32)


def _toeplitz(vec, n, m):
    length = n + m - 1
    lead = vec.shape[:-1]
    flat = jnp.tile(vec, (1,) * len(lead) + (n + 1,))[..., :n * (length + 1)]
    hankel = flat.reshape(lead + (n, length + 1))[..., :m]
    return hankel[..., ::-1]


def _dsa_bias_tiles(t5_table):
    d = jnp.arange(DSA_NEAR + 1)[:, None]
    k = jnp.arange(DSA_LC3 + DSA_TQ - 1)[None, :]
    rel = k - (DSA_TQ - 1) - DSA_TQ * d
    rel = jnp.where(d == DSA_NEAR, -T5_MAX_DIST, rel)
    vec = jnp.moveaxis(t5_table.astype(F32)[_t5_bucket(rel)], -1, 1)
    return _toeplitz(vec, DSA_LC3, DSA_TQ)


def _dsa(qi_hm, qa_hm, wi_t, ki, ka, va, bias_tiles, qoff, l_valid, topk):
    b, _, tq_all, _ = qi_hm.shape
    lp = ki.shape[1]
    nq = tq_all // DSA_TQ
    nc3 = lp // DSA_LC3
    vt = jnp.swapaxes(va.reshape(b, nc3, DSA_LC3, HEAD_DIM), 2, 3)
    qspec = pl.BlockSpec((1, H_A, DSA_TQ, HEAD_DIM), lambda bi, i: (bi, 0, i, 0))
    kspec = pl.BlockSpec((1, lp, HEAD_DIM), lambda bi, i: (bi, 0, 0))
    return pl.pallas_call(
        functools.partial(_dsa_kernel, qoff=qoff, l_valid=l_valid, topk=topk),
        grid=(b, nq),
        in_specs=[qspec, qspec,
                  pl.BlockSpec((1, H_IDX, DSA_TQ), lambda bi, i: (bi, 0, i)),
                  kspec, kspec,
                  pl.BlockSpec((1, nc3, HEAD_DIM, DSA_LC3), lambda bi, i: (bi, 0, 0, 0)),
                  pl.BlockSpec(bias_tiles.shape, lambda bi, i: (0, 0, 0, 0))],
        out_specs=pl.BlockSpec((1, H_A, HEAD_DIM, DSA_TQ), lambda bi, i: (bi, 0, 0, i)),
        out_shape=jax.ShapeDtypeStruct((b, H_A, HEAD_DIM, tq_all), BF16),
        scratch_shapes=[pltpu.VMEM((lp, DSA_TQ), jnp.int32), pltpu.VMEM((32, lp // 32, DSA_TQ), jnp.int32)],
        compiler_params=_cparams(("arbitrary", "arbitrary")),
        name="dsa",
    )(qi_hm, qa_hm, wi_t, ki, ka, vt, bias_tiles)


def _sb_kernel(q_ref, k_ref, v_ref, u_ref, o_ref, *, qoff):
    t = SB_T
    n_heads = q_ref.shape[1]
    i = pl.program_id(1)
    u = u_ref[...]
    kb_diag = qoff // t + i
    row = lax.broadcasted_iota(jnp.int32, (t, t), 0)
    col = lax.broadcasted_iota(jnp.int32, (t, t), 1)
    causal = col < row

    def block(kb, carries, accs, diag):
        s0 = pl.multiple_of(kb * t, t)
        new_c, new_a = [], []
        for h in range(n_heads):
            k = k_ref[0, h, pl.ds(s0, t), :]
            v = v_ref[0, h, pl.ds(s0, t), :]
            z = _nt_dot(q_ref[0, h], k)
            sp = jnp.maximum(z, 0.0) + jnp.log1p(jnp.exp(-jnp.abs(z)))
            lm = -sp
            if diag:
                lm = jnp.where(causal, lm, 0.0)
            hi = lm.astype(BF16)
            lo = (lm - hi.astype(F32)).astype(BF16)
            ext = _dot(hi, u) + _dot(lo, u)
            e = z - sp + ext[:, :t]
            a = jnp.exp(jnp.concatenate(
                [e[:, j * LANES:(j + 1) * LANES] + carries[h] for j in range(t // LANES)], axis=1))
            if diag:
                a = jnp.where(causal, a, 0.0)
            new_a.append(accs[h] + _dot(a.astype(BF16), v))
            new_c.append(carries[h] + ext[:, t:])
        return tuple(new_c), tuple(new_a)

    zeros_c = tuple(jnp.zeros((t, LANES), F32) for _ in range(n_heads))
    zeros_a = tuple(jnp.zeros((t, HEAD_DIM), F32) for _ in range(n_heads))
    carries, accs = block(kb_diag, zeros_c, zeros_a, True)

    def worst(cs):
        m = cs[0]
        for c in cs[1:]:
            m = jnp.maximum(m, c)
        return jnp.max(m)

    def cond(st):
        return jnp.logical_and(st[0] < kb_diag, st[1] > SB_DEAD)

    def body(st):
        j, _, cs, acs = st
        cs, acs = block(kb_diag - 1 - j, cs, acs, False)
        return j + 1, worst(cs), cs, acs

    _, _, _, accs = lax.while_loop(cond, body, (jnp.int32(0), worst(carries), carries, accs))
    for h in range(n_heads):
        o_ref[0, h] = accs[h].astype(BF16)


def _sb(q_hm, k_hm, v_hm, qoff):
    b, h, tq_all, _ = q_hm.shape
    lp = k_hm.shape[2]
    t = SB_T
    assert qoff % t == 0 and tq_all % t == 0 and lp >= qoff + tq_all
    jj = lax.broadcasted_iota(jnp.int32, (t, t + LANES), 0)
    ss = lax.broadcasted_iota(jnp.int32, (t, t + LANES), 1)
    u = jnp.logical_or(jj > ss, ss >= t).astype(BF16)
    kspec = pl.BlockSpec((1, h, lp, HEAD_DIM), lambda bi, i: (bi, 0, 0, 0))
    qspec = pl.BlockSpec((1, h, t, HEAD_DIM), lambda bi, i: (bi, 0, i, 0))
    return pl.pallas_call(
        functools.partial(_sb_kernel, qoff=qoff),
        grid=(b, tq_all // t),
        in_specs=[qspec, kspec, kspec, pl.BlockSpec((t, t + LANES), lambda bi, i: (0, 0))],
        out_specs=qspec,
        out_shape=jax.ShapeDtypeStruct((b, h, tq_all, HEAD_DIM), BF16),
        compiler_params=_cparams(("arbitrary", "arbitrary")),
        name="sb",
    )(q_hm, k_hm, v_hm, u)


def _band_kernel(q_ref, k_ref, v_ref, bm_ref, o_ref, *, tq, w, n_invalid):
    i = pl.program_id(1)
    if n_invalid:
        col = lax.broadcasted_iota(jnp.int32, (1, w), 1)
        valid = jnp.where(i * tq + col >= n_invalid, 0.0, NEG)
    for h in range(q_ref.shape[1]):
        lg = _nt_dot(q_ref[0, h], k_ref[0, h]) + bm_ref[h]
        if n_invalid:
            lg = lg + valid
        m = jnp.max(lg, axis=1, keepdims=True)
        p = jnp.exp(lg - m)
        l = jnp.sum(p, axis=1, keepdims=True)
        o_ref[0, h] = (_dot(p.astype(BF16), v_ref[0, h]) / l).astype(BF16)


def _band_bias(rel_table, tq):
    w = C_BAND + tq
    t = jnp.arange(tq)[:, None]
    c = jnp.arange(w)[None, :] - C_BAND
    k = jnp.arange(tq + w - 1)
    rel = jnp.clip(w - 1 - C_BAND - k, -REL_CLIP, REL_CLIP) + REL_CLIP
    bias = _toeplitz(rel_table.astype(F32)[rel].T, tq, w)
    qc = t // CHUNK
    kc = jnp.floor_divide(c, CHUNK)
    mask = (kc <= qc) & (kc >= qc - C_BAND_CHUNKS)
    return jnp.where(mask[None], bias, NEG)


def _band(q_hm, k_ext, v_ext, bm, tq, n_invalid):
    b, h, t, _ = q_hm.shape
    w = C_BAND + tq
    kspec = pl.BlockSpec((pl.Element(1), pl.Element(h), pl.Element(w), pl.Element(HEAD_DIM)),
                         lambda bi, i: (bi, 0, i * tq, 0))
    qspec = pl.BlockSpec((1, h, tq, HEAD_DIM), lambda bi, i: (bi, 0, i, 0))
    return pl.pallas_call(
        functools.partial(_band_kernel, tq=tq, w=w, n_invalid=n_invalid),
        grid=(b, t // tq),
        in_specs=[qspec, kspec, kspec, pl.BlockSpec((h, tq, w), lambda bi, i: (0, 0, 0))],
        out_specs=qspec,
        out_shape=jax.ShapeDtypeStruct((b, h, t, HEAD_DIM), BF16),
        compiler_params=_cparams(("arbitrary", "arbitrary")),
        name="band",
    )(q_hm, k_ext, v_ext, bm)


def _merge_kernel(x_ref, h_ref, oa_ref, ob_ref, oc_ref, g1_ref, sc2_ref, sh2_ref, n2_ref,
                  wg_ref, bg_ref, wb_ref, wo_ref, wr_ref, br_ref,
                  xo_ref, h2_ref, gates_ref):
    d = x_ref.shape[-1]
    hb = h_ref[0]
    mix = None
    off = 0
    for j, o_ref in enumerate((oa_ref, ob_ref, oc_ref)):
        width = o_ref.shape[-1]
        y = _dot(o_ref[0], wb_ref[off:off + width, :])
        off += width
        g = jax.nn.sigmoid(_dot(hb, wg_ref[:, j * d:(j + 1) * d]) + bg_ref[:, j * d:(j + 1) * d])
        mix = g * y if mix is None else mix + g * y
    x = x_ref[0] + g1_ref[0] * _dot(mix.astype(BF16), wo_ref[...])
    xo_ref[0] = x
    h2 = _rms_mod(x, n2_ref[...], sc2_ref[0], sh2_ref[0])
    h2_ref[0] = h2.astype(BF16)

    h2_hi = h2.astype(BF16)
    h2_lo = (h2 - h2_hi.astype(F32)).astype(BF16)
    wr_hi, wr_lo = wr_ref[0], wr_ref[1]
    lr = _dot(h2_hi, wr_hi) + (_dot(h2_lo, wr_hi) + _dot(h2_hi, wr_lo)) + br_ref[...]
    tm = lr.shape[0]
    lane = lax.broadcasted_iota(jnp.int32, (tm, LANES), 1)
    lanef = lane.astype(F32)
    ninf = -jnp.inf
    is_expert = lane < N_EXPERTS
    lg = jnp.where(is_expert, ninf, jnp.where(lane < N_EXPERTS + N_GROUPS, lr, ninf))
    eg = jnp.exp(lg - jnp.max(lg, axis=1, keepdims=True))
    pg = eg / jnp.sum(eg, axis=1, keepdims=True)
    pg_top = jnp.max(pg, axis=1, keepdims=True)
    g_top = jnp.min(jnp.where(pg == pg_top, lanef, float(LANES)), axis=1, keepdims=True) - float(N_EXPERTS)
    in_group = (lane // EXP_PER_GROUP).astype(F32) == g_top
    le = jnp.where(is_expert, jnp.where(in_group, lr, ninf), ninf)
    m1 = jnp.max(le, axis=1, keepdims=True)
    i1 = jnp.min(jnp.where(le == m1, lanef, float(LANES)), axis=1, keepdims=True)
    le2 = jnp.where(lanef == i1, ninf, le)
    m2 = jnp.max(le2, axis=1, keepdims=True)
    i2 = jnp.min(jnp.where(le2 == m2, lanef, float(LANES)), axis=1, keepdims=True)
    e2 = jnp.exp(m2 - m1)
    den = 1.0 + e2
    gates_ref[0] = (jnp.where(lanef == i1, pg_top / den, 0.0)
                    + jnp.where(lanef == i2, pg_top * e2 / den, 0.0))


def _merge(x, h, oa, ob, oc, g1, sc2, sh2, n2, wg, bg, wb, wo, wr, br, tm):
    b, t, d = x.shape

    def tok(width):
        return pl.BlockSpec((1, tm, width), lambda bi, i: (bi, i, 0))

    vec = pl.BlockSpec((1, 1, d), lambda bi, i: (bi, 0, 0))

    def full(a):
        return pl.BlockSpec(a.shape, lambda bi, i: (0,) * a.ndim)

    n2r, bgr = n2.reshape(1, d), bg.reshape(1, 3 * d)
    return pl.pallas_call(
        _merge_kernel,
        grid=(b, t // tm),
        in_specs=[tok(d), tok(d), tok(oa.shape[-1]), tok(ob.shape[-1]), tok(oc.shape[-1]),
                  vec, vec, vec, full(n2r), full(wg), full(bgr), full(wb), full(wo), full(wr), full(br)],
        out_specs=[tok(d), tok(d), tok(LANES)],
        out_shape=[jax.ShapeDtypeStruct((b, t, d), F32), jax.ShapeDtypeStruct((b, t, d), BF16),
                   jax.ShapeDtypeStruct((b, t, LANES), F32)],
        compiler_params=_cparams(("arbitrary", "arbitrary")),
        name="merge",
    )(x, h, oa, ob, oc, g1.reshape(b, 1, d), sc2.reshape(b, 1, d), sh2.reshape(b, 1, d),
      n2r, wg, bgr, wb, wo, wr, br)


def _moe_kernel(x_ref, h2_ref, gates_ref, g2_ref, w1_ref, w3_ref, w2_ref, o_ref, acc_ref):
    step = pl.program_id(2)
    n_per = w1_ref.shape[0]

    @pl.when(step == 0)
    def _():
        acc_ref[...] = jnp.zeros_like(acc_ref)

    hb = h2_ref[0]
    gates = gates_ref[0]
    lane = lax.broadcasted_iota(jnp.int32, gates.shape, 1)
    total = None
    for j in range(n_per):
        a = _dot(hb, w1_ref[j])
        bb = _dot(hb, w3_ref[j])
        u = (a * jax.nn.sigmoid(a)) * bb
        out = _dot(u.astype(BF16), w2_ref[j])
        ge = jnp.sum(jnp.where(lane == step * n_per + j, gates, 0.0), axis=1, keepdims=True)
        total = ge * out if total is None else total + ge * out
    acc_ref[...] += total

    @pl.when(step == pl.num_programs(2) - 1)
    def _():
        o_ref[0] = x_ref[0] + g2_ref[0] * acc_ref[...]


def _moe(x, h2, gates, g2, w1, w3, w2, tm):
    b, t, d = x.shape
    ne, _, f = w1.shape
    n_per = MOE_EXPERTS_PER_STEP
    tok = lambda width: pl.BlockSpec((1, tm, width), lambda bi, i, e: (bi, i, 0))
    return pl.pallas_call(
        _moe_kernel,
        grid=(b, t // tm, ne // n_per),
        in_specs=[tok(d), tok(d), tok(LANES),
                  pl.BlockSpec((1, 1, d), lambda bi, i, e: (bi, 0, 0)),
                  pl.BlockSpec((n_per, d, f), lambda bi, i, e: (e, 0, 0)),
                  pl.BlockSpec((n_per, d, f), lambda bi, i, e: (e, 0, 0)),
                  pl.BlockSpec((n_per, f, d), lambda bi, i, e: (e, 0, 0))],
        out_specs=tok(d),
        out_shape=jax.ShapeDtypeStruct((b, t, d), F32),
        scratch_shapes=[pltpu.VMEM((tm, d), F32)],
        compiler_params=_cparams(("arbitrary", "arbitrary", "arbitrary")),
        name="moe",
    )(x, h2, gates, g2.reshape(b, 1, d), w1, w3, w2)


def _final_norm_kernel(x_ref, g_ref, o_ref):
    x = x_ref[0]
    ms = jnp.mean(x * x, axis=-1, keepdims=True)
    o_ref[0] = x * lax.rsqrt(ms + RMS_EPS) * g_ref[...]


def _final_norm(x, g, tm):
    b, t, d = x.shape
    spec = pl.BlockSpec((1, tm, d), lambda bi, i: (bi, i, 0))
    return pl.pallas_call(
        _final_norm_kernel,
        grid=(b, t // tm),
        in_specs=[spec, pl.BlockSpec((1, d), lambda bi, i: (0, 0))],
        out_specs=spec,
        out_shape=jax.ShapeDtypeStruct((b, t, d), F32),
        compiler_params=_cparams(("arbitrary", "arbitrary")),
        name="final_norm",
    )(x, g.reshape(1, d))


def _prep_layer_weights(w_in, w_gate, w_branch, w_out, w_rg, b_rg, w_re, b_re, w1, w3, w2):
    d = w_in.shape[0]
    qa, ka, va, qi, ki, wi, qb, kb, vb, qc, kc, vc = _split_in(w_in)
    pad = jnp.zeros((d, 256 - (64 * 3 + H_IDX)), w_in.dtype)
    w_in_p = jnp.concatenate([qa, qi, ka, va, ki, wi, pad, qb, kb, vb, qc, kc, vc], axis=1).astype(BF16)
    wr = jnp.zeros((d, LANES), F32).at[:, :N_EXPERTS].set(w_re).at[:, N_EXPERTS:N_EXPERTS + N_GROUPS].set(w_rg)
    br = jnp.zeros((1, LANES), F32).at[0, :N_EXPERTS].set(b_re).at[0, N_EXPERTS:N_EXPERTS + N_GROUPS].set(b_rg)
    wr_hi = wr.astype(BF16)
    wr = jnp.stack([wr_hi, (wr - wr_hi.astype(F32)).astype(BF16)])
    return (w_in_p, w_gate.astype(BF16), w_branch.astype(BF16), w_out.astype(BF16), wr, br,
            w1.astype(BF16), w3.astype(BF16), w2.astype(BF16))


def _split_in(w_in):
    sizes = (256, 64, 64, 256, 64, H_IDX, 256, 256, 256, 512, 512, 512)
    out, start = [], 0
    for n in sizes:
        out.append(w_in[:, start:start + n])
        start += n
    return out


def _from_hm(o_hm):
    b, h, t, dh = o_hm.shape
    return jnp.swapaxes(o_hm, 1, 2).reshape(b, t, h * dh)


def _to_hm(a, dtype):
    return jnp.swapaxes(a, 1, 2).astype(dtype)


def _pad_axis(a, axis, size):
    if a.shape[axis] == size:
        return a
    widths = [(0, 0)] * a.ndim
    widths[axis] = (0, size - a.shape[axis])
    return jnp.pad(a, widths)


def _round_up(n, m):
    return -(-n // m) * m


def _layer(x, mods, norms, lw, consts, cache):
    n1, n2 = norms
    sh1, sc1, g1, sh2, sc2, g2 = mods
    w_in_p, wg, bg, wb, wo, wr, br, w1, w3, w2 = lw
    dsa_bias, band_bias = consts
    b, t, d = x.shape
    tm = min(512, t)

    (h, qa_hm, qi_hm, sm, qb_hm, kb_hm, vb_hm, kb, vb, qc_hm, kc_hm, vc_hm, kcs, vcs) = _proj(
        x, sc1, sh1, n1, w_in_p, tm, C_BAND if cache is None else 0)
    ka, va, ki = sm[..., 0:64], sm[..., 64:128], sm[..., 128:192]
    wi_t = jnp.swapaxes(sm[..., 192:192 + H_IDX], 1, 2) * (H_IDX ** -0.5)

    if cache is None:
        qoff, tq_pad = 0, t
        ka_f, va_f, ki_f = ka, va, ki
        kb_f, vb_f = kb_hm, vb_hm
        kc_f, vc_f = kc_hm, vc_hm
        band_tq, n_invalid = BAND_TQ, C_BAND
        state = (ka, va, ki, kb.reshape(b, t, H_B, HEAD_DIM), vb.reshape(b, t, H_B, HEAD_DIM),
                 kcs.reshape(b, -1, H_C, HEAD_DIM), vcs.reshape(b, -1, H_C, HEAD_DIM))
    else:
        ca_k, ca_v, ca_ki, cb_k, cb_v, cc_k, cc_v = cache
        qoff = ca_k.shape[1]
        tq_pad = _round_up(t, DSA_TQ)
        ka_f = jnp.concatenate([ca_k, ka], axis=1)
        va_f = jnp.concatenate([ca_v, va], axis=1)
        ki_f = jnp.concatenate([ca_ki, ki], axis=1)
        kb_f = jnp.concatenate([_to_hm(cb_k, BF16), kb_hm], axis=2)
        vb_f = jnp.concatenate([_to_hm(cb_v, BF16), vb_hm], axis=2)
        kc_f = jnp.concatenate([_to_hm(cc_k, BF16), kc_hm], axis=2)
        vc_f = jnp.concatenate([_to_hm(cc_v, BF16), vc_hm], axis=2)
        band_tq, n_invalid = CHUNK, 0
        kc_new = kcs.reshape(b, t, H_C, HEAD_DIM)
        vc_new = vcs.reshape(b, t, H_C, HEAD_DIM)
        state = (ka, va, ki, kb.reshape(b, t, H_B, HEAD_DIM), vb.reshape(b, t, H_B, HEAD_DIM),
                 jnp.concatenate([cc_k, kc_new], axis=1)[:, t:],
                 jnp.concatenate([cc_v, vc_new], axis=1)[:, t:])

    l_valid = ka_f.shape[1]
    topk = min(TOPK_MAX, l_valid // 4)
    lp = _round_up(l_valid, DSA_LC1)
    oa_t = _dsa(_pad_axis(qi_hm, 2, tq_pad), _pad_axis(qa_hm, 2, tq_pad), _pad_axis(wi_t, 2, tq_pad),
                _pad_axis(ki_f.astype(BF16), 1, lp), _pad_axis(ka_f.astype(BF16), 1, lp),
                _pad_axis(va_f.astype(BF16), 1, lp), dsa_bias, qoff, l_valid, topk)
    oa = jnp.transpose(oa_t[..., :t], (0, 3, 1, 2)).reshape(b, t, H_A * HEAD_DIM)

    tq_sb = _round_up(t, SB_T)
    lp_b = qoff + tq_sb
    ob_hm = _sb(_pad_axis(qb_hm, 2, tq_sb), _pad_axis(kb_f, 2, lp_b), _pad_axis(vb_f, 2, lp_b), qoff)
    ob = _from_hm(ob_hm[:, :, :t])

    oc = _from_hm(_band(qc_hm, kc_f, vc_f, band_bias[band_tq], band_tq, n_invalid))

    x, h2, gates = _merge(x, h, oa, ob, oc, g1, sc2, sh2, n2, wg, bg, wb, wo, wr, br, min(512, t))
    x = _moe(x, h2, gates, g2, w1, w3, w2, min(512, t))
    return x, state


def kernel(x_prompt, x_sample, c_prompt, c_sample, cache_a_k, cache_a_v, cache_a_kidx, cache_b_k, cache_b_v, cache_c_k, cache_c_v, norm1, norm2, final_norm, w_ada, b_ada, w_in, t5_table, rel_c, w_gate, b_gate, w_branch, w_out, w_rg, b_rg, w_re, b_re, w1, w3, w2):
    depth = norm1.shape[0]
    bp = x_prompt.shape[0]
    mods_all = _ada_mods(jnp.concatenate([c_prompt, c_sample], axis=0), w_ada, b_ada)
    dsa_bias = _dsa_bias_tiles(t5_table)
    xp, xs = x_prompt, x_sample
    st_p, st_s = [], []
    for l in range(depth):
        lw = _prep_layer_weights(w_in[l], w_gate[l], w_branch[l], w_out[l], w_rg[l], b_rg[l],
                                 w_re[l], b_re[l], w1[l], w3[l], w2[l])
        lw = lw[:2] + (b_gate[l],) + lw[2:]
        band_bias = {tq: _band_bias(rel_c[l], tq) for tq in (CHUNK, BAND_TQ)}
        consts = (dsa_bias, band_bias)
        norms = (norm1[l], norm2[l])
        mods = jnp.split(mods_all[l], 6, axis=-1)
        xp, sp = _layer(xp, [m[:bp] for m in mods], norms, lw, consts, None)
        cache = (cache_a_k[l], cache_a_v[l], cache_a_kidx[l], cache_b_k[l], cache_b_v[l],
                 cache_c_k[l], cache_c_v[l])
        xs, ss = _layer(xs, [m[bp:] for m in mods], norms, lw, consts, cache)
        st_p.append(sp)
        st_s.append(ss)
    y_prompt = _final_norm(xp, final_norm, min(512, xp.shape[1]))
    y_sample = _final_norm(xs, final_norm, min(512, xs.shape[1]))
    stack = lambda states, i: jnp.stack([s[i] for s in states], axis=0)
    return ((y_prompt, y_sample) + tuple(stack(st_p, i) for i in range(7))
            + tuple(stack(st_s, i) for i in range(7)))
```

```python
import functools
import math

import jax
import jax.numpy as jnp
from jax import lax
from jax.experimental import pallas as pl
from jax.experimental.pallas import tpu as pltpu

F32 = jnp.float32
BF16 = jnp.bfloat16
HIGHEST = lax.Precision.HIGHEST

CHUNK = 64
HEAD_DIM = 64
D_IDX = 64
H_A = 4
H_IDX = 4
H_B = 4
H_C = 8
C_BAND_CHUNKS = 8
C_BAND = C_BAND_CHUNKS * CHUNK
REL_CLIP = 256
N_BUCKETS = 32
T5_MAX_DIST = 1024
TOPK_MAX = 256
N_GROUPS = 4
EXP_PER_GROUP = 4
N_EXPERTS = N_GROUPS * EXP_PER_GROUP
RMS_EPS = 1e-6

NEG = -1e30
MIN_NORMAL_BITS = 0x00800000
ZERO_KEY_TOP = 1 << 14
INT_MIN = -(2 ** 31)
LANES = 128
VMEM_LIMIT = 56 * 1024 * 1024

DSA_TQ = 128
DSA_LC1 = 512
DSA_LC3 = 256
SB_T = 256
SB_DEAD = -104.0
BAND_TQ = 4 * CHUNK
MOE_EXPERTS_PER_STEP = 4
DSA_NEAR = -(-(DSA_LC3 - 1 + T5_MAX_DIST) // DSA_TQ)

_C_QA, _C_QI, _C_SM, _C_QB, _C_KB, _C_VB, _C_QC, _C_KC, _C_VC, _C_END = (
    0, 256, 512, 768, 1024, 1280, 1536, 2048, 2560, 3072)


def _cparams(sem):
    return pltpu.CompilerParams(dimension_semantics=sem, vmem_limit_bytes=VMEM_LIMIT)


def _nt_dot(a, b):
    return lax.dot_general(a, b, (((1,), (1,)), ((), ())), preferred_element_type=F32)


def _dot(a, b):
    return jnp.dot(a, b, preferred_element_type=F32)


def _col_reduce(x, op):
    r, c = x.shape
    group = 64 if r % 64 == 0 else 8
    if r > group:
        x = op(x.reshape(r // group, group, c), axis=0)
    return op(x, axis=0, keepdims=True)


def _bit_planes(words):
    a = list(words)
    j, m = 16, 0x0000FFFF
    while j:
        k = 0
        while k < 32:
            t = (a[k] ^ lax.shift_right_logical(a[k + j], j)) & m
            a[k] = a[k] ^ t
            a[k + j] = a[k + j] ^ lax.shift_left(t, j)
            k = (k + j + 1) & ~j
        j >>= 1
        m ^= (m << j) & 0xFFFFFFFF
    return [a[31 - b] for b in range(32)]


def _rms_mod(x, g, sc, sh):
    ms = jnp.mean(x * x, axis=-1, keepdims=True)
    return (x * lax.rsqrt(ms + RMS_EPS) * g) * (1.0 + sc) + sh


def _ada_kernel(c_ref, w_ref, b_ref, o_ref):
    c = c_ref[...]
    s = c * jax.nn.sigmoid(c)
    o_ref[0] = jnp.dot(s, w_ref[0], preferred_element_type=F32, precision=HIGHEST) + b_ref[0]


def _ada_mods(c_all, w_ada, b_ada):
    depth, d, e = w_ada.shape
    r = c_all.shape[0]
    tn = 1024
    return pl.pallas_call(
        _ada_kernel,
        grid=(depth, e // tn),
        in_specs=[pl.BlockSpec((r, d), lambda l, j: (0, 0)),
                  pl.BlockSpec((1, d, tn), lambda l, j: (l, 0, j)),
                  pl.BlockSpec((1, 1, tn), lambda l, j: (l, 0, j))],
        out_specs=pl.BlockSpec((1, r, tn), lambda l, j: (l, 0, j)),
        out_shape=jax.ShapeDtypeStruct((depth, r, e), F32),
        compiler_params=_cparams(("arbitrary", "arbitrary")),
        name="ada_mods",
    )(c_all, w_ada, b_ada.reshape(depth, 1, e))


def _proj_kernel(x_ref, sc_ref, sh_ref, g_ref, w_ref, *refs, n_tiles, n_state_tiles, n_alias):
    (h_ref, qa_ref, qi_ref, sm_ref, qb_ref, kbh_ref, vbh_ref, kb_ref, vb_ref,
     qc_ref, kch_ref, vch_ref, kcs_ref, vcs_ref) = refs[n_alias:]
    i = pl.program_id(1)
    h = _rms_mod(x_ref[0], g_ref[...], sc_ref[0], sh_ref[0])
    hb = h.astype(BF16)
    h_ref[0] = hb

    def mm(lo, hi):
        return _dot(hb, w_ref[:, lo:hi])

    def heads(ref, y, n, scale=None):
        for hh in range(n):
            part = y[:, hh * HEAD_DIM:(hh + 1) * HEAD_DIM]
            if scale is not None:
                part = part * scale
            ref[0, hh] = part.astype(BF16)

    qscale = HEAD_DIM ** -0.5
    heads(qa_ref, mm(_C_QA, _C_QI), H_A, qscale)
    heads(qi_ref, mm(_C_QI, _C_SM), H_IDX, D_IDX ** -0.5)
    sm_ref[0] = mm(_C_SM, _C_QB)
    heads(qb_ref, mm(_C_QB, _C_KB), H_B, qscale)
    kb = mm(_C_KB, _C_VB)
    kb_ref[0] = kb
    heads(kbh_ref, kb, H_B)
    vb = mm(_C_VB, _C_QC)
    vb_ref[0] = vb
    heads(vbh_ref, vb, H_B)
    heads(qc_ref, mm(_C_QC, _C_KC), H_C, qscale)
    kc = mm(_C_KC, _C_VC)
    heads(kch_ref, kc, H_C)
    vc = mm(_C_VC, _C_END)
    heads(vch_ref, vc, H_C)

    @pl.when(i >= n_tiles - n_state_tiles)
    def _():
        kcs_ref[0] = kc
        vcs_ref[0] = vc


def _proj(x, sc, sh, g, w_in_p, tm, kpad):
    b, t, d = x.shape
    n_tiles = t // tm
    n_state_tiles = min(C_BAND, t) // tm
    nbuf = n_state_tiles * tm
    assert kpad % tm == 0

    def hm(n, pad=0):
        return (pl.BlockSpec((1, n, tm, HEAD_DIM), lambda bi, i: (bi, 0, i + pad // tm, 0)),
                jax.ShapeDtypeStruct((b, n, pad + t, HEAD_DIM), BF16))

    def tok(width, dtype):
        return (pl.BlockSpec((1, tm, width), lambda bi, i: (bi, i, 0)),
                jax.ShapeDtypeStruct((b, t, width), dtype))

    def state(width):
        return (pl.BlockSpec((1, tm, width),
                             lambda bi, i: (bi, jnp.maximum(i - (n_tiles - n_state_tiles), 0), 0)),
                jax.ShapeDtypeStruct((b, nbuf, width), F32))

    outs = [tok(d, BF16), hm(H_A), hm(H_IDX), tok(256, F32), hm(H_B), hm(H_B), hm(H_B),
            tok(256, F32), tok(256, F32), hm(H_C), hm(H_C, kpad), hm(H_C, kpad), state(512), state(512)]
    vec = pl.BlockSpec((1, 1, d), lambda bi, i: (bi, 0, 0))
    in_specs = [pl.BlockSpec((1, tm, d), lambda bi, i: (bi, i, 0)), vec, vec,
                pl.BlockSpec((1, d), lambda bi, i: (0, 0)),
                pl.BlockSpec((d, _C_END), lambda bi, i: (0, 0))]
    args = [x, sc.reshape(b, 1, d), sh.reshape(b, 1, d), g.reshape(1, d), w_in_p]
    aliases = {}
    if kpad:
        for out_idx in (10, 11):
            aliases[len(args)] = out_idx
            in_specs.append(pl.BlockSpec(memory_space=pl.ANY))
            args.append(jnp.zeros(outs[out_idx][1].shape, BF16))
    return pl.pallas_call(
        functools.partial(_proj_kernel, n_tiles=n_tiles, n_state_tiles=n_state_tiles, n_alias=len(aliases)),
        grid=(b, n_tiles),
        in_specs=in_specs,
        out_specs=[o[0] for o in outs],
        out_shape=[o[1] for o in outs],
        input_output_aliases=aliases,
        compiler_params=_cparams(("arbitrary", "arbitrary")),
        name="proj",
    )(*args)


def _dsa_kernel(qi_ref, qa_ref, wi_ref, ki_ref, ka_ref, vt_ref, bias_ref, o_ref, key_ref, plane_ref,
                *, qoff, l_valid, topk):
    tq, lc1, lc3 = DSA_TQ, DSA_LC1, DSA_LC3
    grp = 256
    n_rows = plane_ref.shape[1]
    i = pl.program_id(1)
    q0 = qoff + i * tq
    lane = lax.broadcasted_iota(jnp.int32, (1, tq), 1)
    qpos = q0 + lane
    lim = jnp.minimum((qpos // CHUNK + 1) * CHUNK, l_valid)
    n_adm = jnp.minimum(((q0 + tq - 1) // CHUNK + 1) * CHUNK, l_valid)
    nch1 = (n_adm + lc1 - 1) // lc1
    qi = qi_ref[0].reshape(H_IDX * tq, D_IDX)
    qa = qa_ref[0].reshape(H_A * tq, HEAD_DIM)
    wi = wi_ref[0]

    def score_chunk(c, carry):
        s0 = pl.multiple_of(c * lc1, lc1)
        s_all = _nt_dot(ki_ref[0, pl.ds(s0, lc1), :], qi)
        sc = None
        for h in range(H_IDX):
            term = wi[h:h + 1, :] * jnp.maximum(s_all[:, h * tq:(h + 1) * tq], 0.0)
            sc = term if sc is None else sc + term
        bits = lax.bitcast_convert_type(sc, jnp.int32)
        key = bits ^ (lax.shift_right_arithmetic(bits, 31) & 0x7FFFFFFF)
        sidx = s0 + lax.broadcasted_iota(jnp.int32, (lc1, tq), 0)
        tiny = (bits & 0x7FFFFFFF) < MIN_NORMAL_BITS
        key = jnp.where(tiny, jnp.where(bits < 0, -1 - sidx, ZERO_KEY_TOP - sidx), key)
        key = jnp.where(sidx < lim, key, INT_MIN)
        key_ref[pl.ds(s0, lc1), :] = key
        ukey = (key ^ INT_MIN).reshape(lc1 // 8, 8, tq)
        for g in range(lc1 // grp):
            planes = _bit_planes([ukey[g * 32 + r] for r in range(32)])
            row0 = pl.multiple_of((c * (lc1 // grp) + g) * 8, 8)
            for bit in range(32):
                plane_ref[bit, pl.ds(row0, 8), :] = planes[bit]
        return carry

    lax.fori_loop(0, nch1, score_chunk, 0)

    def count(*preds):
        def body(c, accs):
            s0 = pl.multiple_of(c * lc1, lc1)
            kk = key_ref[pl.ds(s0, lc1), :]
            sidx = s0 + lax.broadcasted_iota(jnp.int32, (lc1, tq), 0)
            return tuple(acc + jnp.sum(pred(kk, sidx).reshape(lc1 // 64, 64, tq), axis=0)
                         for acc, pred in zip(accs, preds))
        accs = lax.fori_loop(0, nch1, body, tuple(jnp.zeros((64, tq), F32) for _ in preds))
        return tuple(jnp.sum(acc, axis=0, keepdims=True) for acc in accs)

    kf = float(topk)
    row = lax.broadcasted_iota(jnp.int32, (n_rows, tq), 0)
    eq0 = jnp.where(row < nch1 * (lc1 // grp) * 8, -1, 0)

    def bit_step(it, carry):
        eq, n_gt, thr_u = carry
        bit = 31 - it
        w = eq & plane_ref[bit]
        n_set = _col_reduce(lax.population_count(w).astype(F32), jnp.sum)
        take = n_gt + n_set >= kf
        eq = jnp.where(take, w, eq ^ w)
        n_gt = jnp.where(take, n_gt, n_gt + n_set)
        thr_u = jnp.where(take, thr_u | lax.shift_left(jnp.int32(1), bit), thr_u)
        return eq, n_gt, thr_u

    eq, n_gt, thr_u = lax.fori_loop(
        0, 32, bit_step, (eq0, jnp.zeros((1, tq), F32), jnp.zeros((1, tq), jnp.int32)))
    n_eq = _col_reduce(lax.population_count(eq).astype(F32), jnp.sum)
    at_floor = (thr_u ^ INT_MIN) == INT_MIN
    thr = jnp.maximum(thr_u ^ INT_MIN, INT_MIN + 1)
    n_eq = jnp.where(at_floor, 0.0, n_eq)

    room = kf - n_gt
    idx_bits = 14
    assert l_valid < (1 << idx_bits) <= ZERO_KEY_TOP

    def tie_search():
        def step(it, end):
            cand = end + lax.shift_left(jnp.int32(1), idx_bits - 1 - it)
            n, = count(lambda kk, sidx: jnp.where(kk == thr, jnp.where(sidx < cand, 1.0, 0.0), 0.0))
            return jnp.where(n <= room, cand, end)
        return lax.fori_loop(0, idx_bits, step, jnp.zeros((1, tq), jnp.int32))

    tie_end = lax.cond(jnp.max(n_eq - room) > 0.0, tie_search,
                       lambda: jnp.full((1, tq), 1 << idx_bits, jnp.int32))

    n_sub = lc1 // lc3

    def logits_chunk(c):
        s0 = pl.multiple_of(c * lc3, lc3)
        lg = _nt_dot(ka_ref[0, pl.ds(s0, lc3), :], qa)
        kk = key_ref[pl.ds(s0, lc3), :]
        sidx = s0 + lax.broadcasted_iota(jnp.int32, (lc3, tq), 0)
        madd = jnp.where(kk > thr, 0.0,
                         jnp.where(kk == thr, jnp.where(sidx < tie_end, 0.0, NEG), NEG))
        d = jnp.clip((q0 - s0) // tq, 0, DSA_NEAR)
        return jnp.concatenate(
            [lg[:, h * tq:(h + 1) * tq] + bias_ref[d, h] + madd for h in range(H_A)], axis=1)

    def attend_step(c1, carry):
        m, l, acc = carry
        lgs = [logits_chunk(c1 * n_sub + sub) for sub in range(n_sub)]
        m_new = m
        for lgb in lgs:
            m_new = jnp.maximum(m_new, _col_reduce(lgb, jnp.max))
        alpha = jnp.exp(m - m_new)
        l = l * alpha
        acc = acc * alpha
        for sub, lgb in enumerate(lgs):
            p = jnp.exp(lgb - m_new)
            l = l + _col_reduce(p, jnp.sum)
            acc = acc + _dot(vt_ref[0, c1 * n_sub + sub], p.astype(BF16))
        return m_new, l, acc

    m0 = jnp.full((1, H_A * tq), NEG, F32)
    l0 = jnp.zeros((1, H_A * tq), F32)
    a0 = jnp.zeros((HEAD_DIM, H_A * tq), F32)
    _, l, acc = lax.fori_loop(0, nch1, attend_step, (m0, l0, a0))
    out = acc / l
    for h in range(H_A):
        o_ref[0, h] = out[:, h * tq:(h + 1) * tq].astype(BF16)


def _t5_bucket(rel):
    nb = N_BUCKETS // 2
    max_exact = nb // 2
    ret = jnp.where(rel > 0, nb, 0).astype(jnp.int32)
    n = jnp.abs(rel)
    n_f = jnp.maximum(n, 1).astype(F32)
    large = max_exact + (jnp.log(n_f / max_exact) / math.log(T5_MAX_DIST / max_exact)
                         * (nb - max_exact)).astype(jnp.int32)
    large = jnp.minimum(large, nb - 1)
    return ret + jnp.where(n < max_exact, n, large).astype(jnp.int32)


def _toeplitz(vec, n, m):
    length = n + m - 1
    lead = vec.shape[:-1]
    flat = jnp.tile(vec, (1,) * len(lead) + (n + 1,))[..., :n * (length + 1)]
    hankel = flat.reshape(lead + (n, length + 1))[..., :m]
    return hankel[..., ::-1]


def _dsa_bias_tiles(t5_table):
    d = jnp.arange(DSA_NEAR + 1)[:, None]
    k = jnp.arange(DSA_LC3 + DSA_TQ - 1)[None, :]
    rel = k - (DSA_TQ - 1) - DSA_TQ * d
    rel = jnp.where(d == DSA_NEAR, -T5_MAX_DIST, rel)
    vec = jnp.moveaxis(t5_table.astype(F32)[_t5_bucket(rel)], -1, 1)
    return _toeplitz(vec, DSA_LC3, DSA_TQ)


def _dsa(qi_hm, qa_hm, wi_t, ki, ka, va, bias_tiles, qoff, l_valid, topk):
    b, _, tq_all, _ = qi_hm.shape
    lp = ki.shape[1]
    nq = tq_all // DSA_TQ
    nc3 = lp // DSA_LC3
    vt = jnp.swapaxes(va.reshape(b, nc3, DSA_LC3, HEAD_DIM), 2, 3)
    qspec = pl.BlockSpec((1, H_A, DSA_TQ, HEAD_DIM), lambda bi, i: (bi, 0, i, 0))
    kspec = pl.BlockSpec((1, lp, HEAD_DIM), lambda bi, i: (bi, 0, 0))
    return pl.pallas_call(
        functools.partial(_dsa_kernel, qoff=qoff, l_valid=l_valid, topk=topk),
        grid=(b, nq),
        in_specs=[qspec, qspec,
                  pl.BlockSpec((1, H_IDX, DSA_TQ), lambda bi, i: (bi, 0, i)),
                  kspec, kspec,
                  pl.BlockSpec((1, nc3, HEAD_DIM, DSA_LC3), lambda bi, i: (bi, 0, 0, 0)),
                  pl.BlockSpec(bias_tiles.shape, lambda bi, i: (0, 0, 0, 0))],
        out_specs=pl.BlockSpec((1, H_A, HEAD_DIM, DSA_TQ), lambda bi, i: (bi, 0, 0, i)),
        out_shape=jax.ShapeDtypeStruct((b, H_A, HEAD_DIM, tq_all), BF16),
        scratch_shapes=[pltpu.VMEM((lp, DSA_TQ), jnp.int32), pltpu.VMEM((32, lp // 32, DSA_TQ), jnp.int32)],
        compiler_params=_cparams(("arbitrary", "arbitrary")),
        name="dsa",
    )(qi_hm, qa_hm, wi_t, ki, ka, vt, bias_tiles)


def _sb_kernel(q_ref, k_ref, v_ref, *refs, qoff, natural):
    t = SB_T
    if natural:
        kd_ref, vd_ref, u_ref, o_ref = refs
    else:
        u_ref, o_ref = refs
    n_heads = q_ref.shape[1]
    i = pl.program_id(1)
    u = u_ref[...]
    kb_diag = qoff // t + i
    row = lax.broadcasted_iota(jnp.int32, (t, t), 0)
    col = lax.broadcasted_iota(jnp.int32, (t, t), 1)
    causal = col < row

    def block(kb, carries, accs, diag):
        s0 = pl.multiple_of(kb * t, t)
        new_c, new_a = [], []
        for h in range(n_heads):
            if natural:
                cols = slice(h * HEAD_DIM, (h + 1) * HEAD_DIM)
                k = (kd_ref[0, :, cols] if diag else k_ref[0, pl.ds(s0, t), cols]).astype(BF16)
                v = (vd_ref[0, :, cols] if diag else v_ref[0, pl.ds(s0, t), cols]).astype(BF16)
            else:
                k = k_ref[0, h, pl.ds(s0, t), :]
                v = v_ref[0, h, pl.ds(s0, t), :]
            z = _nt_dot(q_ref[0, h], k)
            sp = jnp.maximum(z, 0.0) + jnp.log1p(jnp.exp(-jnp.abs(z)))
            lm = -sp
            if diag:
                lm = jnp.where(causal, lm, 0.0)
            hi = lm.astype(BF16)
            lo = (lm - hi.astype(F32)).astype(BF16)
            ext = _dot(hi, u) + _dot(lo, u)
            e = z - sp + ext[:, :t]
            a = jnp.exp(jnp.concatenate(
                [e[:, j * LANES:(j + 1) * LANES] + carries[h] for j in range(t // LANES)], axis=1))
            if diag:
                a = jnp.where(causal, a, 0.0)
            new_a.append(accs[h] + _dot(a.astype(BF16), v))
            new_c.append(carries[h] + ext[:, t:])
        return tuple(new_c), tuple(new_a)

    zeros_c = tuple(jnp.zeros((t, LANES), F32) for _ in range(n_heads))
    zeros_a = tuple(jnp.zeros((t, HEAD_DIM), F32) for _ in range(n_heads))
    carries, accs = block(kb_diag, zeros_c, zeros_a, True)

    def worst(cs):
        m = cs[0]
        for c in cs[1:]:
            m = jnp.maximum(m, c)
        return jnp.max(m)

    def cond(st):
        return jnp.logical_and(st[0] < kb_diag, st[1] > SB_DEAD)

    def body(st):
        j, _, cs, acs = st
        cs, acs = block(kb_diag - 1 - j, cs, acs, False)
        return j + 1, worst(cs), cs, acs

    _, _, _, accs = lax.while_loop(cond, body, (jnp.int32(0), worst(carries), carries, accs))
    for h in range(n_heads):
        o_ref[0, h] = accs[h].astype(BF16)


def _sb(q_hm, k, v, qoff, k_new=None, v_new=None):
    b, h, tq_all, _ = q_hm.shape
    t = SB_T
    natural = k_new is not None
    assert qoff % t == 0 and tq_all % t == 0
    jj = lax.broadcasted_iota(jnp.int32, (t, t + LANES), 0)
    ss = lax.broadcasted_iota(jnp.int32, (t, t + LANES), 1)
    u = jnp.logical_or(jj > ss, ss >= t).astype(BF16)
    qspec = pl.BlockSpec((1, h, t, HEAD_DIM), lambda bi, i: (bi, 0, i, 0))
    uspec = pl.BlockSpec((t, t + LANES), lambda bi, i: (0, 0))
    if natural:
        assert tq_all == t and k.shape[1:] == (qoff, h * HEAD_DIM) and k_new.shape[1:] == (t, h * HEAD_DIM)
        kspec = pl.BlockSpec((1, qoff, h * HEAD_DIM), lambda bi, i: (bi, 0, 0))
        dspec = pl.BlockSpec((1, t, h * HEAD_DIM), lambda bi, i: (bi, 0, 0))
        in_specs, args = [qspec, kspec, kspec, dspec, dspec, uspec], (q_hm, k, v, k_new, v_new, u)
    else:
        lp = k.shape[2]
        assert lp >= qoff + tq_all
        kspec = pl.BlockSpec((1, h, lp, HEAD_DIM), lambda bi, i: (bi, 0, 0, 0))
        in_specs, args = [qspec, kspec, kspec, uspec], (q_hm, k, v, u)
    return pl.pallas_call(
        functools.partial(_sb_kernel, qoff=qoff, natural=natural),
        grid=(b, tq_all // t),
        in_specs=in_specs,
        out_specs=qspec,
        out_shape=jax.ShapeDtypeStruct((b, h, tq_all, HEAD_DIM), BF16),
        compiler_params=_cparams(("arbitrary", "arbitrary")),
        name="sb",
    )(*args)


def _band_kernel(q_ref, k_ref, v_ref, bm_ref, o_ref, *, tq, w, n_invalid):
    i = pl.program_id(1)
    if n_invalid:
        col = lax.broadcasted_iota(jnp.int32, (1, w), 1)
        valid = jnp.where(i * tq + col >= n_invalid, 0.0, NEG)
    for h in range(q_ref.shape[1]):
        lg = _nt_dot(q_ref[0, h], k_ref[0, h]) + bm_ref[h]
        if n_invalid:
            lg = lg + valid
        m = jnp.max(lg, axis=1, keepdims=True)
        p = jnp.exp(lg - m)
        l = jnp.sum(p, axis=1, keepdims=True)
        o_ref[0, h] = (_dot(p.astype(BF16), v_ref[0, h]) / l).astype(BF16)


def _band_bias(rel_table, tq):
    w = C_BAND + tq
    t = jnp.arange(tq)[:, None]
    c = jnp.arange(w)[None, :] - C_BAND
    k = jnp.arange(tq + w - 1)
    rel = jnp.clip(w - 1 - C_BAND - k, -REL_CLIP, REL_CLIP) + REL_CLIP
    bias = _toeplitz(rel_table.astype(F32)[rel].T, tq, w)
    qc = t // CHUNK
    kc = jnp.floor_divide(c, CHUNK)
    mask = (kc <= qc) & (kc >= qc - C_BAND_CHUNKS)
    return jnp.where(mask[None], bias, NEG)


def _band(q_hm, k_ext, v_ext, bm, tq, n_invalid):
    b, h, t, _ = q_hm.shape
    w = C_BAND + tq
    kspec = pl.BlockSpec((pl.Element(1), pl.Element(h), pl.Element(w), pl.Element(HEAD_DIM)),
                         lambda bi, i: (bi, 0, i * tq, 0))
    qspec = pl.BlockSpec((1, h, tq, HEAD_DIM), lambda bi, i: (bi, 0, i, 0))
    return pl.pallas_call(
        functools.partial(_band_kernel, tq=tq, w=w, n_invalid=n_invalid),
        grid=(b, t // tq),
        in_specs=[qspec, kspec, kspec, pl.BlockSpec((h, tq, w), lambda bi, i: (0, 0, 0))],
        out_specs=qspec,
        out_shape=jax.ShapeDtypeStruct((b, h, t, HEAD_DIM), BF16),
        compiler_params=_cparams(("arbitrary", "arbitrary")),
        name="band",
    )(q_hm, k_ext, v_ext, bm)


def _merge_kernel(x_ref, h_ref, oa_ref, ob_ref, oc_ref, g1_ref, sc2_ref, sh2_ref, n2_ref,
                  wg_ref, bg_ref, wb_ref, wo_ref, wr_ref, br_ref,
                  xo_ref, h2_ref, gates_ref):
    d = x_ref.shape[-1]
    hb = h_ref[0]
    mix = None
    off = 0
    for j, o_ref in enumerate((oa_ref, ob_ref, oc_ref)):
        width = o_ref.shape[-1]
        y = _dot(o_ref[0], wb_ref[off:off + width, :])
        off += width
        g = jax.nn.sigmoid(_dot(hb, wg_ref[:, j * d:(j + 1) * d]) + bg_ref[:, j * d:(j + 1) * d])
        mix = g * y if mix is None else mix + g * y
    x = x_ref[0] + g1_ref[0] * _dot(mix.astype(BF16), wo_ref[...])
    xo_ref[0] = x
    h2 = _rms_mod(x, n2_ref[...], sc2_ref[0], sh2_ref[0])
    h2_ref[0] = h2.astype(BF16)

    h2_hi = h2.astype(BF16)
    h2_lo = (h2 - h2_hi.astype(F32)).astype(BF16)
    wr_hi, wr_lo = wr_ref[0], wr_ref[1]
    lr = _dot(h2_hi, wr_hi) + (_dot(h2_lo, wr_hi) + _dot(h2_hi, wr_lo)) + br_ref[...]
    tm = lr.shape[0]
    lane = lax.broadcasted_iota(jnp.int32, (tm, LANES), 1)
    lanef = lane.astype(F32)
    ninf = -jnp.inf
    is_expert = lane < N_EXPERTS
    lg = jnp.where(is_expert, ninf, jnp.where(lane < N_EXPERTS + N_GROUPS, lr, ninf))
    eg = jnp.exp(lg - jnp.max(lg, axis=1, keepdims=True))
    pg = eg / jnp.sum(eg, axis=1, keepdims=True)
    pg_top = jnp.max(pg, axis=1, keepdims=True)
    g_top = jnp.min(jnp.where(pg == pg_top, lanef, float(LANES)), axis=1, keepdims=True) - float(N_EXPERTS)
    in_group = (lane // EXP_PER_GROUP).astype(F32) == g_top
    le = jnp.where(is_expert, jnp.where(in_group, lr, ninf), ninf)
    m1 = jnp.max(le, axis=1, keepdims=True)
    i1 = jnp.min(jnp.where(le == m1, lanef, float(LANES)), axis=1, keepdims=True)
    le2 = jnp.where(lanef == i1, ninf, le)
    m2 = jnp.max(le2, axis=1, keepdims=True)
    i2 = jnp.min(jnp.where(le2 == m2, lanef, float(LANES)), axis=1, keepdims=True)
    e2 = jnp.exp(m2 - m1)
    den = 1.0 + e2
    gates_ref[0] = (jnp.where(lanef == i1, pg_top / den, 0.0)
                    + jnp.where(lanef == i2, pg_top * e2 / den, 0.0))


def _merge(x, h, oa, ob, oc, g1, sc2, sh2, n2, wg, bg, wb, wo, wr, br, tm):
    b, t, d = x.shape

    def tok(width):
        return pl.BlockSpec((1, tm, width), lambda bi, i: (bi, i, 0))

    vec = pl.BlockSpec((1, 1, d), lambda bi, i: (bi, 0, 0))

    def full(a):
        return pl.BlockSpec(a.shape, lambda bi, i: (0,) * a.ndim)

    n2r, bgr = n2.reshape(1, d), bg.reshape(1, 3 * d)
    return pl.pallas_call(
        _merge_kernel,
        grid=(b, t // tm),
        in_specs=[tok(d), tok(d), tok(oa.shape[-1]), tok(ob.shape[-1]), tok(oc.shape[-1]),
                  vec, vec, vec, full(n2r), full(wg), full(bgr), full(wb), full(wo), full(wr), full(br)],
        out_specs=[tok(d), tok(d), tok(LANES)],
        out_shape=[jax.ShapeDtypeStruct((b, t, d), F32), jax.ShapeDtypeStruct((b, t, d), BF16),
                   jax.ShapeDtypeStruct((b, t, LANES), F32)],
        compiler_params=_cparams(("arbitrary", "arbitrary")),
        name="merge",
    )(x, h, oa, ob, oc, g1.reshape(b, 1, d), sc2.reshape(b, 1, d), sh2.reshape(b, 1, d),
      n2r, wg, bgr, wb, wo, wr, br)


def _moe_kernel(x_ref, h2_ref, gates_ref, g2_ref, w1_ref, w3_ref, w2_ref, *refs, final):
    fin_ref = refs[0] if final else None
    o_ref, acc_ref = refs[-2:]
    step = pl.program_id(2)
    n_per = w1_ref.shape[0]

    @pl.when(step == 0)
    def _():
        acc_ref[...] = jnp.zeros_like(acc_ref)

    hb = h2_ref[0]
    gates = gates_ref[0]
    lane = lax.broadcasted_iota(jnp.int32, gates.shape, 1)
    total = None
    for j in range(n_per):
        a = _dot(hb, w1_ref[j])
        bb = _dot(hb, w3_ref[j])
        u = (a * jax.nn.sigmoid(a)) * bb
        out = _dot(u.astype(BF16), w2_ref[j])
        ge = jnp.sum(jnp.where(lane == step * n_per + j, gates, 0.0), axis=1, keepdims=True)
        total = ge * out if total is None else total + ge * out
    acc_ref[...] += total

    @pl.when(step == pl.num_programs(2) - 1)
    def _():
        y = x_ref[0] + g2_ref[0] * acc_ref[...]
        if final:
            y = y * lax.rsqrt(jnp.mean(y * y, axis=-1, keepdims=True) + RMS_EPS) * fin_ref[...]
        o_ref[0] = y


def _moe(x, h2, gates, g2, w1, w3, w2, tm, final_gain=None):
    b, t, d = x.shape
    ne, _, f = w1.shape
    n_per = MOE_EXPERTS_PER_STEP
    tok = lambda width: pl.BlockSpec((1, tm, width), lambda bi, i, e: (bi, i, 0))
    g2spec = tok(d) if g2.shape[1] == t else pl.BlockSpec((1, 1, d), lambda bi, i, e: (bi, 0, 0))
    in_specs = [tok(d), tok(d), tok(LANES), g2spec,
                pl.BlockSpec((n_per, d, f), lambda bi, i, e: (e, 0, 0)),
                pl.BlockSpec((n_per, d, f), lambda bi, i, e: (e, 0, 0)),
                pl.BlockSpec((n_per, f, d), lambda bi, i, e: (e, 0, 0))]
    args = [x, h2, gates, g2, w1, w3, w2]
    if final_gain is not None:
        in_specs.append(pl.BlockSpec((1, d), lambda bi, i, e: (0, 0)))
        args.append(final_gain.reshape(1, d))
    return pl.pallas_call(
        functools.partial(_moe_kernel, final=final_gain is not None),
        grid=(b, t // tm, ne // n_per),
        in_specs=in_specs,
        out_specs=tok(d),
        out_shape=jax.ShapeDtypeStruct((b, t, d), F32),
        scratch_shapes=[pltpu.VMEM((tm, d), F32)],
        compiler_params=_cparams(("arbitrary", "arbitrary", "arbitrary")),
        name="moe",
    )(*args)


def _prep_layer_weights(w_in, w_gate, w_branch, w_out, w_rg, b_rg, w_re, b_re, w1, w3, w2):
    d = w_in.shape[0]
    qa, ka, va, qi, ki, wi, qb, kb, vb, qc, kc, vc = _split_in(w_in)
    pad = jnp.zeros((d, 256 - (64 * 3 + H_IDX)), w_in.dtype)
    w_in_p = jnp.concatenate([qa, qi, ka, va, ki, wi, pad, qb, kb, vb, qc, kc, vc], axis=1).astype(BF16)
    wr = jnp.zeros((d, LANES), F32).at[:, :N_EXPERTS].set(w_re).at[:, N_EXPERTS:N_EXPERTS + N_GROUPS].set(w_rg)
    br = jnp.zeros((1, LANES), F32).at[0, :N_EXPERTS].set(b_re).at[0, N_EXPERTS:N_EXPERTS + N_GROUPS].set(b_rg)
    wr_hi = wr.astype(BF16)
    wr = jnp.stack([wr_hi, (wr - wr_hi.astype(F32)).astype(BF16)])
    return (w_in_p, w_gate.astype(BF16), w_branch.astype(BF16), w_out.astype(BF16), wr, br,
            w1.astype(BF16), w3.astype(BF16), w2.astype(BF16))


def _split_in(w_in):
    sizes = (256, 64, 64, 256, 64, H_IDX, 256, 256, 256, 512, 512, 512)
    out, start = [], 0
    for n in sizes:
        out.append(w_in[:, start:start + n])
        start += n
    return out


def _from_hm(o_hm):
    b, h, t, dh = o_hm.shape
    return jnp.swapaxes(o_hm, 1, 2).reshape(b, t, h * dh)


def _to_hm(a, dtype):
    return jnp.swapaxes(a, 1, 2).astype(dtype)


def _pad_axis(a, axis, size):
    if a.shape[axis] == size:
        return a
    widths = [(0, 0)] * a.ndim
    widths[axis] = (0, size - a.shape[axis])
    return jnp.pad(a, widths)


def _round_up(n, m):
    return -(-n // m) * m


def _layer(x, mods, norms, lw, consts, cache, final_gain):
    n1, n2 = norms
    sh1, sc1, g1, sh2, sc2, g2 = mods
    w_in_p, wg, bg, wb, wo, wr, br, w1, w3, w2 = lw
    dsa_bias, band_bias = consts
    b, t, d = x.shape
    tm = min(512, t)

    (h, qa_hm, qi_hm, sm, qb_hm, kb_hm, vb_hm, kb, vb, qc_hm, kc_hm, vc_hm, kcs, vcs) = _proj(
        x, sc1, sh1, n1, w_in_p, tm, C_BAND if cache is None else 0)
    ka, va, ki = sm[..., 0:64], sm[..., 64:128], sm[..., 128:192]
    wi_t = jnp.swapaxes(sm[..., 192:192 + H_IDX], 1, 2) * (H_IDX ** -0.5)

    if cache is None:
        qoff, tq_pad = 0, t
        ka_f, va_f, ki_f = ka, va, ki
        kc_f, vc_f = kc_hm, vc_hm
        band_tq, n_invalid = BAND_TQ, C_BAND
        state = (ka, va, ki, kb.reshape(b, t, H_B, HEAD_DIM), vb.reshape(b, t, H_B, HEAD_DIM),
                 kcs.reshape(b, -1, H_C, HEAD_DIM), vcs.reshape(b, -1, H_C, HEAD_DIM))
    else:
        ca_k, ca_v, ca_ki, cb_k, cb_v, cc_k, cc_v = cache
        qoff = ca_k.shape[1]
        tq_pad = _round_up(t, DSA_TQ)
        ka_f = jnp.concatenate([ca_k, ka], axis=1)
        va_f = jnp.concatenate([ca_v, va], axis=1)
        ki_f = jnp.concatenate([ca_ki, ki], axis=1)
        kc_f = jnp.concatenate([_to_hm(cc_k, BF16), kc_hm], axis=2)
        vc_f = jnp.concatenate([_to_hm(cc_v, BF16), vc_hm], axis=2)
        band_tq, n_invalid = CHUNK, 0
        kc_new = kcs.reshape(b, t, H_C, HEAD_DIM)
        vc_new = vcs.reshape(b, t, H_C, HEAD_DIM)
        state = (ka, va, ki, kb.reshape(b, t, H_B, HEAD_DIM), vb.reshape(b, t, H_B, HEAD_DIM),
                 jnp.concatenate([cc_k, kc_new], axis=1)[:, t:],
                 jnp.concatenate([cc_v, vc_new], axis=1)[:, t:])

    l_valid = ka_f.shape[1]
    topk = min(TOPK_MAX, l_valid // 4)
    lp = _round_up(l_valid, DSA_LC1)
    oa_t = _dsa(_pad_axis(qi_hm, 2, tq_pad), _pad_axis(qa_hm, 2, tq_pad), _pad_axis(wi_t, 2, tq_pad),
                _pad_axis(ki_f.astype(BF16), 1, lp), _pad_axis(ka_f.astype(BF16), 1, lp),
                _pad_axis(va_f.astype(BF16), 1, lp), dsa_bias, qoff, l_valid, topk)
    oa = jnp.transpose(oa_t[..., :t], (0, 3, 1, 2)).reshape(b, t, H_A * HEAD_DIM)

    tq_sb = _round_up(t, SB_T)
    if cache is None:
        ob_hm = _sb(qb_hm, kb_hm, vb_hm, 0)
    else:
        ob_hm = _sb(_pad_axis(qb_hm, 2, tq_sb), cb_k.reshape(b, qoff, -1), cb_v.reshape(b, qoff, -1), qoff,
                    _pad_axis(kb, 1, tq_sb), _pad_axis(vb, 1, tq_sb))
    ob = _from_hm(ob_hm[:, :, :t])

    oc = _from_hm(_band(qc_hm, kc_f, vc_f, band_bias[band_tq], band_tq, n_invalid))

    x, h2, gates = _merge(x, h, oa, ob, oc, g1, sc2, sh2, n2, wg, bg, wb, wo, wr, br, min(512, t))
    if cache is None:
        x = _moe(x, h2, gates, g2.reshape(b, 1, d), w1, w3, w2, min(512, t), final_gain)
    else:
        flat = lambda a: a.reshape(1, b * t, a.shape[-1])
        g2_rows = jnp.broadcast_to(g2[:, None, :], (b, t, d))
        x = _moe(flat(x), flat(h2), flat(gates), flat(g2_rows), w1, w3, w2, min(512, b * t),
                 final_gain).reshape(b, t, d)
    return x, state


def kernel(x_prompt, x_sample, c_prompt, c_sample, cache_a_k, cache_a_v, cache_a_kidx, cache_b_k, cache_b_v, cache_c_k, cache_c_v, norm1, norm2, final_norm, w_ada, b_ada, w_in, t5_table, rel_c, w_gate, b_gate, w_branch, w_out, w_rg, b_rg, w_re, b_re, w1, w3, w2):
    depth = norm1.shape[0]
    bp = x_prompt.shape[0]
    mods_all = _ada_mods(jnp.concatenate([c_prompt, c_sample], axis=0), w_ada, b_ada)
    dsa_bias = _dsa_bias_tiles(t5_table)
    xp, xs = x_prompt, x_sample
    st_p, st_s = [], []
    for l in range(depth):
        lw = _prep_layer_weights(w_in[l], w_gate[l], w_branch[l], w_out[l], w_rg[l], b_rg[l],
                                 w_re[l], b_re[l], w1[l], w3[l], w2[l])
        lw = lw[:2] + (b_gate[l],) + lw[2:]
        band_bias = {tq: _band_bias(rel_c[l], tq) for tq in (CHUNK, BAND_TQ)}
        consts = (dsa_bias, band_bias)
        norms = (norm1[l], norm2[l])
        mods = jnp.split(mods_all[l], 6, axis=-1)
        fin = final_norm if l == depth - 1 else None
        xp, sp = _layer(xp, [m[:bp] for m in mods], norms, lw, consts, None, fin)
        cache = (cache_a_k[l], cache_a_v[l], cache_a_kidx[l], cache_b_k[l], cache_b_v[l],
                 cache_c_k[l], cache_c_v[l])
        xs, ss = _layer(xs, [m[bp:] for m in mods], norms, lw, consts, cache, fin)
        st_p.append(sp)
        st_s.append(ss)
    stack = lambda states, i: jnp.stack([s[i] for s in states], axis=0)
    return ((xp, xs) + tuple(stack(st_p, i) for i in range(7))
            + tuple(stack(st_s, i) for i in range(7)))
```

```python
import functools
import math

import jax
import jax.numpy as jnp
from jax import lax
from jax.experimental import pallas as pl
from jax.experimental.pallas import tpu as pltpu

F32 = jnp.float32
BF16 = jnp.bfloat16
HIGHEST = lax.Precision.HIGHEST

CHUNK = 64
HEAD_DIM = 64
D_IDX = 64
H_A = 4
H_IDX = 4
H_B = 4
H_C = 8
C_BAND_CHUNKS = 8
C_BAND = C_BAND_CHUNKS * CHUNK
REL_CLIP = 256
N_BUCKETS = 32
T5_MAX_DIST = 1024
TOPK_MAX = 256
N_GROUPS = 4
EXP_PER_GROUP = 4
N_EXPERTS = N_GROUPS * EXP_PER_GROUP
RMS_EPS = 1e-6

NEG = -1e30
MIN_NORMAL_BITS = 0x00800000
ZERO_KEY_TOP = 1 << 14
INT_MIN = -(2 ** 31)
LANES = 128
VMEM_LIMIT = 56 * 1024 * 1024

DSA_TQ = 128
DSA_LC1 = 512
DSA_LC3 = 256
SB_T = 256
SB_DEAD = -104.0
BAND_TQ = 4 * CHUNK
MOE_EXPERTS_PER_STEP = 4
DSA_NEAR = -(-(DSA_LC3 - 1 + T5_MAX_DIST) // DSA_TQ)

_C_QA, _C_QI, _C_SM, _C_QB, _C_KB, _C_VB, _C_QC, _C_KC, _C_VC, _C_END = (
    0, 256, 512, 768, 1024, 1280, 1536, 2048, 2560, 3072)


def _cparams(sem):
    return pltpu.CompilerParams(dimension_semantics=sem, vmem_limit_bytes=VMEM_LIMIT)


def _nt_dot(a, b):
    return lax.dot_general(a, b, (((1,), (1,)), ((), ())), preferred_element_type=F32)


def _dot(a, b):
    return jnp.dot(a, b, preferred_element_type=F32)


def _col_reduce(x, op):
    r, c = x.shape
    group = 64 if r % 64 == 0 else 8
    if r > group:
        x = op(x.reshape(r // group, group, c), axis=0)
    return op(x, axis=0, keepdims=True)


def _bit_planes(words):
    a = list(words)
    j, m = 16, 0x0000FFFF
    while j:
        k = 0
        while k < 32:
            t = (a[k] ^ lax.shift_right_logical(a[k + j], j)) & m
            a[k] = a[k] ^ t
            a[k + j] = a[k + j] ^ lax.shift_left(t, j)
            k = (k + j + 1) & ~j
        j >>= 1
        m ^= (m << j) & 0xFFFFFFFF
    return [a[31 - b] for b in range(32)]


def _rms_mod(x, g, sc, sh):
    ms = jnp.mean(x * x, axis=-1, keepdims=True)
    return (x * lax.rsqrt(ms + RMS_EPS) * g) * (1.0 + sc) + sh


def _ada_kernel(c_ref, w_ref, b_ref, o_ref):
    c = c_ref[...]
    s = c * jax.nn.sigmoid(c)
    o_ref[0] = jnp.dot(s, w_ref[0], preferred_element_type=F32, precision=HIGHEST) + b_ref[0]


def _ada_mods(c_all, w_ada, b_ada):
    depth, d, e = w_ada.shape
    r = c_all.shape[0]
    tn = 1024
    return pl.pallas_call(
        _ada_kernel,
        grid=(depth, e // tn),
        in_specs=[pl.BlockSpec((r, d), lambda l, j: (0, 0)),
                  pl.BlockSpec((1, d, tn), lambda l, j: (l, 0, j)),
                  pl.BlockSpec((1, 1, tn), lambda l, j: (l, 0, j))],
        out_specs=pl.BlockSpec((1, r, tn), lambda l, j: (l, 0, j)),
        out_shape=jax.ShapeDtypeStruct((depth, r, e), F32),
        compiler_params=_cparams(("arbitrary", "arbitrary")),
        name="ada_mods",
    )(c_all, w_ada, b_ada.reshape(depth, 1, e))


def _proj_kernel(x_ref, sc_ref, sh_ref, g_ref, w_ref, *refs, n_tiles, n_state_tiles, n_alias):
    (h_ref, qa_ref, qi_ref, sm_ref, qb_ref, kbh_ref, vbh_ref, kb_ref, vb_ref,
     qc_ref, kch_ref, vch_ref, kcs_ref, vcs_ref) = refs[n_alias:]
    i = pl.program_id(1)
    h = _rms_mod(x_ref[0], g_ref[...], sc_ref[0], sh_ref[0])
    hb = h.astype(BF16)
    h_ref[0] = hb

    def mm(lo, hi):
        return _dot(hb, w_ref[:, lo:hi])

    def heads(ref, y, n, scale=None):
        for hh in range(n):
            part = y[:, hh * HEAD_DIM:(hh + 1) * HEAD_DIM]
            if scale is not None:
                part = part * scale
            ref[0, hh] = part.astype(BF16)

    qscale = HEAD_DIM ** -0.5
    heads(qa_ref, mm(_C_QA, _C_QI), H_A, qscale)
    heads(qi_ref, mm(_C_QI, _C_SM), H_IDX, D_IDX ** -0.5)
    sm_ref[0] = mm(_C_SM, _C_QB)
    heads(qb_ref, mm(_C_QB, _C_KB), H_B, qscale)
    kb = mm(_C_KB, _C_VB)
    kb_ref[0, 0] = kb
    heads(kbh_ref, kb, H_B)
    vb = mm(_C_VB, _C_QC)
    vb_ref[0, 0] = vb
    heads(vbh_ref, vb, H_B)
    heads(qc_ref, mm(_C_QC, _C_KC), H_C, qscale)
    kc = mm(_C_KC, _C_VC)
    heads(kch_ref, kc, H_C)
    vc = mm(_C_VC, _C_END)
    heads(vch_ref, vc, H_C)

    @pl.when(i >= n_tiles - n_state_tiles)
    def _():
        kcs_ref[0] = kc
        vcs_ref[0] = vc


def _proj(x, sc, sh, g, w_in_p, tm, layer, depth, band_bufs, stack_bufs):
    b, t, d = x.shape
    kpad = 0 if band_bufs is None else band_bufs[0].shape[2] - t
    n_tiles = t // tm
    n_state_tiles = min(C_BAND, t) // tm
    nbuf = n_state_tiles * tm
    assert kpad % tm == 0

    def hm(n, pad=0):
        return (pl.BlockSpec((1, n, tm, HEAD_DIM), lambda bi, i: (bi, 0, i + pad // tm, 0)),
                jax.ShapeDtypeStruct((b, n, pad + t, HEAD_DIM), BF16))

    def tok(width, dtype):
        return (pl.BlockSpec((1, tm, width), lambda bi, i: (bi, i, 0)),
                jax.ShapeDtypeStruct((b, t, width), dtype))

    def state(width):
        return (pl.BlockSpec((1, tm, width),
                             lambda bi, i: (bi, jnp.maximum(i - (n_tiles - n_state_tiles), 0), 0)),
                jax.ShapeDtypeStruct((b, nbuf, width), F32))

    def stacked(width):
        return (pl.BlockSpec((1, 1, tm, width), lambda bi, i: (layer, bi, i, 0)),
                jax.ShapeDtypeStruct((depth, b, t, width), F32))

    outs = [tok(d, BF16), hm(H_A), hm(H_IDX), tok(256, F32), hm(H_B), hm(H_B), hm(H_B),
            stacked(256), stacked(256), hm(H_C), hm(H_C, kpad), hm(H_C, kpad), state(512), state(512)]
    vec = pl.BlockSpec((1, 1, d), lambda bi, i: (bi, 0, 0))
    in_specs = [pl.BlockSpec((1, tm, d), lambda bi, i: (bi, i, 0)), vec, vec,
                pl.BlockSpec((1, d), lambda bi, i: (0, 0)),
                pl.BlockSpec((d, _C_END), lambda bi, i: (0, 0))]
    args = [x, sc.reshape(b, 1, d), sh.reshape(b, 1, d), g.reshape(1, d), w_in_p]
    aliases = {}
    for bufs, out_idxs in ((band_bufs, (10, 11)), (stack_bufs, (7, 8))):
        if bufs is not None:
            for buf, out_idx in zip(bufs, out_idxs):
                aliases[len(args)] = out_idx
                in_specs.append(pl.BlockSpec(memory_space=pl.ANY))
                args.append(buf)
    return pl.pallas_call(
        functools.partial(_proj_kernel, n_tiles=n_tiles, n_state_tiles=n_state_tiles, n_alias=len(aliases)),
        grid=(b, n_tiles),
        in_specs=in_specs,
        out_specs=[o[0] for o in outs],
        out_shape=[o[1] for o in outs],
        input_output_aliases=aliases,
        compiler_params=_cparams(("arbitrary", "arbitrary")),
        name="proj",
    )(*args)


def _dsa_kernel(qi_ref, qa_ref, wi_ref, ki_ref, ka_ref, vt_ref, bias_ref, o_ref, key_ref, plane_ref,
                *, qoff, l_valid, topk):
    tq, lc1, lc3 = DSA_TQ, DSA_LC1, DSA_LC3
    grp = 256
    n_rows = plane_ref.shape[1]
    i = pl.program_id(1)
    q0 = qoff + i * tq
    lane = lax.broadcasted_iota(jnp.int32, (1, tq), 1)
    qpos = q0 + lane
    lim = jnp.minimum((qpos // CHUNK + 1) * CHUNK, l_valid)
    n_adm = jnp.minimum(((q0 + tq - 1) // CHUNK + 1) * CHUNK, l_valid)
    nch1 = (n_adm + lc1 - 1) // lc1
    qi = qi_ref[0].reshape(H_IDX * tq, D_IDX)
    qa = qa_ref[0].reshape(H_A * tq, HEAD_DIM)
    wi = wi_ref[0]

    def score_chunk(c, carry):
        s0 = pl.multiple_of(c * lc1, lc1)
        s_all = _nt_dot(ki_ref[0, pl.ds(s0, lc1), :], qi)
        sc = None
        for h in range(H_IDX):
            term = wi[h:h + 1, :] * jnp.maximum(s_all[:, h * tq:(h + 1) * tq], 0.0)
            sc = term if sc is None else sc + term
        bits = lax.bitcast_convert_type(sc, jnp.int32)
        key = bits ^ (lax.shift_right_arithmetic(bits, 31) & 0x7FFFFFFF)
        sidx = s0 + lax.broadcasted_iota(jnp.int32, (lc1, tq), 0)
        tiny = (bits & 0x7FFFFFFF) < MIN_NORMAL_BITS
        key = jnp.where(tiny, jnp.where(bits < 0, -1 - sidx, ZERO_KEY_TOP - sidx), key)
        key = jnp.where(sidx < lim, key, INT_MIN)
        key_ref[pl.ds(s0, lc1), :] = key
        ukey = (key ^ INT_MIN).reshape(lc1 // 8, 8, tq)
        for g in range(lc1 // grp):
            planes = _bit_planes([ukey[g * 32 + r] for r in range(32)])
            row0 = pl.multiple_of((c * (lc1 // grp) + g) * 8, 8)
            for bit in range(32):
                plane_ref[bit, pl.ds(row0, 8), :] = planes[bit]
        return carry

    lax.fori_loop(0, nch1, score_chunk, 0)

    def count(*preds):
        def body(c, accs):
            s0 = pl.multiple_of(c * lc1, lc1)
            kk = key_ref[pl.ds(s0, lc1), :]
            sidx = s0 + lax.broadcasted_iota(jnp.int32, (lc1, tq), 0)
            return tuple(acc + jnp.sum(pred(kk, sidx).reshape(lc1 // 64, 64, tq), axis=0)
                         for acc, pred in zip(accs, preds))
        accs = lax.fori_loop(0, nch1, body, tuple(jnp.zeros((64, tq), F32) for _ in preds))
        return tuple(jnp.sum(acc, axis=0, keepdims=True) for acc in accs)

    kf = float(topk)
    row = lax.broadcasted_iota(jnp.int32, (n_rows, tq), 0)
    eq0 = jnp.where(row < nch1 * (lc1 // grp) * 8, -1, 0)

    def bit_step(it, carry):
        eq, n_gt, thr_u = carry
        bit = 31 - it
        w = eq & plane_ref[bit]
        n_set = _col_reduce(lax.population_count(w).astype(F32), jnp.sum)
        take = n_gt + n_set >= kf
        eq = jnp.where(take, w, eq ^ w)
        n_gt = jnp.where(take, n_gt, n_gt + n_set)
        thr_u = jnp.where(take, thr_u | lax.shift_left(jnp.int32(1), bit), thr_u)
        return eq, n_gt, thr_u

    eq, n_gt, thr_u = lax.fori_loop(
        0, 32, bit_step, (eq0, jnp.zeros((1, tq), F32), jnp.zeros((1, tq), jnp.int32)))
    n_eq = _col_reduce(lax.population_count(eq).astype(F32), jnp.sum)
    at_floor = (thr_u ^ INT_MIN) == INT_MIN
    thr = jnp.maximum(thr_u ^ INT_MIN, INT_MIN + 1)
    n_eq = jnp.where(at_floor, 0.0, n_eq)

    room = kf - n_gt
    idx_bits = 14
    assert l_valid < (1 << idx_bits) <= ZERO_KEY_TOP

    def tie_search():
        def step(it, end):
            cand = end + lax.shift_left(jnp.int32(1), idx_bits - 1 - it)
            n, = count(lambda kk, sidx: jnp.where(kk == thr, jnp.where(sidx < cand, 1.0, 0.0), 0.0))
            return jnp.where(n <= room, cand, end)
        return lax.fori_loop(0, idx_bits, step, jnp.zeros((1, tq), jnp.int32))

    tie_end = lax.cond(jnp.max(n_eq - room) > 0.0, tie_search,
                       lambda: jnp.full((1, tq), 1 << idx_bits, jnp.int32))

    n_sub = lc1 // lc3

    def logits_chunk(c):
        s0 = pl.multiple_of(c * lc3, lc3)
        lg = _nt_dot(ka_ref[0, pl.ds(s0, lc3), :], qa)
        kk = key_ref[pl.ds(s0, lc3), :]
        sidx = s0 + lax.broadcasted_iota(jnp.int32, (lc3, tq), 0)
        madd = jnp.where(kk > thr, 0.0,
                         jnp.where(kk == thr, jnp.where(sidx < tie_end, 0.0, NEG), NEG))
        d = jnp.clip((q0 - s0) // tq, 0, DSA_NEAR)
        return jnp.concatenate(
            [lg[:, h * tq:(h + 1) * tq] + bias_ref[d, h] + madd for h in range(H_A)], axis=1)

    def attend_step(c1, carry):
        m, l, acc = carry
        lgs = [logits_chunk(c1 * n_sub + sub) for sub in range(n_sub)]
        m_new = m
        for lgb in lgs:
            m_new = jnp.maximum(m_new, _col_reduce(lgb, jnp.max))
        alpha = jnp.exp(m - m_new)
        l = l * alpha
        acc = acc * alpha
        for sub, lgb in enumerate(lgs):
            p = jnp.exp(lgb - m_new)
            l = l + _col_reduce(p, jnp.sum)
            acc = acc + _dot(vt_ref[0, c1 * n_sub + sub], p.astype(BF16))
        return m_new, l, acc

    m0 = jnp.full((1, H_A * tq), NEG, F32)
    l0 = jnp.zeros((1, H_A * tq), F32)
    a0 = jnp.zeros((HEAD_DIM, H_A * tq), F32)
    _, l, acc = lax.fori_loop(0, nch1, attend_step, (m0, l0, a0))
    out = acc / l
    for h in range(H_A):
        o_ref[0, h] = out[:, h * tq:(h + 1) * tq].astype(BF16)


def _t5_bucket(rel):
    nb = N_BUCKETS // 2
    max_exact = nb // 2
    ret = jnp.where(rel > 0, nb, 0).astype(jnp.int32)
    n = jnp.abs(rel)
    n_f = jnp.maximum(n, 1).astype(F32)
    large = max_exact + (jnp.log(n_f / max_exact) / math.log(T5_MAX_DIST / max_exact)
                         * (nb - max_exact)).astype(jnp.int32)
    large = jnp.minimum(large, nb - 1)
    return ret + jnp.where(n < max_exact, n, large).astype(jnp.int32)


def _toeplitz(vec, n, m):
    length = n + m - 1
    lead = vec.shape[:-1]
    ring = jnp.concatenate([vec[..., m - 1::-1], vec[..., :m - 1:-1]], axis=-1)
    flat = jnp.tile(ring, (1,) * len(lead) + (n,))[..., :n * (length - 1)]
    return flat.reshape(lead + (n, length - 1))[..., :m]


def _dsa_bias_tiles(t5_table):
    d = jnp.arange(DSA_NEAR + 1)[:, None]
    k = jnp.arange(DSA_LC3 + DSA_TQ - 1)[None, :]
    rel = k - (DSA_TQ - 1) - DSA_TQ * d
    rel = jnp.where(d == DSA_NEAR, -T5_MAX_DIST, rel)
    vec = jnp.moveaxis(t5_table.astype(F32)[_t5_bucket(rel)], -1, 1)
    return _toeplitz(vec, DSA_LC3, DSA_TQ)


def _dsa(qi_hm, qa_hm, wi_t, ki, ka, va, bias_tiles, qoff, l_valid, topk):
    b, _, tq_all, _ = qi_hm.shape
    lp = ki.shape[1]
    nq = tq_all // DSA_TQ
    nc3 = lp // DSA_LC3
    vt = jnp.swapaxes(va.reshape(b, nc3, DSA_LC3, HEAD_DIM), 2, 3)
    qspec = pl.BlockSpec((1, H_A, DSA_TQ, HEAD_DIM), lambda bi, i: (bi, 0, i, 0))
    kspec = pl.BlockSpec((1, lp, HEAD_DIM), lambda bi, i: (bi, 0, 0))
    return pl.pallas_call(
        functools.partial(_dsa_kernel, qoff=qoff, l_valid=l_valid, topk=topk),
        grid=(b, nq),
        in_specs=[qspec, qspec,
                  pl.BlockSpec((1, H_IDX, DSA_TQ), lambda bi, i: (bi, 0, i)),
                  kspec, kspec,
                  pl.BlockSpec((1, nc3, HEAD_DIM, DSA_LC3), lambda bi, i: (bi, 0, 0, 0)),
                  pl.BlockSpec(bias_tiles.shape, lambda bi, i: (0, 0, 0, 0))],
        out_specs=pl.BlockSpec((1, H_A, HEAD_DIM, DSA_TQ), lambda bi, i: (bi, 0, 0, i)),
        out_shape=jax.ShapeDtypeStruct((b, H_A, HEAD_DIM, tq_all), BF16),
        scratch_shapes=[pltpu.VMEM((lp, DSA_TQ), jnp.int32), pltpu.VMEM((32, lp // 32, DSA_TQ), jnp.int32)],
        compiler_params=_cparams(("arbitrary", "arbitrary")),
        name="dsa",
    )(qi_hm, qa_hm, wi_t, ki, ka, vt, bias_tiles)


def _sb_kernel(q_ref, k_ref, v_ref, *refs, qoff, natural):
    t = SB_T
    if natural:
        kd_ref, vd_ref, u_ref, o_ref = refs
    else:
        u_ref, o_ref = refs
    n_heads = q_ref.shape[1]
    i = pl.program_id(1)
    u = u_ref[...]
    kb_diag = qoff // t + i
    row = lax.broadcasted_iota(jnp.int32, (t, t), 0)
    col = lax.broadcasted_iota(jnp.int32, (t, t), 1)
    causal = col < row

    def block(kb, carries, accs, diag):
        s0 = pl.multiple_of(kb * t, t)
        new_c, new_a = [], []
        for h in range(n_heads):
            if natural:
                cols = slice(h * HEAD_DIM, (h + 1) * HEAD_DIM)
                k = (kd_ref[0, :, cols] if diag else k_ref[0, pl.ds(s0, t), cols]).astype(BF16)
                v = (vd_ref[0, :, cols] if diag else v_ref[0, pl.ds(s0, t), cols]).astype(BF16)
            else:
                k = k_ref[0, h, pl.ds(s0, t), :]
                v = v_ref[0, h, pl.ds(s0, t), :]
            z = _nt_dot(q_ref[0, h], k)
            sp = jnp.maximum(z, 0.0) + jnp.log1p(jnp.exp(-jnp.abs(z)))
            lm = -sp
            if diag:
                lm = jnp.where(causal, lm, 0.0)
            hi = lm.astype(BF16)
            lo = (lm - hi.astype(F32)).astype(BF16)
            ext = _dot(hi, u) + _dot(lo, u)
            e = z - sp + ext[:, :t]
            a = jnp.exp(jnp.concatenate(
                [e[:, j * LANES:(j + 1) * LANES] + carries[h] for j in range(t // LANES)], axis=1))
            if diag:
                a = jnp.where(causal, a, 0.0)
            new_a.append(accs[h] + _dot(a.astype(BF16), v))
            new_c.append(carries[h] + ext[:, t:])
        return tuple(new_c), tuple(new_a)

    zeros_c = tuple(jnp.zeros((t, LANES), F32) for _ in range(n_heads))
    zeros_a = tuple(jnp.zeros((t, HEAD_DIM), F32) for _ in range(n_heads))
    carries, accs = block(kb_diag, zeros_c, zeros_a, True)

    def worst(cs):
        m = cs[0]
        for c in cs[1:]:
            m = jnp.maximum(m, c)
        return jnp.max(m)

    def cond(st):
        return jnp.logical_and(st[0] < kb_diag, st[1] > SB_DEAD)

    def body(st):
        j, _, cs, acs = st
        cs, acs = block(kb_diag - 1 - j, cs, acs, False)
        return j + 1, worst(cs), cs, acs

    _, _, _, accs = lax.while_loop(cond, body, (jnp.int32(0), worst(carries), carries, accs))
    for h in range(n_heads):
        o_ref[0, h] = accs[h].astype(BF16)


def _sb(q_hm, k, v, qoff, k_new=None, v_new=None):
    b, h, tq_all, _ = q_hm.shape
    t = SB_T
    natural = k_new is not None
    assert qoff % t == 0 and tq_all % t == 0
    jj = lax.broadcasted_iota(jnp.int32, (t, t + LANES), 0)
    ss = lax.broadcasted_iota(jnp.int32, (t, t + LANES), 1)
    u = jnp.logical_or(jj > ss, ss >= t).astype(BF16)
    qspec = pl.BlockSpec((1, h, t, HEAD_DIM), lambda bi, i: (bi, 0, i, 0))
    uspec = pl.BlockSpec((t, t + LANES), lambda bi, i: (0, 0))
    if natural:
        assert tq_all == t and k.shape[1:] == (qoff, h * HEAD_DIM) and k_new.shape[1:] == (t, h * HEAD_DIM)
        kspec = pl.BlockSpec((1, qoff, h * HEAD_DIM), lambda bi, i: (bi, 0, 0))
        dspec = pl.BlockSpec((1, t, h * HEAD_DIM), lambda bi, i: (bi, 0, 0))
        in_specs, args = [qspec, kspec, kspec, dspec, dspec, uspec], (q_hm, k, v, k_new, v_new, u)
    else:
        lp = k.shape[2]
        assert lp >= qoff + tq_all
        kspec = pl.BlockSpec((1, h, lp, HEAD_DIM), lambda bi, i: (bi, 0, 0, 0))
        in_specs, args = [qspec, kspec, kspec, uspec], (q_hm, k, v, u)
    return pl.pallas_call(
        functools.partial(_sb_kernel, qoff=qoff, natural=natural),
        grid=(b, tq_all // t),
        in_specs=in_specs,
        out_specs=qspec,
        out_shape=jax.ShapeDtypeStruct((b, h, tq_all, HEAD_DIM), BF16),
        compiler_params=_cparams(("arbitrary", "arbitrary")),
        name="sb",
    )(*args)


def _band_kernel(q_ref, k_ref, v_ref, bm_ref, o_ref, *, tq, w, n_invalid):
    i = pl.program_id(1)
    if n_invalid:
        col = lax.broadcasted_iota(jnp.int32, (1, w), 1)
        valid = jnp.where(i * tq + col >= n_invalid, 0.0, NEG)
    for h in range(q_ref.shape[1]):
        lg = _nt_dot(q_ref[0, h], k_ref[0, h]) + bm_ref[h]
        if n_invalid:
            lg = lg + valid
        m = jnp.max(lg, axis=1, keepdims=True)
        p = jnp.exp(lg - m)
        l = jnp.sum(p, axis=1, keepdims=True)
        o_ref[0, h] = (_dot(p.astype(BF16), v_ref[0, h]) / l).astype(BF16)


def _band_bias(rel_table, tq):
    w = C_BAND + tq
    t = jnp.arange(tq)[:, None]
    c = jnp.arange(w)[None, :] - C_BAND
    k = jnp.arange(tq + w - 1)
    rel = jnp.clip(w - 1 - C_BAND - k, -REL_CLIP, REL_CLIP) + REL_CLIP
    bias = _toeplitz(rel_table.astype(F32)[rel].T, tq, w)
    qc = t // CHUNK
    kc = jnp.floor_divide(c, CHUNK)
    mask = (kc <= qc) & (kc >= qc - C_BAND_CHUNKS)
    return jnp.where(mask[None], bias, NEG)


def _band(q_hm, k_ext, v_ext, bm, tq, n_invalid):
    b, h, t, _ = q_hm.shape
    w = C_BAND + tq
    kspec = pl.BlockSpec((pl.Element(1), pl.Element(h), pl.Element(w), pl.Element(HEAD_DIM)),
                         lambda bi, i: (bi, 0, i * tq, 0))
    qspec = pl.BlockSpec((1, h, tq, HEAD_DIM), lambda bi, i: (bi, 0, i, 0))
    return pl.pallas_call(
        functools.partial(_band_kernel, tq=tq, w=w, n_invalid=n_invalid),
        grid=(b, t // tq),
        in_specs=[qspec, kspec, kspec, pl.BlockSpec((h, tq, w), lambda bi, i: (0, 0, 0))],
        out_specs=qspec,
        out_shape=jax.ShapeDtypeStruct((b, h, t, HEAD_DIM), BF16),
        compiler_params=_cparams(("arbitrary", "arbitrary")),
        name="band",
    )(q_hm, k_ext, v_ext, bm)


def _merge_kernel(x_ref, h_ref, oa_ref, ob_ref, oc_ref, g1_ref, sc2_ref, sh2_ref, n2_ref,
                  wg_ref, bg_ref, wb_ref, wo_ref, wr_ref, br_ref,
                  xo_ref, h2_ref, gates_ref):
    d = x_ref.shape[-1]
    hb = h_ref[0]
    mix = None
    off = 0
    for j, o_ref in enumerate((oa_ref, ob_ref, oc_ref)):
        width = o_ref.shape[-1]
        y = _dot(o_ref[0], wb_ref[off:off + width, :])
        off += width
        g = jax.nn.sigmoid(_dot(hb, wg_ref[:, j * d:(j + 1) * d]) + bg_ref[:, j * d:(j + 1) * d])
        mix = g * y if mix is None else mix + g * y
    x = x_ref[0] + g1_ref[0] * _dot(mix.astype(BF16), wo_ref[...])
    xo_ref[0] = x
    h2 = _rms_mod(x, n2_ref[...], sc2_ref[0], sh2_ref[0])
    h2_ref[0] = h2.astype(BF16)

    h2_hi = h2.astype(BF16)
    h2_lo = (h2 - h2_hi.astype(F32)).astype(BF16)
    wr_hi, wr_lo = wr_ref[0], wr_ref[1]
    lr = _dot(h2_hi, wr_hi) + (_dot(h2_lo, wr_hi) + _dot(h2_hi, wr_lo)) + br_ref[...]
    tm = lr.shape[0]
    lane = lax.broadcasted_iota(jnp.int32, (tm, LANES), 1)
    lanef = lane.astype(F32)
    ninf = -jnp.inf
    is_expert = lane < N_EXPERTS
    lg = jnp.where(is_expert, ninf, jnp.where(lane < N_EXPERTS + N_GROUPS, lr, ninf))
    eg = jnp.exp(lg - jnp.max(lg, axis=1, keepdims=True))
    pg = eg / jnp.sum(eg, axis=1, keepdims=True)
    pg_top = jnp.max(pg, axis=1, keepdims=True)
    g_top = jnp.min(jnp.where(pg == pg_top, lanef, float(LANES)), axis=1, keepdims=True) - float(N_EXPERTS)
    in_group = (lane // EXP_PER_GROUP).astype(F32) == g_top
    le = jnp.where(is_expert, jnp.where(in_group, lr, ninf), ninf)
    m1 = jnp.max(le, axis=1, keepdims=True)
    i1 = jnp.min(jnp.where(le == m1, lanef, float(LANES)), axis=1, keepdims=True)
    le2 = jnp.where(lanef == i1, ninf, le)
    m2 = jnp.max(le2, axis=1, keepdims=True)
    i2 = jnp.min(jnp.where(le2 == m2, lanef, float(LANES)), axis=1, keepdims=True)
    e2 = jnp.exp(m2 - m1)
    den = 1.0 + e2
    gates_ref[0] = (jnp.where(lanef == i1, pg_top / den, 0.0)
                    + jnp.where(lanef == i2, pg_top * e2 / den, 0.0))


def _merge(x, h, oa, ob, oc, g1, sc2, sh2, n2, wg, bg, wb, wo, wr, br, tm):
    b, t, d = x.shape

    def tok(width):
        return pl.BlockSpec((1, tm, width), lambda bi, i: (bi, i, 0))

    vec = pl.BlockSpec((1, 1, d), lambda bi, i: (bi, 0, 0))

    def full(a):
        return pl.BlockSpec(a.shape, lambda bi, i: (0,) * a.ndim)

    n2r, bgr = n2.reshape(1, d), bg.reshape(1, 3 * d)
    return pl.pallas_call(
        _merge_kernel,
        grid=(b, t // tm),
        in_specs=[tok(d), tok(d), tok(oa.shape[-1]), tok(ob.shape[-1]), tok(oc.shape[-1]),
                  vec, vec, vec, full(n2r), full(wg), full(bgr), full(wb), full(wo), full(wr), full(br)],
        out_specs=[tok(d), tok(d), tok(LANES)],
        out_shape=[jax.ShapeDtypeStruct((b, t, d), F32), jax.ShapeDtypeStruct((b, t, d), BF16),
                   jax.ShapeDtypeStruct((b, t, LANES), F32)],
        compiler_params=_cparams(("arbitrary", "arbitrary")),
        name="merge",
    )(x, h, oa, ob, oc, g1.reshape(b, 1, d), sc2.reshape(b, 1, d), sh2.reshape(b, 1, d),
      n2r, wg, bgr, wb, wo, wr, br)


def _moe_kernel(x_ref, h2_ref, gates_ref, g2_ref, w1_ref, w3_ref, w2_ref, *refs, final):
    fin_ref = refs[0] if final else None
    o_ref, acc_ref = refs[-2:]
    step = pl.program_id(2)
    n_per = w1_ref.shape[0]

    @pl.when(step == 0)
    def _():
        acc_ref[...] = jnp.zeros_like(acc_ref)

    hb = h2_ref[0]
    gates = gates_ref[0]
    lane = lax.broadcasted_iota(jnp.int32, gates.shape, 1)
    total = None
    for j in range(n_per):
        a = _dot(hb, w1_ref[j])
        bb = _dot(hb, w3_ref[j])
        u = (a * jax.nn.sigmoid(a)) * bb
        out = _dot(u.astype(BF16), w2_ref[j])
        ge = jnp.sum(jnp.where(lane == step * n_per + j, gates, 0.0), axis=1, keepdims=True)
        total = ge * out if total is None else total + ge * out
    acc_ref[...] += total

    @pl.when(step == pl.num_programs(2) - 1)
    def _():
        y = x_ref[0] + g2_ref[0] * acc_ref[...]
        if final:
            y = y * lax.rsqrt(jnp.mean(y * y, axis=-1, keepdims=True) + RMS_EPS) * fin_ref[...]
        o_ref[0] = y


def _moe(x, h2, gates, g2, w1, w3, w2, tm, final_gain=None):
    b, t, d = x.shape
    ne, _, f = w1.shape
    n_per = MOE_EXPERTS_PER_STEP
    tok = lambda width: pl.BlockSpec((1, tm, width), lambda bi, i, e: (bi, i, 0))
    g2spec = tok(d) if g2.shape[1] == t else pl.BlockSpec((1, 1, d), lambda bi, i, e: (bi, 0, 0))
    in_specs = [tok(d), tok(d), tok(LANES), g2spec,
                pl.BlockSpec((n_per, d, f), lambda bi, i, e: (e, 0, 0)),
                pl.BlockSpec((n_per, d, f), lambda bi, i, e: (e, 0, 0)),
                pl.BlockSpec((n_per, f, d), lambda bi, i, e: (e, 0, 0))]
    args = [x, h2, gates, g2, w1, w3, w2]
    if final_gain is not None:
        in_specs.append(pl.BlockSpec((1, d), lambda bi, i, e: (0, 0)))
        args.append(final_gain.reshape(1, d))
    return pl.pallas_call(
        functools.partial(_moe_kernel, final=final_gain is not None),
        grid=(b, t // tm, ne // n_per),
        in_specs=in_specs,
        out_specs=tok(d),
        out_shape=jax.ShapeDtypeStruct((b, t, d), F32),
        scratch_shapes=[pltpu.VMEM((tm, d), F32)],
        compiler_params=_cparams(("arbitrary", "arbitrary", "arbitrary")),
        name="moe",
    )(*args)


def _prep_layer_weights(w_in, w_gate, w_branch, w_out, w_rg, b_rg, w_re, b_re, w1, w3, w2):
    d = w_in.shape[0]
    qa, ka, va, qi, ki, wi, qb, kb, vb, qc, kc, vc = _split_in(w_in)
    pad = jnp.zeros((d, 256 - (64 * 3 + H_IDX)), w_in.dtype)
    w_in_p = jnp.concatenate([qa, qi, ka, va, ki, wi, pad, qb, kb, vb, qc, kc, vc], axis=1).astype(BF16)
    wr = jnp.zeros((d, LANES), F32).at[:, :N_EXPERTS].set(w_re).at[:, N_EXPERTS:N_EXPERTS + N_GROUPS].set(w_rg)
    br = jnp.zeros((1, LANES), F32).at[0, :N_EXPERTS].set(b_re).at[0, N_EXPERTS:N_EXPERTS + N_GROUPS].set(b_rg)
    wr_hi = wr.astype(BF16)
    wr = jnp.stack([wr_hi, (wr - wr_hi.astype(F32)).astype(BF16)])
    return (w_in_p, w_gate.astype(BF16), w_branch.astype(BF16), w_out.astype(BF16), wr, br,
            w1.astype(BF16), w3.astype(BF16), w2.astype(BF16))


def _split_in(w_in):
    sizes = (256, 64, 64, 256, 64, H_IDX, 256, 256, 256, 512, 512, 512)
    out, start = [], 0
    for n in sizes:
        out.append(w_in[:, start:start + n])
        start += n
    return out


def _from_hm(o_hm):
    b, h, t, dh = o_hm.shape
    return jnp.swapaxes(o_hm, 1, 2).reshape(b, t, h * dh)


def _to_hm(a, dtype):
    return jnp.swapaxes(a, 1, 2).astype(dtype)


def _pad_axis(a, axis, size):
    if a.shape[axis] == size:
        return a
    widths = [(0, 0)] * a.ndim
    widths[axis] = (0, size - a.shape[axis])
    return jnp.pad(a, widths)


def _round_up(n, m):
    return -(-n // m) * m


def _layer(x, mods, norms, lw, consts, cache, final_gain, layer, depth, bufs):
    n1, n2 = norms
    sh1, sc1, g1, sh2, sc2, g2 = mods
    w_in_p, wg, bg, wb, wo, wr, br, w1, w3, w2 = lw
    dsa_bias, band_bias = consts
    b, t, d = x.shape
    tm = min(512, t)

    (h, qa_hm, qi_hm, sm, qb_hm, kb_hm, vb_hm, kb_all, vb_all, qc_hm, kc_hm, vc_hm, kcs, vcs) = _proj(
        x, sc1, sh1, n1, w_in_p, tm, layer, depth, *bufs)
    bufs = ((kc_hm, vc_hm) if bufs[0] is not None else None, (kb_all, vb_all))
    ka, va, ki = sm[..., 0:64], sm[..., 64:128], sm[..., 128:192]
    wi_t = jnp.swapaxes(sm[..., 192:192 + H_IDX], 1, 2) * (H_IDX ** -0.5)

    if cache is None:
        qoff, tq_pad = 0, t
        ka_f, va_f, ki_f = ka, va, ki
        kc_f, vc_f = kc_hm, vc_hm
        band_tq, n_invalid = BAND_TQ, C_BAND
        state = (ka, va, ki, None, None,
                 kcs.reshape(b, -1, H_C, HEAD_DIM), vcs.reshape(b, -1, H_C, HEAD_DIM))
    else:
        ca_k, ca_v, ca_ki, cb_k, cb_v, cc_k, cc_v = cache
        qoff = ca_k.shape[1]
        tq_pad = _round_up(t, DSA_TQ)
        ka_f = jnp.concatenate([ca_k, ka], axis=1)
        va_f = jnp.concatenate([ca_v, va], axis=1)
        ki_f = jnp.concatenate([ca_ki, ki], axis=1)
        kc_f = jnp.concatenate([_to_hm(cc_k, BF16), kc_hm], axis=2)
        vc_f = jnp.concatenate([_to_hm(cc_v, BF16), vc_hm], axis=2)
        band_tq, n_invalid = CHUNK, 0
        kc_new = kcs.reshape(b, t, H_C, HEAD_DIM)
        vc_new = vcs.reshape(b, t, H_C, HEAD_DIM)
        state = (ka, va, ki, None, None,
                 jnp.concatenate([cc_k, kc_new], axis=1)[:, t:],
                 jnp.concatenate([cc_v, vc_new], axis=1)[:, t:])

    l_valid = ka_f.shape[1]
    topk = min(TOPK_MAX, l_valid // 4)
    lp = _round_up(l_valid, DSA_LC1)
    oa_t = _dsa(_pad_axis(qi_hm, 2, tq_pad), _pad_axis(qa_hm, 2, tq_pad), _pad_axis(wi_t, 2, tq_pad),
                _pad_axis(ki_f.astype(BF16), 1, lp), _pad_axis(ka_f.astype(BF16), 1, lp),
                _pad_axis(va_f.astype(BF16), 1, lp), dsa_bias, qoff, l_valid, topk)
    oa = jnp.transpose(oa_t[..., :t], (0, 3, 1, 2)).reshape(b, t, H_A * HEAD_DIM)

    tq_sb = _round_up(t, SB_T)
    if cache is None:
        ob_hm = _sb(qb_hm, kb_hm, vb_hm, 0)
    else:
        ob_hm = _sb(_pad_axis(qb_hm, 2, tq_sb), cb_k.reshape(b, qoff, -1), cb_v.reshape(b, qoff, -1), qoff,
                    _pad_axis(kb_all[layer], 1, tq_sb), _pad_axis(vb_all[layer], 1, tq_sb))
    ob = _from_hm(ob_hm[:, :, :t])

    oc = _from_hm(_band(qc_hm, kc_f, vc_f, band_bias[band_tq], band_tq, n_invalid))

    x, h2, gates = _merge(x, h, oa, ob, oc, g1, sc2, sh2, n2, wg, bg, wb, wo, wr, br, min(512, t))
    if cache is None:
        x = _moe(x, h2, gates, g2.reshape(b, 1, d), w1, w3, w2, min(512, t), final_gain)
    else:
        flat = lambda a: a.reshape(1, b * t, a.shape[-1])
        g2_rows = jnp.broadcast_to(g2[:, None, :], (b, t, d))
        x = _moe(flat(x), flat(h2), flat(gates), flat(g2_rows), w1, w3, w2, min(512, b * t),
                 final_gain).reshape(b, t, d)
    return x, state, bufs


def kernel(x_prompt, x_sample, c_prompt, c_sample, cache_a_k, cache_a_v, cache_a_kidx, cache_b_k, cache_b_v, cache_c_k, cache_c_v, norm1, norm2, final_norm, w_ada, b_ada, w_in, t5_table, rel_c, w_gate, b_gate, w_branch, w_out, w_rg, b_rg, w_re, b_re, w1, w3, w2):
    depth = norm1.shape[0]
    bp = x_prompt.shape[0]
    mods_all = _ada_mods(jnp.concatenate([c_prompt, c_sample], axis=0), w_ada, b_ada)
    dsa_bias = _dsa_bias_tiles(t5_table)
    xp, xs = x_prompt, x_sample
    st_p, st_s = [], []
    band_shape = (bp, H_C, C_BAND + xp.shape[1], HEAD_DIM)
    bufs_p = ((jnp.zeros(band_shape, BF16), jnp.zeros(band_shape, BF16)), None)
    bufs_s = (None, None)
    for l in range(depth):
        lw = _prep_layer_weights(w_in[l], w_gate[l], w_branch[l], w_out[l], w_rg[l], b_rg[l],
                                 w_re[l], b_re[l], w1[l], w3[l], w2[l])
        lw = lw[:2] + (b_gate[l],) + lw[2:]
        band_bias = {tq: _band_bias(rel_c[l], tq) for tq in (CHUNK, BAND_TQ)}
        consts = (dsa_bias, band_bias)
        norms = (norm1[l], norm2[l])
        mods = jnp.split(mods_all[l], 6, axis=-1)
        fin = final_norm if l == depth - 1 else None
        xp, sp, bufs_p = _layer(xp, [m[:bp] for m in mods], norms, lw, consts, None, fin, l, depth, bufs_p)
        cache = (cache_a_k[l], cache_a_v[l], cache_a_kidx[l], cache_b_k[l], cache_b_v[l],
                 cache_c_k[l], cache_c_v[l])
        xs, ss, bufs_s = _layer(xs, [m[bp:] for m in mods], norms, lw, consts, cache, fin, l, depth, bufs_s)
        st_p.append(sp)
        st_s.append(ss)
    def outputs(states, bufs):
        stacked = [jnp.stack([s[i] for s in states], axis=0) if states[0][i] is not None else None
                   for i in range(7)]
        for i, all_layers in zip((3, 4), bufs[1]):
            stacked[i] = all_layers.reshape(all_layers.shape[:3] + (H_B, HEAD_DIM))
        return tuple(stacked)

    return (xp, xs) + outputs(st_p, bufs_p) + outputs(st_s, bufs_s)
```

```python
import functools
import math

import jax
import jax.numpy as jnp
from jax import lax
from jax.experimental import pallas as pl
from jax.experimental.pallas import tpu as pltpu

F32 = jnp.float32
BF16 = jnp.bfloat16
HIGHEST = lax.Precision.HIGHEST

CHUNK = 64
HEAD_DIM = 64
D_IDX = 64
H_A = 4
H_IDX = 4
H_B = 4
H_C = 8
C_BAND_CHUNKS = 8
C_BAND = C_BAND_CHUNKS * CHUNK
REL_CLIP = 256
N_BUCKETS = 32
T5_MAX_DIST = 1024
TOPK_MAX = 256
N_GROUPS = 4
EXP_PER_GROUP = 4
N_EXPERTS = N_GROUPS * EXP_PER_GROUP
RMS_EPS = 1e-6

NEG = -1e30
MIN_NORMAL_BITS = 0x00800000
ZERO_KEY_TOP = 1 << 14
INT_MIN = -(2 ** 31)
LANES = 128
VMEM_LIMIT = 56 * 1024 * 1024

DSA_TQ = 128
DSA_LC1 = 512
DSA_LC3 = 256
SB_T = 256
SB_DEAD = -104.0
BAND_TQ = 4 * CHUNK
MOE_EXPERTS_PER_STEP = 4
DSA_NEAR = -(-(DSA_LC3 - 1 + T5_MAX_DIST) // DSA_TQ)

_C_QA, _C_QI, _C_SM, _C_QB, _C_KB, _C_VB, _C_QC, _C_KC, _C_VC, _C_END = (
    0, 256, 512, 768, 1024, 1280, 1536, 2048, 2560, 3072)


def _cparams(sem):
    return pltpu.CompilerParams(dimension_semantics=sem, vmem_limit_bytes=VMEM_LIMIT)


def _nt_dot(a, b):
    return lax.dot_general(a, b, (((1,), (1,)), ((), ())), preferred_element_type=F32)


def _dot(a, b):
    return jnp.dot(a, b, preferred_element_type=F32)


def _col_reduce(x, op):
    r, c = x.shape
    group = 64 if r % 64 == 0 else 8
    if r > group:
        x = op(x.reshape(r // group, group, c), axis=0)
    return op(x, axis=0, keepdims=True)


def _bit_planes(words):
    a = list(words)
    j, m = 16, 0x0000FFFF
    while j:
        k = 0
        while k < 32:
            t = (a[k] ^ lax.shift_right_logical(a[k + j], j)) & m
            a[k] = a[k] ^ t
            a[k + j] = a[k + j] ^ lax.shift_left(t, j)
            k = (k + j + 1) & ~j
        j >>= 1
        m ^= (m << j) & 0xFFFFFFFF
    return [a[31 - b] for b in range(32)]


def _rms_mod(x, g, sc, sh):
    ms = jnp.mean(x * x, axis=-1, keepdims=True)
    return (x * lax.rsqrt(ms + RMS_EPS) * g) * (1.0 + sc) + sh


def _ada_kernel(c_ref, w_ref, b_ref, o_ref):
    c = c_ref[...]
    s = c * jax.nn.sigmoid(c)
    o_ref[0] = jnp.dot(s, w_ref[0], preferred_element_type=F32, precision=HIGHEST) + b_ref[0]


def _ada_mods(c_all, w_ada, b_ada):
    depth, d, e = w_ada.shape
    r = c_all.shape[0]
    tn = 1024
    return pl.pallas_call(
        _ada_kernel,
        grid=(depth, e // tn),
        in_specs=[pl.BlockSpec((r, d), lambda l, j: (0, 0)),
                  pl.BlockSpec((1, d, tn), lambda l, j: (l, 0, j)),
                  pl.BlockSpec((1, 1, tn), lambda l, j: (l, 0, j))],
        out_specs=pl.BlockSpec((1, r, tn), lambda l, j: (l, 0, j)),
        out_shape=jax.ShapeDtypeStruct((depth, r, e), F32),
        compiler_params=_cparams(("arbitrary", "arbitrary")),
        name="ada_mods",
    )(c_all, w_ada, b_ada.reshape(depth, 1, e))


def _proj_kernel(x_ref, sc_ref, sh_ref, g_ref, w_ref, *refs, n_tiles, n_state_tiles, n_alias):
    (h_ref, qa_ref, qi_ref, sm_ref, qb_ref, kbh_ref, vbh_ref, kb_ref, vb_ref,
     qc_ref, kch_ref, vch_ref, kcs_ref, vcs_ref) = refs[n_alias:n_alias + 14]
    a_state_refs, a_bf16_refs = refs[n_alias + 14:n_alias + 17], refs[n_alias + 17:]
    i = pl.program_id(1)
    h = _rms_mod(x_ref[0], g_ref[...], sc_ref[0], sh_ref[0])
    hb = h.astype(BF16)
    h_ref[0] = hb

    def mm(lo, hi):
        return _dot(hb, w_ref[:, lo:hi])

    def heads(ref, y, n, scale=None):
        for hh in range(n):
            part = y[:, hh * HEAD_DIM:(hh + 1) * HEAD_DIM]
            if scale is not None:
                part = part * scale
            ref[0, hh] = part.astype(BF16)

    qscale = HEAD_DIM ** -0.5
    heads(qa_ref, mm(_C_QA, _C_QI), H_A, qscale)
    heads(qi_ref, mm(_C_QI, _C_SM), H_IDX, D_IDX ** -0.5)
    small = mm(_C_SM, _C_QB)
    sm_ref[0] = small
    for j in range(3):
        part = small[:, j * HEAD_DIM:(j + 1) * HEAD_DIM]
        a_state_refs[j][0, 0] = part
        a_bf16_refs[j][0] = part.astype(BF16)
    heads(qb_ref, mm(_C_QB, _C_KB), H_B, qscale)
    kb = mm(_C_KB, _C_VB)
    kb_ref[0, 0] = kb
    heads(kbh_ref, kb, H_B)
    vb = mm(_C_VB, _C_QC)
    vb_ref[0, 0] = vb
    heads(vbh_ref, vb, H_B)
    heads(qc_ref, mm(_C_QC, _C_KC), H_C, qscale)
    kc = mm(_C_KC, _C_VC)
    heads(kch_ref, kc, H_C)
    vc = mm(_C_VC, _C_END)
    heads(vch_ref, vc, H_C)

    @pl.when(i >= n_tiles - n_state_tiles)
    def _():
        kcs_ref[0] = kc
        vcs_ref[0] = vc


def _proj(x, sc, sh, g, w_in_p, tm, layer, depth, band_bufs, stack_bufs):
    b, t, d = x.shape
    kpad = 0 if band_bufs is None else band_bufs[0].shape[2] - t
    n_tiles = t // tm
    n_state_tiles = min(C_BAND, t) // tm
    nbuf = n_state_tiles * tm
    assert kpad % tm == 0

    def hm(n, pad=0):
        return (pl.BlockSpec((1, n, tm, HEAD_DIM), lambda bi, i: (bi, 0, i + pad // tm, 0)),
                jax.ShapeDtypeStruct((b, n, pad + t, HEAD_DIM), BF16))

    def tok(width, dtype):
        return (pl.BlockSpec((1, tm, width), lambda bi, i: (bi, i, 0)),
                jax.ShapeDtypeStruct((b, t, width), dtype))

    def state(width):
        return (pl.BlockSpec((1, tm, width),
                             lambda bi, i: (bi, jnp.maximum(i - (n_tiles - n_state_tiles), 0), 0)),
                jax.ShapeDtypeStruct((b, nbuf, width), F32))

    def stacked(width):
        return (pl.BlockSpec((1, 1, tm, width), lambda bi, i: (layer, bi, i, 0)),
                jax.ShapeDtypeStruct((depth, b, t, width), F32))

    outs = [tok(d, BF16), hm(H_A), hm(H_IDX), tok(256, F32), hm(H_B), hm(H_B), hm(H_B),
            stacked(256), stacked(256), hm(H_C), hm(H_C, kpad), hm(H_C, kpad), state(512), state(512),
            stacked(HEAD_DIM), stacked(HEAD_DIM), stacked(HEAD_DIM),
            tok(HEAD_DIM, BF16), tok(HEAD_DIM, BF16), tok(HEAD_DIM, BF16)]
    vec = pl.BlockSpec((1, 1, d), lambda bi, i: (bi, 0, 0))
    in_specs = [pl.BlockSpec((1, tm, d), lambda bi, i: (bi, i, 0)), vec, vec,
                pl.BlockSpec((1, d), lambda bi, i: (0, 0)),
                pl.BlockSpec((d, _C_END), lambda bi, i: (0, 0))]
    args = [x, sc.reshape(b, 1, d), sh.reshape(b, 1, d), g.reshape(1, d), w_in_p]
    aliases = {}
    for bufs, out_idxs in ((band_bufs, (10, 11)), (stack_bufs, (7, 8, 14, 15, 16))):
        if bufs is not None:
            for buf, out_idx in zip(bufs, out_idxs):
                aliases[len(args)] = out_idx
                in_specs.append(pl.BlockSpec(memory_space=pl.ANY))
                args.append(buf)
    return pl.pallas_call(
        functools.partial(_proj_kernel, n_tiles=n_tiles, n_state_tiles=n_state_tiles, n_alias=len(aliases)),
        grid=(b, n_tiles),
        in_specs=in_specs,
        out_specs=[o[0] for o in outs],
        out_shape=[o[1] for o in outs],
        input_output_aliases=aliases,
        compiler_params=_cparams(("arbitrary", "arbitrary")),
        name="proj",
    )(*args)


def _dsa_kernel(qi_ref, qa_ref, wi_ref, ki_ref, ka_ref, vt_ref, bias_ref, o_ref, key_ref, plane_ref,
                *, qoff, l_valid, topk):
    tq, lc1, lc3 = DSA_TQ, DSA_LC1, DSA_LC3
    grp = 256
    n_rows = plane_ref.shape[1]
    i = pl.program_id(1)
    q0 = qoff + i * tq
    lane = lax.broadcasted_iota(jnp.int32, (1, tq), 1)
    qpos = q0 + lane
    lim = jnp.minimum((qpos // CHUNK + 1) * CHUNK, l_valid)
    n_adm = jnp.minimum(((q0 + tq - 1) // CHUNK + 1) * CHUNK, l_valid)
    nch1 = (n_adm + lc1 - 1) // lc1
    qi = qi_ref[0].reshape(H_IDX * tq, D_IDX)
    qa = qa_ref[0].reshape(H_A * tq, HEAD_DIM)
    wi = wi_ref[0]

    def score_chunk(c, carry):
        s0 = pl.multiple_of(c * lc1, lc1)
        s_all = _nt_dot(ki_ref[0, pl.ds(s0, lc1), :], qi)
        sc = None
        for h in range(H_IDX):
            term = wi[h:h + 1, :] * jnp.maximum(s_all[:, h * tq:(h + 1) * tq], 0.0)
            sc = term if sc is None else sc + term
        bits = lax.bitcast_convert_type(sc, jnp.int32)
        key = bits ^ (lax.shift_right_arithmetic(bits, 31) & 0x7FFFFFFF)
        sidx = s0 + lax.broadcasted_iota(jnp.int32, (lc1, tq), 0)
        tiny = (bits & 0x7FFFFFFF) < MIN_NORMAL_BITS
        key = jnp.where(tiny, jnp.where(bits < 0, -1 - sidx, ZERO_KEY_TOP - sidx), key)
        key = jnp.where(sidx < lim, key, INT_MIN)
        key_ref[pl.ds(s0, lc1), :] = key
        ukey = (key ^ INT_MIN).reshape(lc1 // 8, 8, tq)
        for g in range(lc1 // grp):
            planes = _bit_planes([ukey[g * 32 + r] for r in range(32)])
            row0 = pl.multiple_of((c * (lc1 // grp) + g) * 8, 8)
            for bit in range(32):
                plane_ref[bit, pl.ds(row0, 8), :] = planes[bit]
        return carry

    lax.fori_loop(0, nch1, score_chunk, 0)

    def count(*preds):
        def body(c, accs):
            s0 = pl.multiple_of(c * lc1, lc1)
            kk = key_ref[pl.ds(s0, lc1), :]
            sidx = s0 + lax.broadcasted_iota(jnp.int32, (lc1, tq), 0)
            return tuple(acc + jnp.sum(pred(kk, sidx).reshape(lc1 // 64, 64, tq), axis=0)
                         for acc, pred in zip(accs, preds))
        accs = lax.fori_loop(0, nch1, body, tuple(jnp.zeros((64, tq), F32) for _ in preds))
        return tuple(jnp.sum(acc, axis=0, keepdims=True) for acc in accs)

    kf = float(topk)
    row = lax.broadcasted_iota(jnp.int32, (n_rows, tq), 0)
    eq0 = jnp.where(row < nch1 * (lc1 // grp) * 8, -1, 0)

    def bit_step(it, carry):
        eq, n_gt, thr_u = carry
        bit = 31 - it
        w = eq & plane_ref[bit]
        n_set = _col_reduce(lax.population_count(w).astype(F32), jnp.sum)
        take = n_gt + n_set >= kf
        eq = jnp.where(take, w, eq ^ w)
        n_gt = jnp.where(take, n_gt, n_gt + n_set)
        thr_u = jnp.where(take, thr_u | lax.shift_left(jnp.int32(1), bit), thr_u)
        return eq, n_gt, thr_u

    eq, n_gt, thr_u = lax.fori_loop(
        0, 32, bit_step, (eq0, jnp.zeros((1, tq), F32), jnp.zeros((1, tq), jnp.int32)))
    n_eq = _col_reduce(lax.population_count(eq).astype(F32), jnp.sum)
    at_floor = (thr_u ^ INT_MIN) == INT_MIN
    thr = jnp.maximum(thr_u ^ INT_MIN, INT_MIN + 1)
    n_eq = jnp.where(at_floor, 0.0, n_eq)

    room = kf - n_gt
    idx_bits = 14
    assert l_valid < (1 << idx_bits) <= ZERO_KEY_TOP

    def tie_search():
        def step(it, end):
            cand = end + lax.shift_left(jnp.int32(1), idx_bits - 1 - it)
            n, = count(lambda kk, sidx: jnp.where(kk == thr, jnp.where(sidx < cand, 1.0, 0.0), 0.0))
            return jnp.where(n <= room, cand, end)
        return lax.fori_loop(0, idx_bits, step, jnp.zeros((1, tq), jnp.int32))

    tie_end = lax.cond(jnp.max(n_eq - room) > 0.0, tie_search,
                       lambda: jnp.full((1, tq), 1 << idx_bits, jnp.int32))

    n_sub = lc1 // lc3

    def logits_chunk(c):
        s0 = pl.multiple_of(c * lc3, lc3)
        lg = _nt_dot(ka_ref[0, pl.ds(s0, lc3), :], qa)
        kk = key_ref[pl.ds(s0, lc3), :]
        sidx = s0 + lax.broadcasted_iota(jnp.int32, (lc3, tq), 0)
        madd = jnp.where(kk > thr, 0.0,
                         jnp.where(kk == thr, jnp.where(sidx < tie_end, 0.0, NEG), NEG))
        d = jnp.clip((q0 - s0) // tq, 0, DSA_NEAR)
        return jnp.concatenate(
            [lg[:, h * tq:(h + 1) * tq] + bias_ref[d, h] + madd for h in range(H_A)], axis=1)

    def attend_step(c1, carry):
        m, l, acc = carry
        lgs = [logits_chunk(c1 * n_sub + sub) for sub in range(n_sub)]
        m_new = m
        for lgb in lgs:
            m_new = jnp.maximum(m_new, _col_reduce(lgb, jnp.max))
        alpha = jnp.exp(m - m_new)
        l = l * alpha
        acc = acc * alpha
        for sub, lgb in enumerate(lgs):
            p = jnp.exp(lgb - m_new)
            l = l + _col_reduce(p, jnp.sum)
            acc = acc + _dot(vt_ref[0, c1 * n_sub + sub], p.astype(BF16))
        return m_new, l, acc

    m0 = jnp.full((1, H_A * tq), NEG, F32)
    l0 = jnp.zeros((1, H_A * tq), F32)
    a0 = jnp.zeros((HEAD_DIM, H_A * tq), F32)
    _, l, acc = lax.fori_loop(0, nch1, attend_step, (m0, l0, a0))
    out = acc / l
    for h in range(H_A):
        o_ref[0, h] = out[:, h * tq:(h + 1) * tq].astype(BF16)


def _t5_bucket(rel):
    nb = N_BUCKETS // 2
    max_exact = nb // 2
    ret = jnp.where(rel > 0, nb, 0).astype(jnp.int32)
    n = jnp.abs(rel)
    n_f = jnp.maximum(n, 1).astype(F32)
    large = max_exact + (jnp.log(n_f / max_exact) / math.log(T5_MAX_DIST / max_exact)
                         * (nb - max_exact)).astype(jnp.int32)
    large = jnp.minimum(large, nb - 1)
    return ret + jnp.where(n < max_exact, n, large).astype(jnp.int32)


def _toeplitz(vec, n, m):
    length = n + m - 1
    lead = vec.shape[:-1]
    ring = jnp.concatenate([vec[..., m - 1::-1], vec[..., :m - 1:-1]], axis=-1)
    flat = jnp.tile(ring, (1,) * len(lead) + (n,))[..., :n * (length - 1)]
    return flat.reshape(lead + (n, length - 1))[..., :m]


def _dsa_bias_tiles(t5_table):
    d = jnp.arange(DSA_NEAR + 1)[:, None]
    k = jnp.arange(DSA_LC3 + DSA_TQ - 1)[None, :]
    rel = k - (DSA_TQ - 1) - DSA_TQ * d
    rel = jnp.where(d == DSA_NEAR, -T5_MAX_DIST, rel)
    vec = jnp.moveaxis(t5_table.astype(F32)[_t5_bucket(rel)], -1, 1)
    return _toeplitz(vec, DSA_LC3, DSA_TQ)


def _dsa(qi_hm, qa_hm, wi_t, ki, ka, va, bias_tiles, qoff, l_valid, topk):
    b, _, tq_all, _ = qi_hm.shape
    lp = ki.shape[1]
    nq = tq_all // DSA_TQ
    nc3 = lp // DSA_LC3
    vt = jnp.swapaxes(va.reshape(b, nc3, DSA_LC3, HEAD_DIM), 2, 3)
    qspec = pl.BlockSpec((1, H_A, DSA_TQ, HEAD_DIM), lambda bi, i: (bi, 0, i, 0))
    kspec = pl.BlockSpec((1, lp, HEAD_DIM), lambda bi, i: (bi, 0, 0))
    return pl.pallas_call(
        functools.partial(_dsa_kernel, qoff=qoff, l_valid=l_valid, topk=topk),
        grid=(b, nq),
        in_specs=[qspec, qspec,
                  pl.BlockSpec((1, H_IDX, DSA_TQ), lambda bi, i: (bi, 0, i)),
                  kspec, kspec,
                  pl.BlockSpec((1, nc3, HEAD_DIM, DSA_LC3), lambda bi, i: (bi, 0, 0, 0)),
                  pl.BlockSpec(bias_tiles.shape, lambda bi, i: (0, 0, 0, 0))],
        out_specs=pl.BlockSpec((1, H_A, HEAD_DIM, DSA_TQ), lambda bi, i: (bi, 0, 0, i)),
        out_shape=jax.ShapeDtypeStruct((b, H_A, HEAD_DIM, tq_all), BF16),
        scratch_shapes=[pltpu.VMEM((lp, DSA_TQ), jnp.int32), pltpu.VMEM((32, lp // 32, DSA_TQ), jnp.int32)],
        compiler_params=_cparams(("arbitrary", "arbitrary")),
        name="dsa",
    )(qi_hm, qa_hm, wi_t, ki, ka, vt, bias_tiles)


def _sb_kernel(q_ref, k_ref, v_ref, *refs, qoff, natural):
    t = SB_T
    if natural:
        kd_ref, vd_ref, u_ref, o_ref = refs
    else:
        u_ref, o_ref = refs
    n_heads = q_ref.shape[1]
    i = pl.program_id(1)
    u = u_ref[...]
    kb_diag = qoff // t + i
    row = lax.broadcasted_iota(jnp.int32, (t, t), 0)
    col = lax.broadcasted_iota(jnp.int32, (t, t), 1)
    causal = col < row

    def block(kb, carries, accs, diag):
        s0 = pl.multiple_of(kb * t, t)
        new_c, new_a = [], []
        for h in range(n_heads):
            if natural:
                cols = slice(h * HEAD_DIM, (h + 1) * HEAD_DIM)
                k = (kd_ref[0, :, cols] if diag else k_ref[0, pl.ds(s0, t), cols]).astype(BF16)
                v = (vd_ref[0, :, cols] if diag else v_ref[0, pl.ds(s0, t), cols]).astype(BF16)
            else:
                k = k_ref[0, h, pl.ds(s0, t), :]
                v = v_ref[0, h, pl.ds(s0, t), :]
            z = _nt_dot(q_ref[0, h], k)
            sp = jnp.maximum(z, 0.0) + jnp.log1p(jnp.exp(-jnp.abs(z)))
            lm = -sp
            if diag:
                lm = jnp.where(causal, lm, 0.0)
            hi = lm.astype(BF16)
            lo = (lm - hi.astype(F32)).astype(BF16)
            ext = _dot(hi, u) + _dot(lo, u)
            e = z - sp + ext[:, :t]
            a = jnp.exp(jnp.concatenate(
                [e[:, j * LANES:(j + 1) * LANES] + carries[h] for j in range(t // LANES)], axis=1))
            if diag:
                a = jnp.where(causal, a, 0.0)
            new_a.append(accs[h] + _dot(a.astype(BF16), v))
            new_c.append(carries[h] + ext[:, t:])
        return tuple(new_c), tuple(new_a)

    zeros_c = tuple(jnp.zeros((t, LANES), F32) for _ in range(n_heads))
    zeros_a = tuple(jnp.zeros((t, HEAD_DIM), F32) for _ in range(n_heads))
    carries, accs = block(kb_diag, zeros_c, zeros_a, True)

    def worst(cs):
        m = cs[0]
        for c in cs[1:]:
            m = jnp.maximum(m, c)
        return jnp.max(m)

    def cond(st):
        return jnp.logical_and(st[0] < kb_diag, st[1] > SB_DEAD)

    def body(st):
        j, _, cs, acs = st
        cs, acs = block(kb_diag - 1 - j, cs, acs, False)
        return j + 1, worst(cs), cs, acs

    _, _, _, accs = lax.while_loop(cond, body, (jnp.int32(0), worst(carries), carries, accs))
    for h in range(n_heads):
        o_ref[0, h] = accs[h].astype(BF16)


def _sb(q_hm, k, v, qoff, k_new=None, v_new=None):
    b, h, tq_all, _ = q_hm.shape
    t = SB_T
    natural = k_new is not None
    assert qoff % t == 0 and tq_all % t == 0
    jj = lax.broadcasted_iota(jnp.int32, (t, t + LANES), 0)
    ss = lax.broadcasted_iota(jnp.int32, (t, t + LANES), 1)
    u = jnp.logical_or(jj > ss, ss >= t).astype(BF16)
    qspec = pl.BlockSpec((1, h, t, HEAD_DIM), lambda bi, i: (bi, 0, i, 0))
    uspec = pl.BlockSpec((t, t + LANES), lambda bi, i: (0, 0))
    if natural:
        assert tq_all == t and k.shape[1:] == (qoff, h * HEAD_DIM) and k_new.shape[1:] == (t, h * HEAD_DIM)
        kspec = pl.BlockSpec((1, qoff, h * HEAD_DIM), lambda bi, i: (bi, 0, 0))
        dspec = pl.BlockSpec((1, t, h * HEAD_DIM), lambda bi, i: (bi, 0, 0))
        in_specs, args = [qspec, kspec, kspec, dspec, dspec, uspec], (q_hm, k, v, k_new, v_new, u)
    else:
        lp = k.shape[2]
        assert lp >= qoff + tq_all
        kspec = pl.BlockSpec((1, h, lp, HEAD_DIM), lambda bi, i: (bi, 0, 0, 0))
        in_specs, args = [qspec, kspec, kspec, uspec], (q_hm, k, v, u)
    return pl.pallas_call(
        functools.partial(_sb_kernel, qoff=qoff, natural=natural),
        grid=(b, tq_all // t),
        in_specs=in_specs,
        out_specs=qspec,
        out_shape=jax.ShapeDtypeStruct((b, h, tq_all, HEAD_DIM), BF16),
        compiler_params=_cparams(("arbitrary", "arbitrary")),
        name="sb",
    )(*args)


def _band_kernel(q_ref, k_ref, v_ref, bm_ref, o_ref, *, tq, w, n_invalid):
    i = pl.program_id(1)
    if n_invalid:
        col = lax.broadcasted_iota(jnp.int32, (1, w), 1)
        valid = jnp.where(i * tq + col >= n_invalid, 0.0, NEG)
    for h in range(q_ref.shape[1]):
        lg = _nt_dot(q_ref[0, h], k_ref[0, h]) + bm_ref[h]
        if n_invalid:
            lg = lg + valid
        m = jnp.max(lg, axis=1, keepdims=True)
        p = jnp.exp(lg - m)
        l = jnp.sum(p, axis=1, keepdims=True)
        o_ref[0, h] = (_dot(p.astype(BF16), v_ref[0, h]) / l).astype(BF16)


def _band_bias(rel_table, tq):
    w = C_BAND + tq
    t = jnp.arange(tq)[:, None]
    c = jnp.arange(w)[None, :] - C_BAND
    k = jnp.arange(tq + w - 1)
    rel = jnp.clip(w - 1 - C_BAND - k, -REL_CLIP, REL_CLIP) + REL_CLIP
    bias = _toeplitz(rel_table.astype(F32)[rel].T, tq, w)
    qc = t // CHUNK
    kc = jnp.floor_divide(c, CHUNK)
    mask = (kc <= qc) & (kc >= qc - C_BAND_CHUNKS)
    return jnp.where(mask[None], bias, NEG)


def _band(q_hm, k_ext, v_ext, bm, tq, n_invalid):
    b, h, t, _ = q_hm.shape
    w = C_BAND + tq
    kspec = pl.BlockSpec((pl.Element(1), pl.Element(h), pl.Element(w), pl.Element(HEAD_DIM)),
                         lambda bi, i: (bi, 0, i * tq, 0))
    qspec = pl.BlockSpec((1, h, tq, HEAD_DIM), lambda bi, i: (bi, 0, i, 0))
    return pl.pallas_call(
        functools.partial(_band_kernel, tq=tq, w=w, n_invalid=n_invalid),
        grid=(b, t // tq),
        in_specs=[qspec, kspec, kspec, pl.BlockSpec((h, tq, w), lambda bi, i: (0, 0, 0))],
        out_specs=qspec,
        out_shape=jax.ShapeDtypeStruct((b, h, t, HEAD_DIM), BF16),
        compiler_params=_cparams(("arbitrary", "arbitrary")),
        name="band",
    )(q_hm, k_ext, v_ext, bm)


def _merge_kernel(x_ref, h_ref, oa_ref, ob_ref, oc_ref, g1_ref, sc2_ref, sh2_ref, n2_ref,
                  wg_ref, bg_ref, wb_ref, wo_ref, wr_ref, br_ref,
                  xo_ref, h2_ref, gates_ref):
    d = x_ref.shape[-1]
    hb = h_ref[0]
    mix = None
    off = 0
    for j, o_ref in enumerate((oa_ref, ob_ref, oc_ref)):
        width = o_ref.shape[-1]
        y = _dot(o_ref[0], wb_ref[off:off + width, :])
        off += width
        g = jax.nn.sigmoid(_dot(hb, wg_ref[:, j * d:(j + 1) * d]) + bg_ref[:, j * d:(j + 1) * d])
        mix = g * y if mix is None else mix + g * y
    x = x_ref[0] + g1_ref[0] * _dot(mix.astype(BF16), wo_ref[...])
    xo_ref[0] = x
    h2 = _rms_mod(x, n2_ref[...], sc2_ref[0], sh2_ref[0])
    h2_ref[0] = h2.astype(BF16)

    h2_hi = h2.astype(BF16)
    h2_lo = (h2 - h2_hi.astype(F32)).astype(BF16)
    wr_hi, wr_lo = wr_ref[0], wr_ref[1]
    lr = _dot(h2_hi, wr_hi) + (_dot(h2_lo, wr_hi) + _dot(h2_hi, wr_lo)) + br_ref[...]
    tm = lr.shape[0]
    lane = lax.broadcasted_iota(jnp.int32, (tm, LANES), 1)
    lanef = lane.astype(F32)
    ninf = -jnp.inf
    is_expert = lane < N_EXPERTS
    lg = jnp.where(is_expert, ninf, jnp.where(lane < N_EXPERTS + N_GROUPS, lr, ninf))
    eg = jnp.exp(lg - jnp.max(lg, axis=1, keepdims=True))
    pg = eg / jnp.sum(eg, axis=1, keepdims=True)
    pg_top = jnp.max(pg, axis=1, keepdims=True)
    g_top = jnp.min(jnp.where(pg == pg_top, lanef, float(LANES)), axis=1, keepdims=True) - float(N_EXPERTS)
    in_group = (lane // EXP_PER_GROUP).astype(F32) == g_top
    le = jnp.where(is_expert, jnp.where(in_group, lr, ninf), ninf)
    m1 = jnp.max(le, axis=1, keepdims=True)
    i1 = jnp.min(jnp.where(le == m1, lanef, float(LANES)), axis=1, keepdims=True)
    le2 = jnp.where(lanef == i1, ninf, le)
    m2 = jnp.max(le2, axis=1, keepdims=True)
    i2 = jnp.min(jnp.where(le2 == m2, lanef, float(LANES)), axis=1, keepdims=True)
    e2 = jnp.exp(m2 - m1)
    den = 1.0 + e2
    gates_ref[0] = (jnp.where(lanef == i1, pg_top / den, 0.0)
                    + jnp.where(lanef == i2, pg_top * e2 / den, 0.0))


def _merge(x, h, oa, ob, oc, g1, sc2, sh2, n2, wg, bg, wb, wo, wr, br, tm):
    b, t, d = x.shape

    def tok(width):
        return pl.BlockSpec((1, tm, width), lambda bi, i: (bi, i, 0))

    vec = pl.BlockSpec((1, 1, d), lambda bi, i: (bi, 0, 0))

    def full(a):
        return pl.BlockSpec(a.shape, lambda bi, i: (0,) * a.ndim)

    n2r, bgr = n2.reshape(1, d), bg.reshape(1, 3 * d)
    return pl.pallas_call(
        _merge_kernel,
        grid=(b, t // tm),
        in_specs=[tok(d), tok(d), tok(oa.shape[-1]), tok(ob.shape[-1]), tok(oc.shape[-1]),
                  vec, vec, vec, full(n2r), full(wg), full(bgr), full(wb), full(wo), full(wr), full(br)],
        out_specs=[tok(d), tok(d), tok(LANES)],
        out_shape=[jax.ShapeDtypeStruct((b, t, d), F32), jax.ShapeDtypeStruct((b, t, d), BF16),
                   jax.ShapeDtypeStruct((b, t, LANES), F32)],
        compiler_params=_cparams(("arbitrary", "arbitrary")),
        name="merge",
    )(x, h, oa, ob, oc, g1.reshape(b, 1, d), sc2.reshape(b, 1, d), sh2.reshape(b, 1, d),
      n2r, wg, bgr, wb, wo, wr, br)


def _moe_kernel(x_ref, h2_ref, gates_ref, g2_ref, w1_ref, w3_ref, w2_ref, *refs, final):
    fin_ref = refs[0] if final else None
    o_ref, acc_ref = refs[-2:]
    step = pl.program_id(2)
    n_per = w1_ref.shape[0]

    @pl.when(step == 0)
    def _():
        acc_ref[...] = jnp.zeros_like(acc_ref)

    hb = h2_ref[0]
    gates = gates_ref[0]
    lane = lax.broadcasted_iota(jnp.int32, gates.shape, 1)
    total = None
    for j in range(n_per):
        a = _dot(hb, w1_ref[j])
        bb = _dot(hb, w3_ref[j])
        u = (a * jax.nn.sigmoid(a)) * bb
        out = _dot(u.astype(BF16), w2_ref[j])
        ge = jnp.sum(jnp.where(lane == step * n_per + j, gates, 0.0), axis=1, keepdims=True)
        total = ge * out if total is None else total + ge * out
    acc_ref[...] += total

    @pl.when(step == pl.num_programs(2) - 1)
    def _():
        y = x_ref[0] + g2_ref[0] * acc_ref[...]
        if final:
            y = y * lax.rsqrt(jnp.mean(y * y, axis=-1, keepdims=True) + RMS_EPS) * fin_ref[...]
        o_ref[0] = y


def _moe(x, h2, gates, g2, w1, w3, w2, tm, final_gain=None):
    b, t, d = x.shape
    ne, _, f = w1.shape
    n_per = MOE_EXPERTS_PER_STEP
    tok = lambda width: pl.BlockSpec((1, tm, width), lambda bi, i, e: (bi, i, 0))
    g2spec = tok(d) if g2.shape[1] == t else pl.BlockSpec((1, 1, d), lambda bi, i, e: (bi, 0, 0))
    in_specs = [tok(d), tok(d), tok(LANES), g2spec,
                pl.BlockSpec((n_per, d, f), lambda bi, i, e: (e, 0, 0)),
                pl.BlockSpec((n_per, d, f), lambda bi, i, e: (e, 0, 0)),
                pl.BlockSpec((n_per, f, d), lambda bi, i, e: (e, 0, 0))]
    args = [x, h2, gates, g2, w1, w3, w2]
    if final_gain is not None:
        in_specs.append(pl.BlockSpec((1, d), lambda bi, i, e: (0, 0)))
        args.append(final_gain.reshape(1, d))
    return pl.pallas_call(
        functools.partial(_moe_kernel, final=final_gain is not None),
        grid=(b, t // tm, ne // n_per),
        in_specs=in_specs,
        out_specs=tok(d),
        out_shape=jax.ShapeDtypeStruct((b, t, d), F32),
        scratch_shapes=[pltpu.VMEM((tm, d), F32)],
        compiler_params=_cparams(("arbitrary", "arbitrary", "arbitrary")),
        name="moe",
    )(*args)


def _prep_layer_weights(w_in, w_gate, w_branch, w_out, w_rg, b_rg, w_re, b_re, w1, w3, w2):
    d = w_in.shape[0]
    qa, ka, va, qi, ki, wi, qb, kb, vb, qc, kc, vc = _split_in(w_in)
    pad = jnp.zeros((d, 256 - (64 * 3 + H_IDX)), w_in.dtype)
    w_in_p = jnp.concatenate([qa, qi, ka, va, ki, wi, pad, qb, kb, vb, qc, kc, vc], axis=1).astype(BF16)
    wr = jnp.zeros((d, LANES), F32).at[:, :N_EXPERTS].set(w_re).at[:, N_EXPERTS:N_EXPERTS + N_GROUPS].set(w_rg)
    br = jnp.zeros((1, LANES), F32).at[0, :N_EXPERTS].set(b_re).at[0, N_EXPERTS:N_EXPERTS + N_GROUPS].set(b_rg)
    wr_hi = wr.astype(BF16)
    wr = jnp.stack([wr_hi, (wr - wr_hi.astype(F32)).astype(BF16)])
    return (w_in_p, w_gate.astype(BF16), w_branch.astype(BF16), w_out.astype(BF16), wr, br,
            w1.astype(BF16), w3.astype(BF16), w2.astype(BF16))


def _split_in(w_in):
    sizes = (256, 64, 64, 256, 64, H_IDX, 256, 256, 256, 512, 512, 512)
    out, start = [], 0
    for n in sizes:
        out.append(w_in[:, start:start + n])
        start += n
    return out


def _from_hm(o_hm):
    b, h, t, dh = o_hm.shape
    return jnp.swapaxes(o_hm, 1, 2).reshape(b, t, h * dh)


def _to_hm(a, dtype):
    return jnp.swapaxes(a, 1, 2).astype(dtype)


def _pad_axis(a, axis, size):
    if a.shape[axis] == size:
        return a
    widths = [(0, 0)] * a.ndim
    widths[axis] = (0, size - a.shape[axis])
    return jnp.pad(a, widths)


def _round_up(n, m):
    return -(-n // m) * m


def _layer(x, mods, norms, lw, consts, cache, final_gain, layer, depth, bufs):
    n1, n2 = norms
    sh1, sc1, g1, sh2, sc2, g2 = mods
    w_in_p, wg, bg, wb, wo, wr, br, w1, w3, w2 = lw
    dsa_bias, band_bias = consts
    b, t, d = x.shape
    tm = min(512, t)

    (h, qa_hm, qi_hm, sm, qb_hm, kb_hm, vb_hm, kb_all, vb_all, qc_hm, kc_hm, vc_hm, kcs, vcs,
     ka_all, va_all, ki_all, ka, va, ki) = _proj(x, sc1, sh1, n1, w_in_p, tm, layer, depth, *bufs)
    bufs = ((kc_hm, vc_hm) if bufs[0] is not None else None, (kb_all, vb_all, ka_all, va_all, ki_all))
    wi_t = jnp.swapaxes(sm[..., 192:192 + H_IDX], 1, 2) * (H_IDX ** -0.5)

    if cache is None:
        qoff, tq_pad = 0, t
        ka_f, va_f, ki_f = ka, va, ki
        kc_f, vc_f = kc_hm, vc_hm
        band_tq, n_invalid = BAND_TQ, C_BAND
        state = (None, None, None, None, None,
                 kcs.reshape(b, -1, H_C, HEAD_DIM), vcs.reshape(b, -1, H_C, HEAD_DIM))
    else:
        ca_k, ca_v, ca_ki, cb_k, cb_v, cc_k, cc_v = cache
        qoff = ca_k.shape[1]
        tq_pad = _round_up(t, DSA_TQ)
        ka_f = jnp.concatenate([ca_k.astype(BF16), ka], axis=1)
        va_f = jnp.concatenate([ca_v.astype(BF16), va], axis=1)
        ki_f = jnp.concatenate([ca_ki.astype(BF16), ki], axis=1)
        kc_f = jnp.concatenate([_to_hm(cc_k, BF16), kc_hm], axis=2)
        vc_f = jnp.concatenate([_to_hm(cc_v, BF16), vc_hm], axis=2)
        band_tq, n_invalid = CHUNK, 0
        kc_new = kcs.reshape(b, t, H_C, HEAD_DIM)
        vc_new = vcs.reshape(b, t, H_C, HEAD_DIM)
        state = (None, None, None, None, None,
                 jnp.concatenate([cc_k, kc_new], axis=1)[:, t:],
                 jnp.concatenate([cc_v, vc_new], axis=1)[:, t:])

    l_valid = ka_f.shape[1]
    topk = min(TOPK_MAX, l_valid // 4)
    lp = _round_up(l_valid, DSA_LC1)
    oa_t = _dsa(_pad_axis(qi_hm, 2, tq_pad), _pad_axis(qa_hm, 2, tq_pad), _pad_axis(wi_t, 2, tq_pad),
                _pad_axis(ki_f, 1, lp), _pad_axis(ka_f, 1, lp), _pad_axis(va_f, 1, lp),
                dsa_bias, qoff, l_valid, topk)
    oa = jnp.transpose(oa_t[..., :t], (0, 3, 1, 2)).reshape(b, t, H_A * HEAD_DIM)

    tq_sb = _round_up(t, SB_T)
    if cache is None:
        ob_hm = _sb(qb_hm, kb_hm, vb_hm, 0)
    else:
        ob_hm = _sb(_pad_axis(qb_hm, 2, tq_sb), cb_k.reshape(b, qoff, -1), cb_v.reshape(b, qoff, -1), qoff,
                    _pad_axis(kb_all[layer], 1, tq_sb), _pad_axis(vb_all[layer], 1, tq_sb))
    ob = _from_hm(ob_hm[:, :, :t])

    oc = _from_hm(_band(qc_hm, kc_f, vc_f, band_bias[band_tq], band_tq, n_invalid))

    x, h2, gates = _merge(x, h, oa, ob, oc, g1, sc2, sh2, n2, wg, bg, wb, wo, wr, br, min(512, t))
    if cache is None:
        x = _moe(x, h2, gates, g2.reshape(b, 1, d), w1, w3, w2, min(512, t), final_gain)
    else:
        flat = lambda a: a.reshape(1, b * t, a.shape[-1])
        g2_rows = jnp.broadcast_to(g2[:, None, :], (b, t, d))
        x = _moe(flat(x), flat(h2), flat(gates), flat(g2_rows), w1, w3, w2, min(512, b * t),
                 final_gain).reshape(b, t, d)
    return x, state, bufs


def kernel(x_prompt, x_sample, c_prompt, c_sample, cache_a_k, cache_a_v, cache_a_kidx, cache_b_k, cache_b_v, cache_c_k, cache_c_v, norm1, norm2, final_norm, w_ada, b_ada, w_in, t5_table, rel_c, w_gate, b_gate, w_branch, w_out, w_rg, b_rg, w_re, b_re, w1, w3, w2):
    depth = norm1.shape[0]
    bp = x_prompt.shape[0]
    mods_all = _ada_mods(jnp.concatenate([c_prompt, c_sample], axis=0), w_ada, b_ada)
    dsa_bias = _dsa_bias_tiles(t5_table)
    xp, xs = x_prompt, x_sample
    st_p, st_s = [], []
    band_shape = (bp, H_C, C_BAND + xp.shape[1], HEAD_DIM)
    bufs_p = ((jnp.zeros(band_shape, BF16), jnp.zeros(band_shape, BF16)), None)
    bufs_s = (None, None)
    for l in range(depth):
        lw = _prep_layer_weights(w_in[l], w_gate[l], w_branch[l], w_out[l], w_rg[l], b_rg[l],
                                 w_re[l], b_re[l], w1[l], w3[l], w2[l])
        lw = lw[:2] + (b_gate[l],) + lw[2:]
        band_bias = {tq: _band_bias(rel_c[l], tq) for tq in (CHUNK, BAND_TQ)}
        consts = (dsa_bias, band_bias)
        norms = (norm1[l], norm2[l])
        mods = jnp.split(mods_all[l], 6, axis=-1)
        fin = final_norm if l == depth - 1 else None
        xp, sp, bufs_p = _layer(xp, [m[:bp] for m in mods], norms, lw, consts, None, fin, l, depth, bufs_p)
        cache = (cache_a_k[l], cache_a_v[l], cache_a_kidx[l], cache_b_k[l], cache_b_v[l],
                 cache_c_k[l], cache_c_v[l])
        xs, ss, bufs_s = _layer(xs, [m[bp:] for m in mods], norms, lw, consts, cache, fin, l, depth, bufs_s)
        st_p.append(sp)
        st_s.append(ss)
    def outputs(states, bufs):
        stacked = [jnp.stack([s[i] for s in states], axis=0) if states[0][i] is not None else None
                   for i in range(7)]
        kb_all, vb_all, ka_all, va_all, ki_all = bufs[1]
        stacked[:5] = [ka_all, va_all, ki_all] + [a.reshape(a.shape[:3] + (H_B, HEAD_DIM)) for a in (kb_all, vb_all)]
        return tuple(stacked)

    return (xp, xs) + outputs(st_p, bufs_p) + outputs(st_s, bufs_s)
```

```python
import functools
import math

import jax
import jax.numpy as jnp
from jax import lax
from jax.experimental import pallas as pl
from jax.experimental.pallas import tpu as pltpu

F32 = jnp.float32
BF16 = jnp.bfloat16
HIGHEST = lax.Precision.HIGHEST

CHUNK = 64
HEAD_DIM = 64
D_IDX = 64
H_A = 4
H_IDX = 4
H_B = 4
H_C = 8
C_BAND_CHUNKS = 8
C_BAND = C_BAND_CHUNKS * CHUNK
REL_CLIP = 256
N_BUCKETS = 32
T5_MAX_DIST = 1024
TOPK_MAX = 256
N_GROUPS = 4
EXP_PER_GROUP = 4
N_EXPERTS = N_GROUPS * EXP_PER_GROUP
RMS_EPS = 1e-6

NEG = -1e30
MIN_NORMAL_BITS = 0x00800000
ZERO_KEY_TOP = 1 << 14
INT_MIN = -(2 ** 31)
LANES = 128
VMEM_LIMIT = 56 * 1024 * 1024

DSA_TQ = 128
DSA_LC1 = 512
DSA_LC3 = 256
SB_T = 256
SB_DEAD = -104.0
BAND_TQ = 4 * CHUNK
MOE_EXPERTS_PER_STEP = 4
DSA_NEAR = -(-(DSA_LC3 - 1 + T5_MAX_DIST) // DSA_TQ)

_C_QA, _C_QI, _C_SM, _C_QB, _C_KB, _C_VB, _C_QC, _C_KC, _C_VC, _C_END = (
    0, 256, 512, 768, 1024, 1280, 1536, 2048, 2560, 3072)


def _cparams(sem):
    return pltpu.CompilerParams(dimension_semantics=sem, vmem_limit_bytes=VMEM_LIMIT)


def _nt_dot(a, b):
    return lax.dot_general(a, b, (((1,), (1,)), ((), ())), preferred_element_type=F32)


def _dot(a, b):
    return jnp.dot(a, b, preferred_element_type=F32)


def _col_reduce(x, op):
    r, c = x.shape
    group = 64 if r % 64 == 0 else 8
    if r > group:
        x = op(x.reshape(r // group, group, c), axis=0)
    return op(x, axis=0, keepdims=True)


def _bit_planes(words):
    a = list(words)
    j, m = 16, 0x0000FFFF
    while j:
        k = 0
        while k < 32:
            t = (a[k] ^ lax.shift_right_logical(a[k + j], j)) & m
            a[k] = a[k] ^ t
            a[k + j] = a[k + j] ^ lax.shift_left(t, j)
            k = (k + j + 1) & ~j
        j >>= 1
        m ^= (m << j) & 0xFFFFFFFF
    return [a[31 - b] for b in range(32)]


def _rms_mod(x, g, sc, sh):
    ms = jnp.mean(x * x, axis=-1, keepdims=True)
    return (x * lax.rsqrt(ms + RMS_EPS) * g) * (1.0 + sc) + sh


def _ada_kernel(c_ref, w_ref, b_ref, o_ref):
    c = c_ref[...]
    s = c * jax.nn.sigmoid(c)
    o_ref[0] = jnp.dot(s, w_ref[0], preferred_element_type=F32, precision=HIGHEST) + b_ref[0]


def _ada_mods(c_all, w_ada, b_ada):
    depth, d, e = w_ada.shape
    r = c_all.shape[0]
    tn = 1024
    return pl.pallas_call(
        _ada_kernel,
        grid=(depth, e // tn),
        in_specs=[pl.BlockSpec((r, d), lambda l, j: (0, 0)),
                  pl.BlockSpec((1, d, tn), lambda l, j: (l, 0, j)),
                  pl.BlockSpec((1, 1, tn), lambda l, j: (l, 0, j))],
        out_specs=pl.BlockSpec((1, r, tn), lambda l, j: (l, 0, j)),
        out_shape=jax.ShapeDtypeStruct((depth, r, e), F32),
        compiler_params=_cparams(("arbitrary", "arbitrary")),
        name="ada_mods",
    )(c_all, w_ada, b_ada.reshape(depth, 1, e))


def _proj_kernel(x_ref, sc_ref, sh_ref, g_ref, w_ref, *refs, n_tiles, n_state_tiles, n_alias):
    (h_ref, qa_ref, qi_ref, sm_ref, qb_ref, kbh_ref, vbh_ref, kb_ref, vb_ref,
     qc_ref, kch_ref, vch_ref, kcs_ref, vcs_ref) = refs[n_alias:n_alias + 14]
    a_state_refs, a_bf16_refs = refs[n_alias + 14:n_alias + 17], refs[n_alias + 17:]
    i = pl.program_id(1)
    h = _rms_mod(x_ref[0], g_ref[...], sc_ref[0], sh_ref[0])
    hb = h.astype(BF16)
    h_ref[0] = hb

    def mm(lo, hi):
        return _dot(hb, w_ref[:, lo:hi])

    def heads(ref, y, n, scale=None):
        for hh in range(n):
            part = y[:, hh * HEAD_DIM:(hh + 1) * HEAD_DIM]
            if scale is not None:
                part = part * scale
            ref[0, hh] = part.astype(BF16)

    qscale = HEAD_DIM ** -0.5
    heads(qa_ref, mm(_C_QA, _C_QI), H_A, qscale)
    heads(qi_ref, mm(_C_QI, _C_SM), H_IDX, D_IDX ** -0.5)
    small = mm(_C_SM, _C_QB)
    sm_ref[0] = small
    for j in range(3):
        part = small[:, j * HEAD_DIM:(j + 1) * HEAD_DIM]
        a_state_refs[j][0, 0] = part
        a_bf16_refs[j][0] = part.astype(BF16)
    heads(qb_ref, mm(_C_QB, _C_KB), H_B, qscale)
    kb = mm(_C_KB, _C_VB)
    kb_ref[0, 0] = kb
    heads(kbh_ref, kb, H_B)
    vb = mm(_C_VB, _C_QC)
    vb_ref[0, 0] = vb
    heads(vbh_ref, vb, H_B)
    heads(qc_ref, mm(_C_QC, _C_KC), H_C, qscale)
    kc = mm(_C_KC, _C_VC)
    heads(kch_ref, kc, H_C)
    vc = mm(_C_VC, _C_END)
    heads(vch_ref, vc, H_C)

    @pl.when(i >= n_tiles - n_state_tiles)
    def _():
        kcs_ref[0] = kc
        vcs_ref[0] = vc


def _proj(x, sc, sh, g, w_in_p, tm, layer, depth, band_bufs, stack_bufs):
    b, t, d = x.shape
    kpad = 0 if band_bufs is None else band_bufs[0].shape[2] - t
    n_tiles = t // tm
    n_state_tiles = min(C_BAND, t) // tm
    nbuf = n_state_tiles * tm
    assert kpad % tm == 0

    def hm(n, pad=0):
        return (pl.BlockSpec((1, n, tm, HEAD_DIM), lambda bi, i: (bi, 0, i + pad // tm, 0)),
                jax.ShapeDtypeStruct((b, n, pad + t, HEAD_DIM), BF16))

    def tok(width, dtype):
        return (pl.BlockSpec((1, tm, width), lambda bi, i: (bi, i, 0)),
                jax.ShapeDtypeStruct((b, t, width), dtype))

    def state(width):
        return (pl.BlockSpec((1, tm, width),
                             lambda bi, i: (bi, jnp.maximum(i - (n_tiles - n_state_tiles), 0), 0)),
                jax.ShapeDtypeStruct((b, nbuf, width), F32))

    def stacked(width):
        return (pl.BlockSpec((1, 1, tm, width), lambda bi, i: (layer, bi, i, 0)),
                jax.ShapeDtypeStruct((depth, b, t, width), F32))

    outs = [tok(d, BF16), hm(H_A), hm(H_IDX), tok(256, F32), hm(H_B), hm(H_B), hm(H_B),
            stacked(256), stacked(256), hm(H_C), hm(H_C, kpad), hm(H_C, kpad), state(512), state(512),
            stacked(HEAD_DIM), stacked(HEAD_DIM), stacked(HEAD_DIM),
            tok(HEAD_DIM, BF16), tok(HEAD_DIM, BF16), tok(HEAD_DIM, BF16)]
    vec = pl.BlockSpec((1, 1, d), lambda bi, i: (bi, 0, 0))
    in_specs = [pl.BlockSpec((1, tm, d), lambda bi, i: (bi, i, 0)), vec, vec,
                pl.BlockSpec((1, d), lambda bi, i: (0, 0)),
                pl.BlockSpec((d, _C_END), lambda bi, i: (0, 0))]
    args = [x, sc.reshape(b, 1, d), sh.reshape(b, 1, d), g.reshape(1, d), w_in_p]
    aliases = {}
    for bufs, out_idxs in ((band_bufs, (10, 11)), (stack_bufs, (7, 8, 14, 15, 16))):
        if bufs is not None:
            for buf, out_idx in zip(bufs, out_idxs):
                aliases[len(args)] = out_idx
                in_specs.append(pl.BlockSpec(memory_space=pl.ANY))
                args.append(buf)
    return pl.pallas_call(
        functools.partial(_proj_kernel, n_tiles=n_tiles, n_state_tiles=n_state_tiles, n_alias=len(aliases)),
        grid=(b, n_tiles),
        in_specs=in_specs,
        out_specs=[o[0] for o in outs],
        out_shape=[o[1] for o in outs],
        input_output_aliases=aliases,
        compiler_params=_cparams(("arbitrary", "arbitrary")),
        name="proj",
    )(*args)


def _dsa_kernel(qi_ref, qa_ref, wi_ref, ki_ref, ka_ref, vt_ref, bias_ref, o_ref, key_ref, plane_ref,
                *, qoff, l_valid, topk):
    tq, lc1, lc3 = DSA_TQ, DSA_LC1, DSA_LC3
    grp = 256
    n_rows = plane_ref.shape[1]
    i = pl.program_id(1)
    q0 = qoff + i * tq
    lane = lax.broadcasted_iota(jnp.int32, (1, tq), 1)
    qpos = q0 + lane
    lim = jnp.minimum((qpos // CHUNK + 1) * CHUNK, l_valid)
    n_adm = jnp.minimum(((q0 + tq - 1) // CHUNK + 1) * CHUNK, l_valid)
    nch1 = (n_adm + lc1 - 1) // lc1
    qi = qi_ref[0].reshape(H_IDX * tq, D_IDX)
    qa = qa_ref[0].reshape(H_A * tq, HEAD_DIM)
    wi = wi_ref[0]

    def score_chunk(c, carry):
        s0 = pl.multiple_of(c * lc1, lc1)
        s_all = _nt_dot(ki_ref[0, pl.ds(s0, lc1), :], qi)
        sc = None
        for h in range(H_IDX):
            term = wi[h:h + 1, :] * jnp.maximum(s_all[:, h * tq:(h + 1) * tq], 0.0)
            sc = term if sc is None else sc + term
        bits = lax.bitcast_convert_type(sc, jnp.int32)
        key = bits ^ (lax.shift_right_arithmetic(bits, 31) & 0x7FFFFFFF)
        sidx = s0 + lax.broadcasted_iota(jnp.int32, (lc1, tq), 0)
        tiny = (bits & 0x7FFFFFFF) < MIN_NORMAL_BITS
        key = jnp.where(tiny, jnp.where(bits < 0, -1 - sidx, ZERO_KEY_TOP - sidx), key)
        key = jnp.where(sidx < lim, key, INT_MIN)
        key_ref[pl.ds(s0, lc1), :] = key
        ukey = (key ^ INT_MIN).reshape(lc1 // 8, 8, tq)
        for g in range(lc1 // grp):
            planes = _bit_planes([ukey[g * 32 + r] for r in range(32)])
            row0 = pl.multiple_of((c * (lc1 // grp) + g) * 8, 8)
            for bit in range(32):
                plane_ref[bit, pl.ds(row0, 8), :] = planes[bit]
        return carry

    lax.fori_loop(0, nch1, score_chunk, 0)

    def count(*preds):
        def body(c, accs):
            s0 = pl.multiple_of(c * lc1, lc1)
            kk = key_ref[pl.ds(s0, lc1), :]
            sidx = s0 + lax.broadcasted_iota(jnp.int32, (lc1, tq), 0)
            return tuple(acc + jnp.sum(pred(kk, sidx).reshape(lc1 // 64, 64, tq), axis=0)
                         for acc, pred in zip(accs, preds))
        accs = lax.fori_loop(0, nch1, body, tuple(jnp.zeros((64, tq), F32) for _ in preds))
        return tuple(jnp.sum(acc, axis=0, keepdims=True) for acc in accs)

    kf = float(topk)
    row = lax.broadcasted_iota(jnp.int32, (n_rows, tq), 0)
    eq0 = jnp.where(row < nch1 * (lc1 // grp) * 8, -1, 0)

    def bit_step(it, carry):
        eq, n_gt, thr_u = carry
        bit = 31 - it
        w = eq & plane_ref[bit]
        n_set = _col_reduce(lax.population_count(w).astype(F32), jnp.sum)
        take = n_gt + n_set >= kf
        eq = jnp.where(take, w, eq ^ w)
        n_gt = jnp.where(take, n_gt, n_gt + n_set)
        thr_u = jnp.where(take, thr_u | lax.shift_left(jnp.int32(1), bit), thr_u)
        return eq, n_gt, thr_u

    eq, n_gt, thr_u = lax.fori_loop(
        0, 32, bit_step, (eq0, jnp.zeros((1, tq), F32), jnp.zeros((1, tq), jnp.int32)))
    n_eq = _col_reduce(lax.population_count(eq).astype(F32), jnp.sum)
    at_floor = (thr_u ^ INT_MIN) == INT_MIN
    thr = jnp.maximum(thr_u ^ INT_MIN, INT_MIN + 1)
    n_eq = jnp.where(at_floor, 0.0, n_eq)

    room = kf - n_gt
    idx_bits = 14
    assert l_valid < (1 << idx_bits) <= ZERO_KEY_TOP

    def tie_search():
        def step(it, end):
            cand = end + lax.shift_left(jnp.int32(1), idx_bits - 1 - it)
            n, = count(lambda kk, sidx: jnp.where(kk == thr, jnp.where(sidx < cand, 1.0, 0.0), 0.0))
            return jnp.where(n <= room, cand, end)
        return lax.fori_loop(0, idx_bits, step, jnp.zeros((1, tq), jnp.int32))

    tie_end = lax.cond(jnp.max(n_eq - room) > 0.0, tie_search,
                       lambda: jnp.full((1, tq), 1 << idx_bits, jnp.int32))

    n_sub = lc1 // lc3

    def logits_chunk(c):
        s0 = pl.multiple_of(c * lc3, lc3)
        lg = _nt_dot(ka_ref[0, pl.ds(s0, lc3), :], qa)
        kk = key_ref[pl.ds(s0, lc3), :]
        sidx = s0 + lax.broadcasted_iota(jnp.int32, (lc3, tq), 0)
        madd = jnp.where(kk > thr, 0.0,
                         jnp.where(kk == thr, jnp.where(sidx < tie_end, 0.0, NEG), NEG))
        d = jnp.clip((q0 - s0) // tq, 0, DSA_NEAR)
        return jnp.concatenate(
            [lg[:, h * tq:(h + 1) * tq] + bias_ref[d, h] + madd for h in range(H_A)], axis=1)

    def attend_step(c1, carry):
        m, l, acc = carry
        lgs = [logits_chunk(c1 * n_sub + sub) for sub in range(n_sub)]
        m_new = m
        for lgb in lgs:
            m_new = jnp.maximum(m_new, _col_reduce(lgb, jnp.max))
        alpha = jnp.exp(m - m_new)
        l = l * alpha
        acc = acc * alpha
        for sub, lgb in enumerate(lgs):
            p = jnp.exp(lgb - m_new)
            l = l + _col_reduce(p, jnp.sum)
            acc = acc + _dot(vt_ref[0, c1 * n_sub + sub], p.astype(BF16))
        return m_new, l, acc

    m0 = jnp.full((1, H_A * tq), NEG, F32)
    l0 = jnp.zeros((1, H_A * tq), F32)
    a0 = jnp.zeros((HEAD_DIM, H_A * tq), F32)
    _, l, acc = lax.fori_loop(0, nch1, attend_step, (m0, l0, a0))
    out = acc / l
    for h in range(H_A):
        o_ref[0, h] = out[:, h * tq:(h + 1) * tq].astype(BF16)


def _t5_bucket(rel):
    nb = N_BUCKETS // 2
    max_exact = nb // 2
    ret = jnp.where(rel > 0, nb, 0).astype(jnp.int32)
    n = jnp.abs(rel)
    n_f = jnp.maximum(n, 1).astype(F32)
    large = max_exact + (jnp.log(n_f / max_exact) / math.log(T5_MAX_DIST / max_exact)
                         * (nb - max_exact)).astype(jnp.int32)
    large = jnp.minimum(large, nb - 1)
    return ret + jnp.where(n < max_exact, n, large).astype(jnp.int32)


def _toeplitz(vec, n, m):
    length = n + m - 1
    lead = vec.shape[:-1]
    ring = jnp.concatenate([vec[..., m - 1::-1], vec[..., :m - 1:-1]], axis=-1)
    flat = jnp.tile(ring, (1,) * len(lead) + (n,))[..., :n * (length - 1)]
    return flat.reshape(lead + (n, length - 1))[..., :m]


def _dsa_bias_tiles(t5_table):
    d = jnp.arange(DSA_NEAR + 1)[:, None]
    k = jnp.arange(DSA_LC3 + DSA_TQ - 1)[None, :]
    rel = k - (DSA_TQ - 1) - DSA_TQ * d
    rel = jnp.where(d == DSA_NEAR, -T5_MAX_DIST, rel)
    vec = jnp.moveaxis(t5_table.astype(F32)[_t5_bucket(rel)], -1, 1)
    return _toeplitz(vec, DSA_LC3, DSA_TQ)


def _dsa(qi_hm, qa_hm, wi_t, ki, ka, va, bias_tiles, qoff, l_valid, topk):
    b, _, tq_all, _ = qi_hm.shape
    lp = ki.shape[1]
    nq = tq_all // DSA_TQ
    nc3 = lp // DSA_LC3
    vt = jnp.swapaxes(va.reshape(b, nc3, DSA_LC3, HEAD_DIM), 2, 3)
    qspec = pl.BlockSpec((1, H_A, DSA_TQ, HEAD_DIM), lambda bi, i: (bi, 0, i, 0))
    kspec = pl.BlockSpec((1, lp, HEAD_DIM), lambda bi, i: (bi, 0, 0))
    return pl.pallas_call(
        functools.partial(_dsa_kernel, qoff=qoff, l_valid=l_valid, topk=topk),
        grid=(b, nq),
        in_specs=[qspec, qspec,
                  pl.BlockSpec((1, H_IDX, DSA_TQ), lambda bi, i: (bi, 0, i)),
                  kspec, kspec,
                  pl.BlockSpec((1, nc3, HEAD_DIM, DSA_LC3), lambda bi, i: (bi, 0, 0, 0)),
                  pl.BlockSpec(bias_tiles.shape, lambda bi, i: (0, 0, 0, 0))],
        out_specs=pl.BlockSpec((1, H_A, HEAD_DIM, DSA_TQ), lambda bi, i: (bi, 0, 0, i)),
        out_shape=jax.ShapeDtypeStruct((b, H_A, HEAD_DIM, tq_all), BF16),
        scratch_shapes=[pltpu.VMEM((lp, DSA_TQ), jnp.int32), pltpu.VMEM((32, lp // 32, DSA_TQ), jnp.int32)],
        compiler_params=_cparams(("arbitrary", "arbitrary")),
        name="dsa",
    )(qi_hm, qa_hm, wi_t, ki, ka, vt, bias_tiles)


def _sb_kernel(q_ref, k_ref, v_ref, *refs, qoff, natural):
    t = SB_T
    if natural:
        kd_ref, vd_ref, u_ref, o_ref = refs
    else:
        u_ref, o_ref = refs
    n_heads = q_ref.shape[1]
    i = pl.program_id(1)
    u = u_ref[...]
    kb_diag = qoff // t + i
    row = lax.broadcasted_iota(jnp.int32, (t, t), 0)
    col = lax.broadcasted_iota(jnp.int32, (t, t), 1)
    causal = col < row

    def block(kb, carries, accs, diag):
        s0 = pl.multiple_of(kb * t, t)
        def kv(ref, dref, h):
            if natural:
                cols = slice(h * HEAD_DIM, (h + 1) * HEAD_DIM)
                return (dref[0, :, cols] if diag else ref[0, pl.ds(s0, t), cols]).astype(BF16)
            return ref[0, h, pl.ds(s0, t), :]

        heads = range(n_heads)
        zs = [_nt_dot(q_ref[0, h], kv(k_ref, kd_ref if natural else None, h)) for h in heads]
        sps = [jnp.maximum(z, 0.0) + jnp.log1p(jnp.exp(-jnp.abs(z))) for z in zs]
        lms = [jnp.where(causal, -sp, 0.0) if diag else -sp for sp in sps]
        exts = []
        for lm in lms:
            hi = lm.astype(BF16)
            lo = (lm - hi.astype(F32)).astype(BF16)
            exts.append(_dot(hi, u) + _dot(lo, u))
        new_c, new_a = [], []
        for h in heads:
            e = zs[h] - sps[h] + exts[h][:, :t]
            a = jnp.exp(jnp.concatenate(
                [e[:, j * LANES:(j + 1) * LANES] + carries[h] for j in range(t // LANES)], axis=1))
            if diag:
                a = jnp.where(causal, a, 0.0)
            new_a.append(accs[h] + _dot(a.astype(BF16), kv(v_ref, vd_ref if natural else None, h)))
            new_c.append(carries[h] + exts[h][:, t:])
        return tuple(new_c), tuple(new_a)

    zeros_c = tuple(jnp.zeros((t, LANES), F32) for _ in range(n_heads))
    zeros_a = tuple(jnp.zeros((t, HEAD_DIM), F32) for _ in range(n_heads))
    carries, accs = block(kb_diag, zeros_c, zeros_a, True)

    def worst(cs):
        m = cs[0]
        for c in cs[1:]:
            m = jnp.maximum(m, c)
        return jnp.max(m)

    def cond(st):
        return jnp.logical_and(st[0] < kb_diag, st[1] > SB_DEAD)

    def body(st):
        j, _, cs, acs = st
        cs, acs = block(kb_diag - 1 - j, cs, acs, False)
        return j + 1, worst(cs), cs, acs

    _, _, _, accs = lax.while_loop(cond, body, (jnp.int32(0), worst(carries), carries, accs))
    for h in range(n_heads):
        o_ref[0, h] = accs[h].astype(BF16)


def _sb(q_hm, k, v, qoff, k_new=None, v_new=None):
    b, h, tq_all, _ = q_hm.shape
    t = SB_T
    natural = k_new is not None
    assert qoff % t == 0 and tq_all % t == 0
    jj = lax.broadcasted_iota(jnp.int32, (t, t + LANES), 0)
    ss = lax.broadcasted_iota(jnp.int32, (t, t + LANES), 1)
    u = jnp.logical_or(jj > ss, ss >= t).astype(BF16)
    qspec = pl.BlockSpec((1, h, t, HEAD_DIM), lambda bi, i: (bi, 0, i, 0))
    uspec = pl.BlockSpec((t, t + LANES), lambda bi, i: (0, 0))
    if natural:
        assert tq_all == t and k.shape[1:] == (qoff, h * HEAD_DIM) and k_new.shape[1:] == (t, h * HEAD_DIM)
        kspec = pl.BlockSpec((1, qoff, h * HEAD_DIM), lambda bi, i: (bi, 0, 0))
        dspec = pl.BlockSpec((1, t, h * HEAD_DIM), lambda bi, i: (bi, 0, 0))
        in_specs, args = [qspec, kspec, kspec, dspec, dspec, uspec], (q_hm, k, v, k_new, v_new, u)
    else:
        lp = k.shape[2]
        assert lp >= qoff + tq_all
        kspec = pl.BlockSpec((1, h, lp, HEAD_DIM), lambda bi, i: (bi, 0, 0, 0))
        in_specs, args = [qspec, kspec, kspec, uspec], (q_hm, k, v, u)
    return pl.pallas_call(
        functools.partial(_sb_kernel, qoff=qoff, natural=natural),
        grid=(b, tq_all // t),
        in_specs=in_specs,
        out_specs=qspec,
        out_shape=jax.ShapeDtypeStruct((b, h, tq_all, HEAD_DIM), BF16),
        compiler_params=_cparams(("arbitrary", "arbitrary")),
        name="sb",
    )(*args)


def _band_kernel(q_ref, k_ref, v_ref, bm_ref, o_ref, *, tq, w, n_invalid):
    i = pl.program_id(1)
    if n_invalid:
        col = lax.broadcasted_iota(jnp.int32, (1, w), 1)
        valid = jnp.where(i * tq + col >= n_invalid, 0.0, NEG)
    n_heads = q_ref.shape[1]
    lgs = []
    for h in range(n_heads):
        lg = _nt_dot(q_ref[0, h], k_ref[0, h]) + bm_ref[h]
        lgs.append(lg + valid if n_invalid else lg)
    ms = [jnp.max(lg, axis=1, keepdims=True) for lg in lgs]
    ps = [jnp.exp(lg - m) for lg, m in zip(lgs, ms)]
    ls = [jnp.sum(p, axis=1, keepdims=True) for p in ps]
    for h in range(n_heads):
        o_ref[0, h] = (_dot(ps[h].astype(BF16), v_ref[0, h]) / ls[h]).astype(BF16)


def _band_bias(rel_table, tq):
    w = C_BAND + tq
    t = jnp.arange(tq)[:, None]
    c = jnp.arange(w)[None, :] - C_BAND
    k = jnp.arange(tq + w - 1)
    rel = jnp.clip(w - 1 - C_BAND - k, -REL_CLIP, REL_CLIP) + REL_CLIP
    bias = _toeplitz(rel_table.astype(F32)[rel].T, tq, w)
    qc = t // CHUNK
    kc = jnp.floor_divide(c, CHUNK)
    mask = (kc <= qc) & (kc >= qc - C_BAND_CHUNKS)
    return jnp.where(mask[None], bias, NEG)


def _band(q_hm, k_ext, v_ext, bm, tq, n_invalid):
    b, h, t, _ = q_hm.shape
    w = C_BAND + tq
    kspec = pl.BlockSpec((pl.Element(1), pl.Element(h), pl.Element(w), pl.Element(HEAD_DIM)),
                         lambda bi, i: (bi, 0, i * tq, 0))
    qspec = pl.BlockSpec((1, h, tq, HEAD_DIM), lambda bi, i: (bi, 0, i, 0))
    return pl.pallas_call(
        functools.partial(_band_kernel, tq=tq, w=w, n_invalid=n_invalid),
        grid=(b, t // tq),
        in_specs=[qspec, kspec, kspec, pl.BlockSpec((h, tq, w), lambda bi, i: (0, 0, 0))],
        out_specs=qspec,
        out_shape=jax.ShapeDtypeStruct((b, h, t, HEAD_DIM), BF16),
        compiler_params=_cparams(("arbitrary", "arbitrary")),
        name="band",
    )(q_hm, k_ext, v_ext, bm)


def _merge_kernel(x_ref, h_ref, oa_ref, ob_ref, oc_ref, g1_ref, sc2_ref, sh2_ref, n2_ref,
                  wg_ref, bg_ref, wb_ref, wo_ref, wr_ref, br_ref,
                  xo_ref, h2_ref, gates_ref):
    d = x_ref.shape[-1]
    hb = h_ref[0]
    mix = None
    off = 0
    for j, o_ref in enumerate((oa_ref, ob_ref, oc_ref)):
        width = o_ref.shape[-1]
        y = _dot(o_ref[0], wb_ref[off:off + width, :])
        off += width
        g = jax.nn.sigmoid(_dot(hb, wg_ref[:, j * d:(j + 1) * d]) + bg_ref[:, j * d:(j + 1) * d])
        mix = g * y if mix is None else mix + g * y
    x = x_ref[0] + g1_ref[0] * _dot(mix.astype(BF16), wo_ref[...])
    xo_ref[0] = x
    h2 = _rms_mod(x, n2_ref[...], sc2_ref[0], sh2_ref[0])
    h2_ref[0] = h2.astype(BF16)

    h2_hi = h2.astype(BF16)
    h2_lo = (h2 - h2_hi.astype(F32)).astype(BF16)
    wr_hi, wr_lo = wr_ref[0], wr_ref[1]
    lr = _dot(h2_hi, wr_hi) + (_dot(h2_lo, wr_hi) + _dot(h2_hi, wr_lo)) + br_ref[...]
    tm = lr.shape[0]
    lane = lax.broadcasted_iota(jnp.int32, (tm, LANES), 1)
    lanef = lane.astype(F32)
    ninf = -jnp.inf
    is_expert = lane < N_EXPERTS
    lg = jnp.where(is_expert, ninf, jnp.where(lane < N_EXPERTS + N_GROUPS, lr, ninf))
    eg = jnp.exp(lg - jnp.max(lg, axis=1, keepdims=True))
    pg = eg / jnp.sum(eg, axis=1, keepdims=True)
    pg_top = jnp.max(pg, axis=1, keepdims=True)
    g_top = jnp.min(jnp.where(pg == pg_top, lanef, float(LANES)), axis=1, keepdims=True) - float(N_EXPERTS)
    in_group = (lane // EXP_PER_GROUP).astype(F32) == g_top
    le = jnp.where(is_expert, jnp.where(in_group, lr, ninf), ninf)
    m1 = jnp.max(le, axis=1, keepdims=True)
    i1 = jnp.min(jnp.where(le == m1, lanef, float(LANES)), axis=1, keepdims=True)
    le2 = jnp.where(lanef == i1, ninf, le)
    m2 = jnp.max(le2, axis=1, keepdims=True)
    i2 = jnp.min(jnp.where(le2 == m2, lanef, float(LANES)), axis=1, keepdims=True)
    e2 = jnp.exp(m2 - m1)
    den = 1.0 + e2
    gates_ref[0] = (jnp.where(lanef == i1, pg_top / den, 0.0)
                    + jnp.where(lanef == i2, pg_top * e2 / den, 0.0))


def _merge(x, h, oa, ob, oc, g1, sc2, sh2, n2, wg, bg, wb, wo, wr, br, tm):
    b, t, d = x.shape

    def tok(width):
        return pl.BlockSpec((1, tm, width), lambda bi, i: (bi, i, 0))

    vec = pl.BlockSpec((1, 1, d), lambda bi, i: (bi, 0, 0))

    def full(a):
        return pl.BlockSpec(a.shape, lambda bi, i: (0,) * a.ndim)

    n2r, bgr = n2.reshape(1, d), bg.reshape(1, 3 * d)
    return pl.pallas_call(
        _merge_kernel,
        grid=(b, t // tm),
        in_specs=[tok(d), tok(d), tok(oa.shape[-1]), tok(ob.shape[-1]), tok(oc.shape[-1]),
                  vec, vec, vec, full(n2r), full(wg), full(bgr), full(wb), full(wo), full(wr), full(br)],
        out_specs=[tok(d), tok(d), tok(LANES)],
        out_shape=[jax.ShapeDtypeStruct((b, t, d), F32), jax.ShapeDtypeStruct((b, t, d), BF16),
                   jax.ShapeDtypeStruct((b, t, LANES), F32)],
        compiler_params=_cparams(("arbitrary", "arbitrary")),
        name="merge",
    )(x, h, oa, ob, oc, g1.reshape(b, 1, d), sc2.reshape(b, 1, d), sh2.reshape(b, 1, d),
      n2r, wg, bgr, wb, wo, wr, br)


def _moe_kernel(x_ref, h2_ref, gates_ref, g2_ref, w1_ref, w3_ref, w2_ref, *refs, final):
    fin_ref = refs[0] if final else None
    o_ref, acc_ref = refs[-2:]
    step = pl.program_id(2)
    n_per = w1_ref.shape[0]

    @pl.when(step == 0)
    def _():
        acc_ref[...] = jnp.zeros_like(acc_ref)

    hb = h2_ref[0]
    gates = gates_ref[0]
    lane = lax.broadcasted_iota(jnp.int32, gates.shape, 1)
    ups = [(_dot(hb, w1_ref[j]), _dot(hb, w3_ref[j])) for j in range(n_per)]
    acts = [((a * jax.nn.sigmoid(a)) * bb).astype(BF16) for a, bb in ups]
    total = None
    for j in range(n_per):
        out = _dot(acts[j], w2_ref[j])
        ge = jnp.sum(jnp.where(lane == step * n_per + j, gates, 0.0), axis=1, keepdims=True)
        total = ge * out if total is None else total + ge * out
    acc_ref[...] += total

    @pl.when(step == pl.num_programs(2) - 1)
    def _():
        y = x_ref[0] + g2_ref[0] * acc_ref[...]
        if final:
            y = y * lax.rsqrt(jnp.mean(y * y, axis=-1, keepdims=True) + RMS_EPS) * fin_ref[...]
        o_ref[0] = y


def _moe(x, h2, gates, g2, w1, w3, w2, tm, final_gain=None):
    b, t, d = x.shape
    ne, _, f = w1.shape
    n_per = MOE_EXPERTS_PER_STEP
    tok = lambda width: pl.BlockSpec((1, tm, width), lambda bi, i, e: (bi, i, 0))
    g2spec = tok(d) if g2.shape[1] == t else pl.BlockSpec((1, 1, d), lambda bi, i, e: (bi, 0, 0))
    in_specs = [tok(d), tok(d), tok(LANES), g2spec,
                pl.BlockSpec((n_per, d, f), lambda bi, i, e: (e, 0, 0)),
                pl.BlockSpec((n_per, d, f), lambda bi, i, e: (e, 0, 0)),
                pl.BlockSpec((n_per, f, d), lambda bi, i, e: (e, 0, 0))]
    args = [x, h2, gates, g2, w1, w3, w2]
    if final_gain is not None:
        in_specs.append(pl.BlockSpec((1, d), lambda bi, i, e: (0, 0)))
        args.append(final_gain.reshape(1, d))
    return pl.pallas_call(
        functools.partial(_moe_kernel, final=final_gain is not None),
        grid=(b, t // tm, ne // n_per),
        in_specs=in_specs,
        out_specs=tok(d),
        out_shape=jax.ShapeDtypeStruct((b, t, d), F32),
        scratch_shapes=[pltpu.VMEM((tm, d), F32)],
        compiler_params=_cparams(("arbitrary", "arbitrary", "arbitrary")),
        name="moe",
    )(*args)


def _prep_layer_weights(w_in, w_gate, w_branch, w_out, w_rg, b_rg, w_re, b_re, w1, w3, w2):
    d = w_in.shape[0]
    qa, ka, va, qi, ki, wi, qb, kb, vb, qc, kc, vc = _split_in(w_in)
    pad = jnp.zeros((d, 256 - (64 * 3 + H_IDX)), w_in.dtype)
    w_in_p = jnp.concatenate([qa, qi, ka, va, ki, wi, pad, qb, kb, vb, qc, kc, vc], axis=1).astype(BF16)
    wr = jnp.zeros((d, LANES), F32).at[:, :N_EXPERTS].set(w_re).at[:, N_EXPERTS:N_EXPERTS + N_GROUPS].set(w_rg)
    br = jnp.zeros((1, LANES), F32).at[0, :N_EXPERTS].set(b_re).at[0, N_EXPERTS:N_EXPERTS + N_GROUPS].set(b_rg)
    wr_hi = wr.astype(BF16)
    wr = jnp.stack([wr_hi, (wr - wr_hi.astype(F32)).astype(BF16)])
    return (w_in_p, w_gate.astype(BF16), w_branch.astype(BF16), w_out.astype(BF16), wr, br,
            w1.astype(BF16), w3.astype(BF16), w2.astype(BF16))


def _split_in(w_in):
    sizes = (256, 64, 64, 256, 64, H_IDX, 256, 256, 256, 512, 512, 512)
    out, start = [], 0
    for n in sizes:
        out.append(w_in[:, start:start + n])
        start += n
    return out


def _from_hm(o_hm):
    b, h, t, dh = o_hm.shape
    return jnp.swapaxes(o_hm, 1, 2).reshape(b, t, h * dh)


def _to_hm(a, dtype):
    return jnp.swapaxes(a, 1, 2).astype(dtype)


def _pad_axis(a, axis, size):
    if a.shape[axis] == size:
        return a
    widths = [(0, 0)] * a.ndim
    widths[axis] = (0, size - a.shape[axis])
    return jnp.pad(a, widths)


def _round_up(n, m):
    return -(-n // m) * m


def _layer(x, mods, norms, lw, consts, cache, final_gain, layer, depth, bufs):
    n1, n2 = norms
    sh1, sc1, g1, sh2, sc2, g2 = mods
    w_in_p, wg, bg, wb, wo, wr, br, w1, w3, w2 = lw
    dsa_bias, band_bias = consts
    b, t, d = x.shape
    tm = min(512, t)

    (h, qa_hm, qi_hm, sm, qb_hm, kb_hm, vb_hm, kb_all, vb_all, qc_hm, kc_hm, vc_hm, kcs, vcs,
     ka_all, va_all, ki_all, ka, va, ki) = _proj(x, sc1, sh1, n1, w_in_p, tm, layer, depth, *bufs)
    bufs = ((kc_hm, vc_hm) if bufs[0] is not None else None, (kb_all, vb_all, ka_all, va_all, ki_all))
    wi_t = jnp.swapaxes(sm[..., 192:192 + H_IDX], 1, 2) * (H_IDX ** -0.5)

    if cache is None:
        qoff, tq_pad = 0, t
        ka_f, va_f, ki_f = ka, va, ki
        kc_f, vc_f = kc_hm, vc_hm
        band_tq, n_invalid = BAND_TQ, C_BAND
        state = (None, None, None, None, None,
                 kcs.reshape(b, -1, H_C, HEAD_DIM), vcs.reshape(b, -1, H_C, HEAD_DIM))
    else:
        ca_k, ca_v, ca_ki, cb_k, cb_v, cc_k, cc_v = cache
        qoff = ca_k.shape[1]
        tq_pad = _round_up(t, DSA_TQ)
        ka_f = jnp.concatenate([ca_k.astype(BF16), ka], axis=1)
        va_f = jnp.concatenate([ca_v.astype(BF16), va], axis=1)
        ki_f = jnp.concatenate([ca_ki.astype(BF16), ki], axis=1)
        kc_f = jnp.concatenate([_to_hm(cc_k, BF16), kc_hm], axis=2)
        vc_f = jnp.concatenate([_to_hm(cc_v, BF16), vc_hm], axis=2)
        band_tq, n_invalid = CHUNK, 0
        kc_new = kcs.reshape(b, t, H_C, HEAD_DIM)
        vc_new = vcs.reshape(b, t, H_C, HEAD_DIM)
        state = (None, None, None, None, None,
                 jnp.concatenate([cc_k, kc_new], axis=1)[:, t:],
                 jnp.concatenate([cc_v, vc_new], axis=1)[:, t:])

    l_valid = ka_f.shape[1]
    topk = min(TOPK_MAX, l_valid // 4)
    lp = _round_up(l_valid, DSA_LC1)
    oa_t = _dsa(_pad_axis(qi_hm, 2, tq_pad), _pad_axis(qa_hm, 2, tq_pad), _pad_axis(wi_t, 2, tq_pad),
                _pad_axis(ki_f, 1, lp), _pad_axis(ka_f, 1, lp), _pad_axis(va_f, 1, lp),
                dsa_bias, qoff, l_valid, topk)
    oa = jnp.transpose(oa_t[..., :t], (0, 3, 1, 2)).reshape(b, t, H_A * HEAD_DIM)

    tq_sb = _round_up(t, SB_T)
    if cache is None:
        ob_hm = _sb(qb_hm, kb_hm, vb_hm, 0)
    else:
        ob_hm = _sb(_pad_axis(qb_hm, 2, tq_sb), cb_k.reshape(b, qoff, -1), cb_v.reshape(b, qoff, -1), qoff,
                    _pad_axis(kb_all[layer], 1, tq_sb), _pad_axis(vb_all[layer], 1, tq_sb))
    ob = _from_hm(ob_hm[:, :, :t])

    oc = _from_hm(_band(qc_hm, kc_f, vc_f, band_bias[band_tq], band_tq, n_invalid))

    x, h2, gates = _merge(x, h, oa, ob, oc, g1, sc2, sh2, n2, wg, bg, wb, wo, wr, br, min(512, t))
    if cache is None:
        x = _moe(x, h2, gates, g2.reshape(b, 1, d), w1, w3, w2, min(512, t), final_gain)
    else:
        flat = lambda a: a.reshape(1, b * t, a.shape[-1])
        g2_rows = jnp.broadcast_to(g2[:, None, :], (b, t, d))
        x = _moe(flat(x), flat(h2), flat(gates), flat(g2_rows), w1, w3, w2, min(512, b * t),
                 final_gain).reshape(b, t, d)
    return x, state, bufs


def kernel(x_prompt, x_sample, c_prompt, c_sample, cache_a_k, cache_a_v, cache_a_kidx, cache_b_k, cache_b_v, cache_c_k, cache_c_v, norm1, norm2, final_norm, w_ada, b_ada, w_in, t5_table, rel_c, w_gate, b_gate, w_branch, w_out, w_rg, b_rg, w_re, b_re, w1, w3, w2):
    depth = norm1.shape[0]
    bp = x_prompt.shape[0]
    mods_all = _ada_mods(jnp.concatenate([c_prompt, c_sample], axis=0), w_ada, b_ada)
    dsa_bias = _dsa_bias_tiles(t5_table)
    xp, xs = x_prompt, x_sample
    st_p, st_s = [], []
    band_shape = (bp, H_C, C_BAND + xp.shape[1], HEAD_DIM)
    bufs_p = ((jnp.zeros(band_shape, BF16), jnp.zeros(band_shape, BF16)), None)
    bufs_s = (None, None)
    for l in range(depth):
        lw = _prep_layer_weights(w_in[l], w_gate[l], w_branch[l], w_out[l], w_rg[l], b_rg[l],
                                 w_re[l], b_re[l], w1[l], w3[l], w2[l])
        lw = lw[:2] + (b_gate[l],) + lw[2:]
        band_bias = {tq: _band_bias(rel_c[l], tq) for tq in (CHUNK, BAND_TQ)}
        consts = (dsa_bias, band_bias)
        norms = (norm1[l], norm2[l])
        mods = jnp.split(mods_all[l], 6, axis=-1)
        fin = final_norm if l == depth - 1 else None
        xp, sp, bufs_p = _layer(xp, [m[:bp] for m in mods], norms, lw, consts, None, fin, l, depth, bufs_p)
        cache = (cache_a_k[l], cache_a_v[l], cache_a_kidx[l], cache_b_k[l], cache_b_v[l],
                 cache_c_k[l], cache_c_v[l])
        xs, ss, bufs_s = _layer(xs, [m[bp:] for m in mods], norms, lw, consts, cache, fin, l, depth, bufs_s)
        st_p.append(sp)
        st_s.append(ss)
    def outputs(states, bufs):
        stacked = [jnp.stack([s[i] for s in states], axis=0) if states[0][i] is not None else None
                   for i in range(7)]
        kb_all, vb_all, ka_all, va_all, ki_all = bufs[1]
        stacked[:5] = [ka_all, va_all, ki_all] + [a.reshape(a.shape[:3] + (H_B, HEAD_DIM)) for a in (kb_all, vb_all)]
        return tuple(stacked)

    return (xp, xs) + outputs(st_p, bufs_p) + outputs(st_s, bufs_s)
```
